```python
import math
import jax, jax.numpy as jnp
from jax import lax
import numpy as np

D_MODEL = 2048
BATCH = 4
SEQ = 2048
DEPTH = 4
DEC_BATCH = 8
DEC_SEQ = 4
PAST_LEN = 16384
PAGE_SIZE = 128

N_HEADS = 8
MIX_WIDTH = D_MODEL // 2
DA_QK = MIX_WIDTH // (2 * N_HEADS)
DA_V = 2 * DA_QK
POOL_DIM = MIX_WIDTH
POOL_WINDOWS = (2, 4, 8, 16)
POOL_GROUP = POOL_DIM // len(POOL_WINDOWS)
POOL_STATE = max(POOL_WINDOWS) - 1
NSA_DH = MIX_WIDTH // N_HEADS
NSA_KV = 2
NSA_HPG = N_HEADS // NSA_KV
CMP_BLOCK = 64
SLC_BLOCK = 64
SLC_TOPK = 16
WINDOW = 512
CONV_DIM = MIX_WIDTH
CONV_WIDTH = 3
D_FF = 4 * D_MODEL
NUM_BUCKETS = 32
MAX_DISTANCE = 128
ATTN_QBLOCK = 128
SLC_QCHUNK = 32
WIN_QBLOCK = 128
N_EVEN = (DEPTH + 1) // 2
N_ODD = DEPTH // 2
EVEN_IN = 2 * N_HEADS * 2 * DA_QK + N_HEADS * DA_V + POOL_DIM
ODD_SIZES = (N_HEADS * NSA_DH, 2 * NSA_KV * NSA_DH, 2 * NSA_KV * NSA_DH, 2 * NSA_KV * NSA_DH, 3 * N_HEADS, CONV_DIM, CONV_DIM, CONV_DIM)
ODD_IN = sum(ODD_SIZES)
EPS = 1e-6
NEG = -1e30

kernel_name = 'hybrid_diffattn_pool_nsa_shortconv_step'


def _rmsnorm(x, g):
    xf = x.astype(jnp.float32)
    y = xf * lax.rsqrt(jnp.mean(xf * xf, axis=-1, keepdims=True) + EPS)
    return (y * g.astype(jnp.float32)).astype(x.dtype)


def _rel_bucket(dist):
    n = jnp.maximum(dist, 0)
    max_exact = NUM_BUCKETS // 2
    nf = jnp.maximum(n, 1).astype(jnp.float32)
    large = max_exact + (jnp.log(nf / max_exact) / math.log(MAX_DISTANCE / max_exact) * (NUM_BUCKETS - max_exact)).astype(jnp.int32)
    large = jnp.minimum(large, NUM_BUCKETS - 1)
    return jnp.where(n < max_exact, n, large)


def _rel_bias(dist, table):
    return table[_rel_bucket(dist)].astype(jnp.float32)


def _sqrelu_mlp(h, w1, w2):
    a = jnp.maximum(h @ w1, 0)
    return (a * a) @ w2


def _even_project(h, w_in):
    B, T, _ = h.shape
    z = h @ w_in
    nq = N_HEADS * 2 * DA_QK
    nv = N_HEADS * DA_V
    q = z[..., :nq].reshape(B, T, N_HEADS, 2, DA_QK)
    k = z[..., nq:2 * nq].reshape(B, T, N_HEADS, 2, DA_QK)
    v = z[..., 2 * nq:2 * nq + nv].reshape(B, T, N_HEADS, DA_V)
    u = z[..., 2 * nq + nv:]
    return q, k, v, u


def _diff_lambda(lv, lam_init):
    lv = lv.astype(jnp.float32)
    return jnp.exp(jnp.sum(lv[0] * lv[1])) - jnp.exp(jnp.sum(lv[2] * lv[3])) + lam_init


def _diff_attn(q, k, v, q_pos, k_pos, lam, lam_init, subln_g, table):
    logits = jnp.einsum('bthmd,blhmd->bhmtl', q, k).astype(jnp.float32) * (DA_QK ** -0.5)
    dist = q_pos[:, None] - k_pos[None, :]
    bias = jnp.moveaxis(_rel_bias(dist, table), -1, 0)
    logits = jnp.where(dist >= 0, logits + bias[None, :, None], -jnp.inf)
    p = jax.nn.softmax(logits, axis=-1)
    a = p[:, :, 0] - lam * p[:, :, 1]
    o = jnp.einsum('bhtl,blhd->bthd', a.astype(v.dtype), v)
    return _rmsnorm(o, subln_g) * (1.0 - lam_init)


def _pool_mix(u, prev, pos0, w_pool, scale):
    B, T, _ = u.shape
    ext = jnp.concatenate([prev.astype(u.dtype), u], axis=1).astype(jnp.float32)
    cs = jnp.concatenate([jnp.zeros((B, 1, POOL_DIM), jnp.float32), jnp.cumsum(ext, axis=1)], axis=1)
    pos = pos0 + jnp.arange(T)
    outs = []
    for g, w in enumerate(POOL_WINDOWS):
        c0, c1 = g * POOL_GROUP, (g + 1) * POOL_GROUP
        win_sum = cs[:, POOL_STATE + 1:POOL_STATE + 1 + T, c0:c1] - cs[:, POOL_STATE + 1 - w:POOL_STATE + 1 - w + T, c0:c1]
        cnt = jnp.minimum(pos + 1, w).astype(jnp.float32)[None, :, None]
        d = win_sum / cnt - ext[:, POOL_STATE:, c0:c1]
        outs.append(jnp.einsum('btc,cd->btd', d, w_pool[g].astype(jnp.float32)))
    y = jnp.concatenate(outs, axis=-1) * scale.astype(jnp.float32)
    return y.astype(u.dtype), ext[:, -POOL_STATE:].astype(u.dtype)


def _even_mixer_prompt(h, w_in, w_out, lam_vec, lam_init, subln_g, pool_w, pool_scale, table):
    B, S, _ = h.shape
    q, k, v, u = _even_project(h, w_in)
    lam = _diff_lambda(lam_vec, lam_init)
    pos = jnp.arange(S)
    nqb = S // ATTN_QBLOCK
    q_b = jnp.moveaxis(q.reshape(B, nqb, ATTN_QBLOCK, N_HEADS, 2, DA_QK), 1, 0)
    pos_b = pos.reshape(nqb, ATTN_QBLOCK)
    o = lax.map(lambda a: _diff_attn(a[0], k, v, a[1], pos, lam, lam_init, subln_g, table), (q_b, pos_b))
    o = jnp.moveaxis(o, 0, 1).reshape(B, S, N_HEADS * DA_V)
    o_pool, pool_state = _pool_mix(u, jnp.zeros((B, POOL_STATE, POOL_DIM), u.dtype), 0, pool_w, pool_scale)
    y = jnp.concatenate([o, o_pool], axis=-1) @ w_out
    kv_rows = jnp.stack([k.reshape(B, S, N_HEADS, 2 * DA_QK), v], axis=2)
    return y, kv_rows, pool_state


def _even_mixer_sample(h, cache_kv, e, page_table, pool_prev, w_in, w_out, lam_vec, lam_init, subln_g, pool_w, pool_scale, table):
    B, T, _ = h.shape
    q, k, v, u = _even_project(h, w_in)
    lam = _diff_lambda(lam_vec, lam_init)
    past = cache_kv[e, page_table].reshape(B, PAST_LEN, 2, N_HEADS, DA_V)
    k_all = jnp.concatenate([past[:, :, 0].reshape(B, PAST_LEN, N_HEADS, 2, DA_QK).astype(k.dtype), k], axis=1)
    v_all = jnp.concatenate([past[:, :, 1].astype(v.dtype), v], axis=1)
    q_pos = PAST_LEN + jnp.arange(T)
    k_pos = jnp.arange(PAST_LEN + T)
    o = _diff_attn(q, k_all, v_all, q_pos, k_pos, lam, lam_init, subln_g, table).reshape(B, T, N_HEADS * DA_V)
    o_pool, pool_state = _pool_mix(u, pool_prev, PAST_LEN, pool_w, pool_scale)
    y = jnp.concatenate([o, o_pool], axis=-1) @ w_out
    kv_rows = jnp.stack([k.reshape(B, T, N_HEADS, 2 * DA_QK), v], axis=2)
    return y, kv_rows, pool_state


def _odd_project(h, w_in):
    B, T, _ = h.shape
    z = h @ w_in
    offs = np.cumsum(ODD_SIZES)[:-1].tolist()
    q, kc, ks_, kw, gates, cb, cc, ch = jnp.split(z, offs, axis=-1)
    kv_shape = (B, T, 2, NSA_KV, NSA_DH)
    return (q.reshape(B, T, N_HEADS, NSA_DH), kc.reshape(kv_shape), ks_.reshape(kv_shape),
            kw.reshape(kv_shape), gates, cb, cc, ch)


def _cmp_attend(q, q_pos, kv_c, table):
    B, T = q.shape[:2]
    nb = kv_c.shape[1]
    qg = q.reshape(B, T, NSA_KV, NSA_HPG, NSA_DH)
    logits = jnp.einsum('btghd,bngd->btghn', qg, kv_c[:, :, 0]).astype(jnp.float32) * (NSA_DH ** -0.5)
    blk_end = (jnp.arange(nb) + 1) * CMP_BLOCK - 1
    dist = q_pos[:, None] - blk_end[None, :]
    valid = (dist >= 0)[:, None, None, :]
    bias = _rel_bias(dist, table).reshape(T, nb, NSA_KV, NSA_HPG).transpose(0, 2, 3, 1)
    p = jax.nn.softmax(jnp.where(valid, logits + bias, NEG), axis=-1)
    p = jnp.where(valid, p, 0.0)
    o = jnp.einsum('btghn,bngd->btghd', p.astype(q.dtype), kv_c[:, :, 1]).reshape(B, T, N_HEADS, NSA_DH)
    return o, jnp.sum(p, axis=3)


def _slc_attend(q, q_pos, score, gather_fn, table):
    B, T = q.shape[:2]
    nb = score.shape[-1]
    blk = jnp.arange(nb)[None, :]
    cur = (q_pos // SLC_BLOCK)[:, None]
    forced = ((blk == 0) | (blk == cur) | (blk == cur - 1))[None, :, None, :]
    future = (blk > cur)[None, :, None, :]
    s = jnp.where(forced, jnp.inf, jnp.where(future, -jnp.inf, score))
    kk = min(SLC_TOPK, nb)
    _, idx = lax.top_k(s, kk)
    kv = gather_fn(idx)
    kpos = idx[..., None] * SLC_BLOCK + jnp.arange(SLC_BLOCK)
    qg = q.reshape(B, T, NSA_KV, NSA_HPG, NSA_DH)
    logits = jnp.einsum('btghd,btgksd->btghks', qg, kv[..., 0, :]).astype(jnp.float32) * (NSA_DH ** -0.5)
    dist = q_pos[None, :, None, None, None] - kpos
    table_t = table.reshape(NUM_BUCKETS, NSA_KV, NSA_HPG).transpose(1, 0, 2).astype(jnp.float32)
    gi = jnp.arange(NSA_KV)[None, None, :, None, None]
    bias = jnp.moveaxis(table_t[gi, _rel_bucket(dist)], -1, 3)
    logits = jnp.where((dist >= 0)[:, :, :, None], logits + bias, -jnp.inf)
    p = jax.nn.softmax(logits.reshape(B, T, NSA_KV, NSA_HPG, kk * SLC_BLOCK), axis=-1)
    p = p.reshape(B, T, NSA_KV, NSA_HPG, kk, SLC_BLOCK)
    o = jnp.einsum('btghks,btgksd->btghd', p.astype(q.dtype), kv[..., 1, :])
    return o.reshape(B, T, N_HEADS, NSA_DH)


def _win_attend(q, q_pos, kv, k_pos, table):
    B, T = q.shape[:2]
    L = kv.shape[1]
    qg = q.reshape(B, T, NSA_KV, NSA_HPG, NSA_DH)
    logits = jnp.einsum('btghd,blgd->btghl', qg, kv[:, :, 0]).astype(jnp.float32) * (NSA_DH ** -0.5)
    dist = q_pos[:, None] - k_pos[None, :]
    valid = ((dist >= 0) & (dist < WINDOW) & (k_pos[None, :] >= 0))[:, None, None, :]
    bias = _rel_bias(dist, table).reshape(T, L, NSA_KV, NSA_HPG).transpose(0, 2, 3, 1)
    p = jax.nn.softmax(jnp.where(valid, logits + bias, -jnp.inf), axis=-1)
    o = jnp.einsum('btghl,blgd->btghd', p.astype(q.dtype), kv[:, :, 1])
    return o.reshape(B, T, N_HEADS, NSA_DH)


def _nsa_combine(gates, o_cmp, o_slc, o_win):
    B, T = gates.shape[:2]
    g = jax.nn.sigmoid(gates.astype(jnp.float32)).reshape(B, T, 3, N_HEADS, 1).astype(o_cmp.dtype)
    o = g[:, :, 0] * o_cmp + g[:, :, 1] * o_slc + g[:, :, 2] * o_win
    return o.reshape(B, T, N_HEADS * NSA_DH)


def _short_conv(z, prev, w):
    T = z.shape[1]
    ext = jnp.concatenate([prev.astype(z.dtype), z], axis=1)
    y = w[0] * ext[:, 0:T]
    for j in range(1, CONV_WIDTH):
        y = y + w[j] * ext[:, j:j + T]
    return y, ext[:, -(CONV_WIDTH - 1):]


def _odd_mixer_prompt(h, w_in, w_out, w_conv, table):
    B, S, _ = h.shape
    q, kv_cmp, kv_slc, kv_win, gates, cb, cc, ch = _odd_project(h, w_in)
    pos = jnp.arange(S)
    nb = S // SLC_BLOCK
    kv_c = kv_cmp.reshape(B, S // CMP_BLOCK, CMP_BLOCK, 2, NSA_KV, NSA_DH).mean(axis=2)
    o_cmp, p_grp = _cmp_attend(q, pos, kv_c, table)
    kvb = kv_slc.reshape(B, nb, SLC_BLOCK, 2, NSA_KV, NSA_DH).transpose(0, 4, 1, 2, 3, 5)
    bi = jnp.arange(B)[:, None, None, None]
    gi = jnp.arange(NSA_KV)[None, None, :, None]
    gather = lambda idx: kvb[bi, gi, idx]
    nc = S // SLC_QCHUNK
    q_c = jnp.moveaxis(q.reshape(B, nc, SLC_QCHUNK, N_HEADS, NSA_DH), 1, 0)
    pos_c = pos.reshape(nc, SLC_QCHUNK)
    s_c = jnp.moveaxis(p_grp.reshape(B, nc, SLC_QCHUNK, NSA_KV, nb), 1, 0)
    o_slc = lax.map(lambda a: _slc_attend(a[0], a[1], a[2], gather, table), (q_c, pos_c, s_c))
    o_slc = jnp.moveaxis(o_slc, 0, 1).reshape(B, S, N_HEADS, NSA_DH)
    nqb = S // WIN_QBLOCK
    span = WINDOW + WIN_QBLOCK
    kv_pad = jnp.pad(kv_win, ((0, 0), (WINDOW, 0), (0, 0), (0, 0), (0, 0)))
    rows = jnp.arange(nqb)[:, None] * WIN_QBLOCK + jnp.arange(span)[None, :]
    kv_bands = kv_pad[:, rows]
    o_win = jax.vmap(lambda qq, qp, kv_, kp: _win_attend(qq, qp, kv_, kp, table), in_axes=(1, 0, 1, 0), out_axes=1)(
        q.reshape(B, nqb, WIN_QBLOCK, N_HEADS, NSA_DH), pos.reshape(nqb, WIN_QBLOCK), kv_bands, rows - WINDOW)
    o_win = o_win.reshape(B, S, N_HEADS, NSA_DH)
    o_nsa = _nsa_combine(gates, o_cmp, o_slc, o_win)
    conv_out, conv_state = _short_conv(cc * ch, jnp.zeros((B, CONV_WIDTH - 1, CONV_DIM), h.dtype), w_conv)
    y = jnp.concatenate([o_nsa, cb * conv_out], axis=-1) @ w_out
    return y, kv_cmp, kv_slc, kv_win[:, -min(WINDOW, S):], conv_state


def _odd_mixer_sample(h, cache_cmp, cache_slc, o_idx, page_table, win_prev, conv_prev, w_in, w_out, w_conv, table):
    B, T, _ = h.shape
    q, kv_cmp, kv_slc, kv_win, gates, cb, cc, ch = _odd_project(h, w_in)
    q_pos = PAST_LEN + jnp.arange(T)
    npb = PAST_LEN // SLC_BLOCK
    nt = -(-T // SLC_BLOCK)
    pad = ((0, 0), (0, nt * SLC_BLOCK - T), (0, 0), (0, 0), (0, 0))
    past_c = cache_cmp[o_idx, page_table].reshape(B, npb, CMP_BLOCK, 2, NSA_KV, NSA_DH).mean(axis=2)
    new_c = jnp.pad(kv_cmp, pad).reshape(B, nt, CMP_BLOCK, 2, NSA_KV, NSA_DH).mean(axis=2)
    kv_c = jnp.concatenate([past_c.astype(kv_cmp.dtype), new_c], axis=1)
    o_cmp, p_grp = _cmp_attend(q, q_pos, kv_c, table)
    tail = jnp.pad(kv_slc, pad).reshape(B, nt, SLC_BLOCK, 2, NSA_KV, NSA_DH).transpose(0, 4, 1, 2, 3, 5)
    bpp = PAGE_SIZE // SLC_BLOCK
    bi = jnp.arange(B)[:, None, None, None]
    gi = jnp.arange(NSA_KV)[None, None, :, None]

    def gather(idx):
        in_past = idx < npb
        pidx = jnp.minimum(idx, npb - 1)
        page = page_table[bi, pidx // bpp]
        rows = (pidx % bpp)[..., None] * SLC_BLOCK + jnp.arange(SLC_BLOCK)
        past = cache_slc[o_idx, page[..., None], rows, :, gi[..., None]].astype(kv_slc.dtype)
        new = tail[bi, gi, jnp.clip(idx - npb, 0, nt - 1)]
        return jnp.where(in_past[..., None, None, None], past, new)

    o_slc = _slc_attend(q, q_pos, p_grp, gather, table)
    wbuf = win_prev.shape[1]
    kv_w = jnp.concatenate([win_prev.astype(kv_win.dtype), kv_win], axis=1)
    k_pos = PAST_LEN - wbuf + jnp.arange(wbuf + T)
    o_win = _win_attend(q, q_pos, kv_w, k_pos, table)
    o_nsa = _nsa_combine(gates, o_cmp, o_slc, o_win)
    conv_out, conv_state = _short_conv(cc * ch, conv_prev, w_conv)
    y = jnp.concatenate([o_nsa, cb * conv_out], axis=-1) @ w_out
    return y, kv_cmp, kv_slc, kv_w[:, -min(WINDOW, PAST_LEN + T):], conv_state


def setup_inputs(seed: int = 0) -> dict:
    key = jax.random.key(seed)
    ks = jax.random.split(key, 24)
    n_pages = PAST_LEN // PAGE_SIZE
    used = DEC_BATCH * n_pages
    n_phys = used + max(1, used // 4)
    wbuf = min(WINDOW, PAST_LEN)

    def nrm(k, shape, scale):
        return scale * jax.random.normal(k, shape, jnp.float32)

    page_table = jax.random.permutation(ks[8], n_phys)[:used].reshape(DEC_BATCH, n_pages).astype(jnp.int32)
    return {
        'x_prompt': nrm(ks[0], (BATCH, SEQ, D_MODEL), 1.0),
        'x_sample': nrm(ks[1], (DEC_BATCH, DEC_SEQ, D_MODEL), 1.0),
        'cache_diff_kv': nrm(ks[2], (N_EVEN, n_phys, PAGE_SIZE, 2, N_HEADS, DA_V), 1.0),
        'cache_cmp_kv': nrm(ks[3], (N_ODD, n_phys, PAGE_SIZE, 2, NSA_KV, NSA_DH), 1.0),
        'cache_slc_kv': nrm(ks[4], (N_ODD, n_phys, PAGE_SIZE, 2, NSA_KV, NSA_DH), 1.0),
        'state_win_kv': nrm(ks[5], (N_ODD, DEC_BATCH, wbuf, 2, NSA_KV, NSA_DH), 1.0),
        'state_pool': nrm(ks[6], (N_EVEN, DEC_BATCH, POOL_STATE, POOL_DIM), 1.0),
        'state_conv': nrm(ks[7], (N_ODD, DEC_BATCH, CONV_WIDTH - 1, CONV_DIM), 1.0),
        'page_table': page_table,
        'rel_bias': nrm(ks[9], (NUM_BUCKETS, N_HEADS), 0.5),
        'norm_mix': 1.0 + nrm(ks[10], (DEPTH, D_MODEL), 0.1),
        'norm_mlp': 1.0 + nrm(ks[11], (DEPTH, D_MODEL), 0.1),
        'norm_final': 1.0 + nrm(ks[12], (D_MODEL,), 0.1),
        'even_w_in': nrm(ks[13], (N_EVEN, D_MODEL, EVEN_IN), D_MODEL ** -0.5),
        'even_w_out': nrm(ks[14], (N_EVEN, 2 * MIX_WIDTH, D_MODEL), (2 * MIX_WIDTH) ** -0.5),
        'diff_lambda': nrm(ks[15], (N_EVEN, 4, DA_QK), 0.1),
        'diff_subln': 1.0 + nrm(ks[16], (N_EVEN, DA_V), 0.1),
        'pool_w': nrm(ks[17], (N_EVEN, len(POOL_WINDOWS), POOL_GROUP, POOL_GROUP), POOL_GROUP ** -0.5),
        'pool_scale': 1.0 + nrm(ks[18], (N_EVEN, POOL_DIM), 0.1),
        'odd_w_in': nrm(ks[19], (N_ODD, D_MODEL, ODD_IN), D_MODEL ** -0.5),
        'odd_w_out': nrm(ks[20], (N_ODD, 2 * MIX_WIDTH, D_MODEL), (2 * MIX_WIDTH) ** -0.5),
        'conv_w': nrm(ks[21], (N_ODD, CONV_WIDTH, CONV_DIM), CONV_WIDTH ** -0.5),
        'mlp_w1': nrm(ks[22], (DEPTH, D_MODEL, D_FF), D_MODEL ** -0.5),
        'mlp_w2': nrm(ks[23], (DEPTH, D_FF, D_MODEL), D_FF ** -0.5),
    }


def reference(x_prompt, x_sample, cache_diff_kv, cache_cmp_kv, cache_slc_kv, state_win_kv, state_pool, state_conv,
              page_table, rel_bias, norm_mix, norm_mlp, norm_final, even_w_in, even_w_out, diff_lambda, diff_subln,
              pool_w, pool_scale, odd_w_in, odd_w_out, conv_w, mlp_w1, mlp_w2):
    xp, xs = x_prompt, x_sample
    diff_p, diff_s, pool_p, pool_s = [], [], [], []
    cmp_p, cmp_s, slc_p, slc_s, win_p, win_s, conv_p, conv_s = [], [], [], [], [], [], [], []
    for layer in range(DEPTH):
        hp = _rmsnorm(xp, norm_mix[layer])
        hs = _rmsnorm(xs, norm_mix[layer])
        if layer % 2 == 0:
            e = layer // 2
            lam_init = 0.8 - 0.6 * math.exp(-0.3 * layer)
            yp, kv_rows_p, pl_p = _even_mixer_prompt(hp, even_w_in[e], even_w_out[e], diff_lambda[e], lam_init,
                                                     diff_subln[e], pool_w[e], pool_scale[e], rel_bias)
            ys, kv_rows_s, pl_s = _even_mixer_sample(hs, cache_diff_kv, e, page_table, state_pool[e], even_w_in[e],
                                                     even_w_out[e], diff_lambda[e], lam_init, diff_subln[e],
                                                     pool_w[e], pool_scale[e], rel_bias)
            diff_p.append(kv_rows_p); diff_s.append(kv_rows_s)
            pool_p.append(pl_p); pool_s.append(pl_s)
        else:
            o = layer // 2
            yp, c_p, s_p, w_p, cv_p = _odd_mixer_prompt(hp, odd_w_in[o], odd_w_out[o], conv_w[o], rel_bias)
            ys, c_s, s_s, w_s, cv_s = _odd_mixer_sample(hs, cache_cmp_kv, cache_slc_kv, o, page_table,
                                                        state_win_kv[o], state_conv[o], odd_w_in[o], odd_w_out[o],
                                                        conv_w[o], rel_bias)
            cmp_p.append(c_p); cmp_s.append(c_s)
            slc_p.append(s_p); slc_s.append(s_s)
            win_p.append(w_p); win_s.append(w_s)
            conv_p.append(cv_p); conv_s.append(cv_s)
        xp = xp + yp
        xs = xs + ys
        xp = xp + _sqrelu_mlp(_rmsnorm(xp, norm_mlp[layer]), mlp_w1[layer], mlp_w2[layer])
        xs = xs + _sqrelu_mlp(_rmsnorm(xs, norm_mlp[layer]), mlp_w1[layer], mlp_w2[layer])
    y_prompt = _rmsnorm(xp, norm_final)
    y_sample = _rmsnorm(xs, norm_final)
    diff_kv_prompt = jnp.stack(diff_p)
    diff_kv_sample = jnp.stack(diff_s)
    pool_prompt = jnp.stack(pool_p)
    pool_sample = jnp.stack(pool_s)
    cmp_kv_prompt = jnp.stack(cmp_p)
    cmp_kv_sample = jnp.stack(cmp_s)
    slc_kv_prompt = jnp.stack(slc_p)
    slc_kv_sample = jnp.stack(slc_s)
    win_kv_prompt = jnp.stack(win_p)
    win_kv_sample = jnp.stack(win_s)
    conv_prompt = jnp.stack(conv_p)
    conv_sample = jnp.stack(conv_s)
    return (y_prompt, y_sample, diff_kv_prompt, diff_kv_sample, pool_prompt, pool_sample, cmp_kv_prompt,
            cmp_kv_sample, slc_kv_prompt, slc_kv_sample, win_kv_prompt, win_kv_sample, conv_prompt, conv_sample)
```

```python
import functools
import math

import jax
import jax.numpy as jnp
import numpy as np
from jax import lax
from jax.experimental import pallas as pl
from jax.experimental.pallas import tpu as pltpu

F32 = jnp.float32
BF16 = jnp.bfloat16

D_MODEL = 2048
N_HEADS = 8
MIX_WIDTH = D_MODEL // 2
DA_QK = 64
DA_V = 128
POOL_WINDOWS = (2, 4, 8, 16)
POOL_GROUP = MIX_WIDTH // 4
POOL_STATE = 15
NSA_DH = 128
NSA_KV = 2
NSA_HPG = 4
CMP_BLOCK = 64
SLC_BLOCK = 64
SLC_TOPK = 16
WINDOW = 512
CONV_WIDTH = 3
D_FF = 4 * D_MODEL
NUM_BUCKETS = 32
MAX_DISTANCE = 128
PAGE_SIZE = 128
EPS = 1e-6
NEG = -1e30
ODD_SIZES = (1024, 512, 512, 512, 24, 1024, 1024, 1024)
ODD_IN_PAD = 5760
GATE_COL = 5632

LANES = 128
VMEM_LIMIT = 56 * 1024 * 1024
TQ = 256
TK = 256


def _cparams(sem):
    return pltpu.CompilerParams(dimension_semantics=sem, vmem_limit_bytes=VMEM_LIMIT)


def _bucket_np(dist):
    n = np.maximum(dist, 0)
    max_exact = NUM_BUCKETS // 2
    nf = np.maximum(n, 1).astype(np.float32)
    large = max_exact + (np.log(nf / np.float32(max_exact)) / np.float32(math.log(MAX_DISTANCE / max_exact))
                         * np.float32(NUM_BUCKETS - max_exact)).astype(np.int32)
    large = np.minimum(large, NUM_BUCKETS - 1)
    return np.where(n < max_exact, n, large).astype(np.int32)


def _bucket_or_masked(dist, valid):
    return np.where(valid, _bucket_np(dist), -1).astype(np.int32)


def _bias_kernel(table_ref, bucket_ref, o_ref):
    h = pl.program_id(0)
    bt = bucket_ref[...]
    far = table_ref[NUM_BUCKETS - 1, h]
    acc = jnp.zeros(bt.shape, F32)
    for b in range(NUM_BUCKETS - 1):
        acc = jnp.where(bt == b, table_ref[b, h] - far, acc)
    o_ref[...] = jnp.where(bt < 0, NEG, acc)


def _bias_tiles(table, buckets):
    shp = buckets.shape
    flat = jnp.asarray(buckets.reshape(-1, shp[-1]))
    rows = flat.shape[0]
    out = pl.pallas_call(
        _bias_kernel,
        grid=(N_HEADS,),
        in_specs=[pl.BlockSpec(memory_space=pltpu.SMEM),
                  pl.BlockSpec((rows, shp[-1]), lambda h: (0, 0))],
        out_specs=pl.BlockSpec((None, rows, shp[-1]), lambda h: (h, 0, 0)),
        out_shape=jax.ShapeDtypeStruct((N_HEADS, rows, shp[-1]), F32),
        compiler_params=_cparams(("arbitrary",)),
        name="bias_tiles",
    )(table, flat)
    return out.reshape((N_HEADS,) + shp)


def _rms_rows(x, g):
    y = x * lax.rsqrt(jnp.mean(x * x, axis=-1, keepdims=True) + EPS)
    return y * g


def _norm_matmul_kernel(x_ref, g_ref, w_ref, o_ref, h_scr):
    @pl.when(pl.program_id(1) == 0)
    def _():
        h_scr[...] = _rms_rows(x_ref[...], g_ref[...]).astype(BF16)

    o_ref[...] = jnp.dot(h_scr[...], w_ref[...], preferred_element_type=F32)


def _norm_matmul(x, g, w, tm, tn):
    m, d = x.shape
    n = w.shape[1]
    assert m % tm == 0 and n % tn == 0
    return pl.pallas_call(
        _norm_matmul_kernel,
        grid=(m // tm, n // tn),
        in_specs=[pl.BlockSpec((tm, d), lambda i, j: (i, 0)),
                  pl.BlockSpec((1, d), lambda i, j: (0, 0)),
                  pl.BlockSpec((d, tn), lambda i, j: (0, j))],
        out_specs=pl.BlockSpec((tm, tn), lambda i, j: (i, j)),
        out_shape=jax.ShapeDtypeStruct((m, n), F32),
        scratch_shapes=[pltpu.VMEM((tm, d), BF16)],
        compiler_params=_cparams(("parallel", "arbitrary")),
        name="norm_matmul",
    )(x, g.reshape(1, d), w)


def _out_proj_kernel(x_ref, a_ref, b_ref, wa_ref, wb_ref, o_ref):
    acc = jnp.dot(a_ref[...], wa_ref[...], preferred_element_type=F32)
    acc = acc + jnp.dot(b_ref[...], wb_ref[...], preferred_element_type=F32)
    o_ref[...] = x_ref[...] + acc


def _out_proj(x, a, b, w, tm, tn):
    m, d = x.shape
    ka = a.shape[1]
    assert m % tm == 0 and d % tn == 0
    return pl.pallas_call(
        _out_proj_kernel,
        grid=(m // tm, d // tn),
        in_specs=[pl.BlockSpec((tm, tn), lambda i, j: (i, j)),
                  pl.BlockSpec((tm, ka), lambda i, j: (i, 0)),
                  pl.BlockSpec((tm, ka), lambda i, j: (i, 0)),
                  pl.BlockSpec((ka, tn), lambda i, j: (0, j)),
                  pl.BlockSpec((ka, tn), lambda i, j: (1, j))],
        out_specs=pl.BlockSpec((tm, tn), lambda i, j: (i, j)),
        out_shape=jax.ShapeDtypeStruct((m, d), F32),
        compiler_params=_cparams(("parallel", "arbitrary")),
        name="out_proj",
    )(x, a, b, w, w)


def _mlp_kernel(x_ref, g_ref, w1_ref, w2_ref, gf_ref, o_ref, h_scr, *, final_norm):
    f = pl.program_id(1)

    @pl.when(f == 0)
    def _():
        x = x_ref[...]
        h_scr[...] = _rms_rows(x, g_ref[...]).astype(BF16)
        o_ref[...] = x

    a = jnp.maximum(jnp.dot(h_scr[...], w1_ref[...], preferred_element_type=F32), 0.0)
    a = (a * a).astype(BF16)
    o_ref[...] += jnp.dot(a, w2_ref[...], preferred_element_type=F32)

    if final_norm:
        @pl.when(f == pl.num_programs(1) - 1)
        def _():
            o_ref[...] = _rms_rows(o_ref[...], gf_ref[...])


def _mlp(x, g, w1, w2, gf, tm, tf, final_norm):
    m, d = x.shape
    ff = w1.shape[1]
    assert m % tm == 0 and ff % tf == 0
    return pl.pallas_call(
        functools.partial(_mlp_kernel, final_norm=final_norm),
        grid=(m // tm, ff // tf),
        in_specs=[pl.BlockSpec((tm, d), lambda i, f: (i, 0)),
                  pl.BlockSpec((1, d), lambda i, f: (0, 0)),
                  pl.BlockSpec((d, tf), lambda i, f: (0, f)),
                  pl.BlockSpec((tf, d), lambda i, f: (f, 0)),
                  pl.BlockSpec((1, d), lambda i, f: (0, 0))],
        out_specs=pl.BlockSpec((tm, d), lambda i, f: (i, 0)),
        out_shape=jax.ShapeDtypeStruct((m, d), F32),
        scratch_shapes=[pltpu.VMEM((tm, d), BF16)],
        compiler_params=_cparams(("parallel", "arbitrary")),
        name="mlp",
    )(x, g.reshape(1, d), w1, w2, gf.reshape(1, d))


def _flash_init(m_scr, l_scr, acc_scr):
    m_scr[...] = jnp.full(m_scr.shape, NEG, F32)
    l_scr[...] = jnp.zeros(l_scr.shape, F32)
    acc_scr[...] = jnp.zeros(acc_scr.shape, F32)


def _flash_tile(qs, kt, vt, m_scr, l_scr, acc_scr, *, heads, scale=None, bias=None, mask=None):
    s = lax.dot_general(qs, kt, (((1,), (1,)), ((), ())), preferred_element_type=F32)
    rows, tk = s.shape
    if scale is not None:
        s = s * scale
    if bias is not None or mask is not None:
        s3 = s.reshape(heads, rows // heads, tk)
        if bias is not None:
            s3 = s3 + bias
        if mask is not None:
            s3 = jnp.where(mask[None], s3, NEG)
        s = s3.reshape(rows, tk)
    m_prev = m_scr[...]
    m_next = jnp.maximum(m_prev, jnp.max(s, axis=1, keepdims=True))
    p = jnp.exp(s - pltpu.repeat(m_next, tk // LANES, axis=1))
    alpha = jnp.exp(m_prev - m_next)
    l_scr[...] = alpha * l_scr[...] + jnp.sum(p, axis=1, keepdims=True)
    acc_scr[...] = alpha * acc_scr[...] + jnp.dot(p.astype(BF16), vt, preferred_element_type=F32)
    m_scr[...] = m_next


def _kv_tile(k_ref, v_ref, j, tk):
    off = pl.multiple_of(j * tk, tk)
    return k_ref[pl.ds(off, tk), :].astype(BF16), v_ref[pl.ds(off, tk), :].astype(BF16)


def _diff_lambda_in_kernel(lv_ref, lam_init):
    lv = lv_ref[...]
    a = jnp.sum(lv[0:1] * lv[1:2], axis=1, keepdims=True)
    b = jnp.sum(lv[2:3] * lv[3:4], axis=1, keepdims=True)
    return jnp.exp(a) - jnp.exp(b) + lam_init


def _diff_attn_kernel(lv_ref, q_ref, k_ref, v_ref, bias_ref, g_ref, o_ref, m_scr, l_scr, acc_scr, *, lam_init):
    qi = pl.program_id(2)
    tq = q_ref.shape[0]
    tk = bias_ref.shape[-1]
    q = q_ref[...] * (DA_QK ** -0.5)
    lane = lax.broadcasted_iota(jnp.int32, q.shape, 1)
    qs = jnp.concatenate([jnp.where(lane < DA_QK, q, 0.0), jnp.where(lane >= DA_QK, q, 0.0)], axis=0).astype(BF16)
    _flash_init(m_scr, l_scr, acc_scr)
    upd = functools.partial(_flash_tile, m_scr=m_scr, l_scr=l_scr, acc_scr=acc_scr, heads=2)

    def far(j, c):
        kt, vt = _kv_tile(k_ref, v_ref, j, tk)
        upd(qs, kt, vt)
        return c

    lax.fori_loop(0, jnp.maximum(qi - 1, 0), far, 0)

    @pl.when(qi >= 1)
    def _():
        kt, vt = _kv_tile(k_ref, v_ref, qi - 1, tk)
        upd(qs, kt, vt, bias=bias_ref[1][None])

    kt, vt = _kv_tile(k_ref, v_ref, qi, tk)
    upd(qs, kt, vt, bias=bias_ref[0][None])

    o = acc_scr[...] / l_scr[...]
    lam = _diff_lambda_in_kernel(lv_ref, lam_init)
    a = o[:tq] - lam * o[tq:]
    o_ref[...] = (_rms_rows(a, g_ref[...]) * (1.0 - lam_init)).astype(BF16)


def _diff_attn_prompt(z, lam_vec, subln_g, bias, lam_init, batch, seq):
    nq = seq // TQ
    return pl.pallas_call(
        functools.partial(_diff_attn_kernel, lam_init=lam_init),
        grid=(batch, N_HEADS, nq),
        in_specs=[pl.BlockSpec((4, DA_QK), lambda b, h, i: (0, 0)),
                  pl.BlockSpec((TQ, DA_V), lambda b, h, i: (b * nq + i, h)),
                  pl.BlockSpec((seq, DA_V), lambda b, h, i: (b, N_HEADS + h)),
                  pl.BlockSpec((seq, DA_V), lambda b, h, i: (b, 2 * N_HEADS + h)),
                  pl.BlockSpec((None, bias.shape[1], TQ, TK), lambda b, h, i: (h, 0, 0, 0)),
                  pl.BlockSpec((1, DA_V), lambda b, h, i: (0, 0))],
        out_specs=pl.BlockSpec((TQ, DA_V), lambda b, h, i: (b * nq + i, h)),
        out_shape=jax.ShapeDtypeStruct((batch * seq, N_HEADS * DA_V), BF16),
        scratch_shapes=[pltpu.VMEM((2 * TQ, LANES), F32)] * 3,
        compiler_params=_cparams(("parallel", "parallel", "arbitrary")),
        name="diff_attn_prompt",
    )(lam_vec, z, z, z, bias, subln_g.reshape(1, DA_V))


def _toeplitz_buckets(tq, tk):
    i = np.arange(tq)[:, None]
    j = np.arange(tk)[None, :]
    diag = _bucket_or_masked(i - j, i >= j)
    sub = _bucket_or_masked(tk + i - j, np.ones((tq, tk), bool))
    return np.stack([diag, sub])


HALO = 16


def _pool_kernel(u_ref, halo_ref, prev_ref, w_ref, scale_ref, o_ref, ext_scr, *, pos0):
    i = pl.program_id(1)
    tm = u_ref.shape[0]
    ext_scr[0:HALO, :] = jnp.where(i == 0, prev_ref[...], halo_ref[...])
    ext_scr[HALO:HALO + tm, :] = u_ref[...]
    pos = pos0 + i * tm + lax.broadcasted_iota(jnp.int32, (tm, 1), 0)
    outs = []
    for g, w in enumerate(POOL_WINDOWS):
        c0, c1 = g * POOL_GROUP, (g + 1) * POOL_GROUP
        x0 = ext_scr[HALO:HALO + tm, c0:c1]
        win = x0
        for k in range(1, w):
            win = win + ext_scr[HALO - k:HALO - k + tm, c0:c1]
        cnt = jnp.minimum(pos + 1, w).astype(F32)
        d = win / cnt - x0
        outs.append(jnp.dot(d.astype(BF16), w_ref[g], preferred_element_type=F32))
    o_ref[...] = (jnp.concatenate(outs, axis=-1) * scale_ref[...]).astype(BF16)


def _pool_mix(z, prev, w_pool, scale, pos0, batch, seq, tm):
    nb = seq // tm
    ucol = 3
    return pl.pallas_call(
        functools.partial(_pool_kernel, pos0=pos0),
        grid=(batch, nb),
        in_specs=[pl.BlockSpec((tm, MIX_WIDTH), lambda b, i: (b * nb + i, ucol)),
                  pl.BlockSpec((HALO, MIX_WIDTH),
                               lambda b, i: (jnp.maximum((b * nb + i) * (tm // HALO) - 1, 0), ucol)),
                  pl.BlockSpec((None, HALO, MIX_WIDTH), lambda b, i: (b, 0, 0)),
                  pl.BlockSpec((4, POOL_GROUP, POOL_GROUP), lambda b, i: (0, 0, 0)),
                  pl.BlockSpec((1, MIX_WIDTH), lambda b, i: (0, 0))],
        out_specs=pl.BlockSpec((tm, MIX_WIDTH), lambda b, i: (b * nb + i, 0)),
        out_shape=jax.ShapeDtypeStruct((batch * seq, MIX_WIDTH), BF16),
        scratch_shapes=[pltpu.VMEM((HALO + tm, MIX_WIDTH), F32)],
        compiler_params=_cparams(("parallel", "arbitrary")),
        name="pool_mix",
    )(z, z, prev, w_pool, scale.reshape(1, MIX_WIDTH))


COL_CB, COL_CC, COL_CH, COL_KC, COL_KS, COL_KW, COL_GATE = 8, 16, 24, 32, 36, 40, 44


def _odd_weight_layout(w_in):
    offs = np.cumsum((0,) + ODD_SIZES)
    q, kc, ks_, kw, gates, cb, cc, ch = [w_in[:, offs[i]:offs[i + 1]] for i in range(8)]
    pad = jnp.zeros((w_in.shape[0], LANES - ODD_SIZES[4]), w_in.dtype)
    return jnp.concatenate([q, cb, cc, ch, kc, ks_, kw, gates, pad], axis=1)


def _select_blocks(score_t, q0, nb):
    nbp, tq = score_t.shape
    blk = lax.broadcasted_iota(jnp.int32, (nbp, tq), 0)
    cur = (q0 + lax.broadcasted_iota(jnp.int32, (nbp, tq), 1)) // SLC_BLOCK
    forced = (blk == 0) | (blk == cur) | (blk == cur - 1)
    future = (blk > cur) | (blk >= nb)
    s = jnp.where(forced, jnp.inf, jnp.where(future, -jnp.inf, score_t))
    cnt = jnp.zeros((nbp, tq), jnp.int32)
    for n in range(nb):
        row = s[n:n + 1, :]
        beats = (row > s) | ((row == s) & (blk > n))
        cnt = cnt + beats.astype(jnp.int32)
    return (cnt < min(SLC_TOPK, nb)).astype(F32)


def _nsa_prompt_kernel(q_ref, kc_ref, vc_ref, ks_ref, vs_ref, kw_ref, vw_ref, gate_ref, tb_ref, cb_ref, e_ref,
                       o_ref, kcm_scr, vcm_scr, m_scr, l_scr, acc_scr):
    g = pl.program_id(1)
    qi = pl.program_id(2)
    tq = q_ref.shape[0]
    tk = tb_ref.shape[-1]
    seq = kc_ref.shape[0]
    nb = seq // CMP_BLOCK
    nbp = -(-nb // 8) * 8
    scale = NSA_DH ** -0.5
    hp = NSA_HPG

    @pl.when(qi == 0)
    def _():
        kcm_scr[...] = jnp.zeros(kcm_scr.shape, BF16)
        vcm_scr[...] = jnp.zeros(vcm_scr.shape, BF16)
        kcm_scr[0:nb, :] = (jnp.sum(kc_ref[...].reshape(nb, CMP_BLOCK, NSA_DH), axis=1) / CMP_BLOCK).astype(BF16)
        vcm_scr[0:nb, :] = (jnp.sum(vc_ref[...].reshape(nb, CMP_BLOCK, NSA_DH), axis=1) / CMP_BLOCK).astype(BF16)

    qh = [q_ref[:, h * NSA_DH:(h + 1) * NSA_DH].astype(BF16) for h in range(hp)]
    qs = jnp.concatenate(qh, axis=0)

    kcm = kcm_scr[...]
    vcm = vcm_scr[...]
    o_cmp = []
    p_grp_t = jnp.zeros((LANES, tq), F32)
    for h in range(hp):
        lt = lax.dot_general(kcm, qh[h], (((1,), (1,)), ((), ())), preferred_element_type=F32) * scale
        bt = cb_ref[h]
        lt = lt + bt
        p = jnp.exp(lt - jnp.max(lt, axis=0, keepdims=True))
        p = p / jnp.sum(p, axis=0, keepdims=True)
        p = jnp.where(bt > 0.5 * NEG, p, 0.0)
        p_grp_t = p_grp_t + p
        o_cmp.append(jnp.dot(p.T.astype(BF16), vcm, preferred_element_type=F32))

    sel_t = _select_blocks(p_grp_t[0:nbp, :], qi * tq, nb)
    if nbp < LANES:
        sel_t = jnp.concatenate([sel_t, jnp.zeros((LANES - nbp, tq), F32)], axis=0)
    sel = sel_t.T.astype(BF16)

    upd = functools.partial(_flash_tile, m_scr=m_scr, l_scr=l_scr, acc_scr=acc_scr, heads=hp, scale=scale)

    def sel_mask(j):
        return jnp.dot(sel, e_ref[j], preferred_element_type=F32) > 0.5

    _flash_init(m_scr, l_scr, acc_scr)

    def far(j, c):
        kt, vt = _kv_tile(ks_ref, vs_ref, j, tk)
        upd(qs, kt, vt, mask=sel_mask(j))
        return c

    lax.fori_loop(0, jnp.maximum(qi - 1, 0), far, 0)

    @pl.when(qi >= 1)
    def _():
        kt, vt = _kv_tile(ks_ref, vs_ref, qi - 1, tk)
        upd(qs, kt, vt, bias=tb_ref[:, 1], mask=sel_mask(qi - 1))

    kt, vt = _kv_tile(ks_ref, vs_ref, qi, tk)
    upd(qs, kt, vt, bias=tb_ref[:, 0], mask=sel_mask(qi))
    o_slc = acc_scr[...] / l_scr[...]

    _flash_init(m_scr, l_scr, acc_scr)

    @pl.when(qi >= 2)
    def _():
        kt, vt = _kv_tile(kw_ref, vw_ref, qi - 2, tk)
        upd(qs, kt, vt, bias=tb_ref[:, 2])

    @pl.when(qi >= 1)
    def _():
        kt, vt = _kv_tile(kw_ref, vw_ref, qi - 1, tk)
        upd(qs, kt, vt, bias=tb_ref[:, 1])

    kt, vt = _kv_tile(kw_ref, vw_ref, qi, tk)
    upd(qs, kt, vt, bias=tb_ref[:, 0])
    o_win = acc_scr[...] / l_scr[...]

    sig = jax.nn.sigmoid(gate_ref[...])
    lane = lax.broadcasted_iota(jnp.int32, sig.shape, 1)

    def gate(branch, h):
        col = branch * N_HEADS + g * hp + h
        return jnp.sum(jnp.where(lane == col, sig, 0.0), axis=1, keepdims=True)

    for h in range(hp):
        rows = slice(h * tq, (h + 1) * tq)
        o = gate(0, h) * o_cmp[h] + gate(1, h) * o_slc[rows] + gate(2, h) * o_win[rows]
        o_ref[:, h * NSA_DH:(h + 1) * NSA_DH] = o.astype(BF16)


def _nsa_buckets(seq, tq, tk):
    i = np.arange(tq)[:, None]
    j = np.arange(tk)[None, :]
    toe = _toeplitz_buckets(tq, tk)
    assert WINDOW == 2 * tk and tq == tk
    win2 = np.where(i < j, NUM_BUCKETS - 1, -1).astype(np.int32)
    tiles = np.concatenate([toe, win2[None]])
    nb = seq // CMP_BLOCK
    blk_end = (np.arange(LANES)[:, None] + 1) * CMP_BLOCK - 1
    dist = np.arange(seq)[None, :] - blk_end
    cmp_t = _bucket_or_masked(dist, (dist >= 0) & (np.arange(LANES)[:, None] < nb))
    nk = seq // tk
    key_blk = (np.arange(nk)[:, None, None] * tk + np.arange(tk)[None, None, :]) // SLC_BLOCK
    expand = (key_blk == np.arange(LANES)[None, :, None]).astype(np.float32)
    return tiles, cmp_t, expand


def _nsa_prompt(z, tile_bias, cmp_bias_t, expand, batch, seq):
    nq = seq // TQ
    nk = seq // TK
    kv = lambda col: pl.BlockSpec((seq, NSA_DH), lambda b, g, i: (b, col + g))
    return pl.pallas_call(
        _nsa_prompt_kernel,
        grid=(batch, NSA_KV, nq),
        in_specs=[pl.BlockSpec((TQ, NSA_HPG * NSA_DH), lambda b, g, i: (b * nq + i, g)),
                  kv(COL_KC), kv(COL_KC + 2), kv(COL_KS), kv(COL_KS + 2), kv(COL_KW), kv(COL_KW + 2),
                  pl.BlockSpec((TQ, LANES), lambda b, g, i: (b * nq + i, COL_GATE)),
                  pl.BlockSpec((NSA_HPG, 3, TQ, TK), lambda b, g, i: (g, 0, 0, 0)),
                  pl.BlockSpec((NSA_HPG, LANES, TQ), lambda b, g, i: (g, 0, i)),
                  pl.BlockSpec((nk, LANES, TK), lambda b, g, i: (0, 0, 0))],
        out_specs=pl.BlockSpec((TQ, NSA_HPG * NSA_DH), lambda b, g, i: (b * nq + i, g)),
        out_shape=jax.ShapeDtypeStruct((batch * seq, MIX_WIDTH), BF16),
        scratch_shapes=[pltpu.VMEM((LANES, NSA_DH), BF16), pltpu.VMEM((LANES, NSA_DH), BF16)]
        + [pltpu.VMEM((NSA_HPG * TQ, LANES), F32)] * 3,
        compiler_params=_cparams(("parallel", "parallel", "arbitrary")),
        name="nsa_prompt",
    )(z, z, z, z, z, z, z, z, tile_bias, cmp_bias_t, expand)


def _conv_kernel(cb_ref, cc_ref, ch_ref, hc_ref, hh_ref, prev_ref, w_ref, o_ref, tail_ref, ext_scr):
    i = pl.program_id(1)
    tm = cb_ref.shape[0]
    e = cc_ref[...] * ch_ref[...]
    ext_scr[0:8, :] = jnp.where(i == 0, prev_ref[...], hc_ref[...] * hh_ref[...])
    ext_scr[8:8 + tm, :] = e
    w = w_ref[...]
    y = w[0:1] * ext_scr[6:6 + tm, :]
    y = y + w[1:2] * ext_scr[7:7 + tm, :]
    y = y + w[2:3] * e
    o_ref[...] = (cb_ref[...] * y).astype(BF16)

    @pl.when(i == pl.num_programs(1) - 1)
    def _():
        tail_ref[...] = e[tm - 8:tm, :]


def _short_conv(z, prev, w_conv, batch, seq, tm):
    nb = seq // tm
    cw = MIX_WIDTH // LANES
    blk = lambda c: pl.BlockSpec((tm, MIX_WIDTH), lambda b, i: (b * nb + i, c // cw))
    halo = lambda c: pl.BlockSpec((8, MIX_WIDTH), lambda b, i: (jnp.maximum((b * nb + i) * (tm // 8) - 1, 0), c // cw))
    return pl.pallas_call(
        _conv_kernel,
        grid=(batch, nb),
        in_specs=[blk(COL_CB), blk(COL_CC), blk(COL_CH), halo(COL_CC), halo(COL_CH),
                  pl.BlockSpec((None, 8, MIX_WIDTH), lambda b, i: (b, 0, 0)),
                  pl.BlockSpec((8, MIX_WIDTH), lambda b, i: (0, 0))],
        out_specs=[pl.BlockSpec((tm, MIX_WIDTH), lambda b, i: (b * nb + i, 0)),
                   pl.BlockSpec((None, 8, MIX_WIDTH), lambda b, i: (b, 0, 0))],
        out_shape=[jax.ShapeDtypeStruct((batch * seq, MIX_WIDTH), BF16),
                   jax.ShapeDtypeStruct((batch, 8, MIX_WIDTH), F32)],
        scratch_shapes=[pltpu.VMEM((8 + tm, MIX_WIDTH), F32)],
        compiler_params=_cparams(("parallel", "arbitrary")),
        name="short_conv",
    )(z, z, z, z, z, prev, jnp.pad(w_conv, ((0, 8 - CONV_WIDTH), (0, 0))))


T8 = 8
DIFF_PAGES_PER_STEP = 8
NSA_PAGES_PER_STEP = 16


def _page_specs(n, width, layer, pages_per_step):
    def spec(k):
        return pl.BlockSpec((None, None, PAGE_SIZE, width),
                            lambda b, s, pt: (layer, pt[b, s * pages_per_step + k], 0, 0))
    return [spec(k) for k in range(n)]


def _diff_decode_kernel(pt_ref, q_ref, new_ref, bias_ref, lv_ref, g_ref, *rest, lam_init, past_pages):
    pages = rest[:DIFF_PAGES_PER_STEP]
    o_ref, m_scr, l_scr, acc_scr = rest[DIFF_PAGES_PER_STEP:]
    s = pl.program_id(1)
    last = pl.num_programs(1) - 1
    kw = N_HEADS * DA_V
    rows = q_ref.shape[0]

    @pl.when(s == 0)
    def _():
        _flash_init(m_scr, l_scr, acc_scr)

    q = (q_ref[...].astype(F32) * (DA_QK ** -0.5)).astype(BF16)

    def update(s_tile, v_tiles):
        m_prev = m_scr[...]
        m_next = jnp.maximum(m_prev, jnp.max(s_tile, axis=1, keepdims=True))
        p = jnp.exp(s_tile - pltpu.repeat(m_next, s_tile.shape[1] // LANES, axis=1))
        alpha = jnp.exp(m_prev - m_next)
        l_scr[...] = alpha * l_scr[...] + jnp.sum(p, axis=1, keepdims=True)
        pv = jnp.zeros((rows, kw), F32)
        for k, vt in enumerate(v_tiles):
            pv = pv + jnp.dot(p[:, k * PAGE_SIZE:(k + 1) * PAGE_SIZE].astype(BF16), vt, preferred_element_type=F32)
        acc_scr[...] = pltpu.repeat(alpha, kw // LANES, axis=1) * acc_scr[...] + pv
        m_scr[...] = m_next

    is_last = (s == last).astype(F32)
    tiles = []
    for k, pg in enumerate(pages):
        st = lax.dot_general(q, pg[:, 0:kw].astype(BF16), (((1,), (1,)), ((), ())), preferred_element_type=F32)
        if k == DIFF_PAGES_PER_STEP - 1:
            st = st + is_last * bias_ref[0]
        tiles.append(st)
    update(jnp.concatenate(tiles, axis=1), [pg[:, kw:2 * kw].astype(BF16) for pg in pages])

    @pl.when(s == last)
    def _():
        st = lax.dot_general(q, new_ref[:, 0:kw].astype(BF16), (((1,), (1,)), ((), ())), preferred_element_type=F32)
        update(st + bias_ref[1], [new_ref[:, kw:2 * kw].astype(BF16)])
        o_full = acc_scr[...] / pltpu.repeat(l_scr[...], kw // LANES, axis=1)
        head = (lax.broadcasted_iota(jnp.int32, (rows, DA_V), 0) // T8) % N_HEADS
        o = jnp.zeros((rows, DA_V), F32)
        for h in range(N_HEADS):
            o = o + jnp.where(head == h, o_full[:, h * DA_V:(h + 1) * DA_V], 0.0)
        half = rows // 2
        a = o[:half] - _diff_lambda_in_kernel(lv_ref, lam_init) * o[half:]
        o_ref[...] = _rms_rows(a, g_ref[...]) * (1.0 - lam_init)


def _diff_decode(page_table, qbd, kv_new, bias, lam_vec, subln_g, cache, layer, lam_init):
    batch, n_pages = page_table.shape
    steps = n_pages // DIFF_PAGES_PER_STEP
    width = 2 * N_HEADS * DA_V
    rows = qbd.shape[1]
    grid_spec = pltpu.PrefetchScalarGridSpec(
        num_scalar_prefetch=1,
        grid=(batch, steps),
        in_specs=[pl.BlockSpec((None, rows, N_HEADS * DA_V), lambda b, s, pt: (b, 0, 0)),
                  pl.BlockSpec((None, PAGE_SIZE, width), lambda b, s, pt: (b, 0, 0)),
                  pl.BlockSpec((2, rows, PAGE_SIZE), lambda b, s, pt: (0, 0, 0)),
                  pl.BlockSpec((4, DA_QK), lambda b, s, pt: (0, 0)),
                  pl.BlockSpec((1, DA_V), lambda b, s, pt: (0, 0))]
        + _page_specs(DIFF_PAGES_PER_STEP, width, layer, DIFF_PAGES_PER_STEP),
        out_specs=pl.BlockSpec((None, rows // 2, DA_V), lambda b, s, pt: (b, 0, 0)),
        scratch_shapes=[pltpu.VMEM((rows, LANES), F32), pltpu.VMEM((rows, LANES), F32),
                        pltpu.VMEM((rows, N_HEADS * DA_V), F32)],
    )
    return pl.pallas_call(
        functools.partial(_diff_decode_kernel, lam_init=lam_init, past_pages=n_pages),
        grid_spec=grid_spec,
        out_shape=jax.ShapeDtypeStruct((batch, rows // 2, DA_V), F32),
        compiler_params=_cparams(("parallel", "arbitrary")),
        name="diff_decode",
    )(page_table, qbd, kv_new, bias, lam_vec, subln_g.reshape(1, DA_V), *([cache] * DIFF_PAGES_PER_STEP))


def _cmp_means_kernel(pt_ref, *rest):
    pages = rest[:NSA_PAGES_PER_STEP]
    o_ref = rest[NSA_PAGES_PER_STEP]
    rows = []
    for pg in pages:
        x = pg[...]
        for half in range(PAGE_SIZE // CMP_BLOCK):
            rows.append(jnp.sum(x[half * CMP_BLOCK:(half + 1) * CMP_BLOCK], axis=0, keepdims=True) / CMP_BLOCK)
    o_ref[...] = jnp.concatenate(rows, axis=0)


def _cmp_means(page_table, cache, layer):
    batch, n_pages = page_table.shape
    steps = n_pages // NSA_PAGES_PER_STEP
    width = 2 * NSA_KV * NSA_DH
    per_step = NSA_PAGES_PER_STEP * (PAGE_SIZE // CMP_BLOCK)
    grid_spec = pltpu.PrefetchScalarGridSpec(
        num_scalar_prefetch=1,
        grid=(batch, steps),
        in_specs=_page_specs(NSA_PAGES_PER_STEP, width, layer, NSA_PAGES_PER_STEP),
        out_specs=pl.BlockSpec((None, per_step, width), lambda b, s, pt: (b, s, 0)),
    )
    return pl.pallas_call(
        _cmp_means_kernel,
        grid_spec=grid_spec,
        out_shape=jax.ShapeDtypeStruct((batch, steps * per_step, width), F32),
        compiler_params=_cparams(("parallel", "arbitrary")),
        name="cmp_means",
    )(page_table, *([cache] * NSA_PAGES_PER_STEP))


NB_DEC = 384


def _cmp_select_decode_kernel(q_ref, kvc_ref, new_ref, bias_ref, ocmp_ref, sel_ref, k_scr, v_scr, *, past_len, n_new):
    nbp = kvc_ref.shape[0]
    real_row = lax.broadcasted_iota(jnp.int32, (T8, NSA_DH), 0) < n_new
    scale = NSA_DH ** -0.5
    lane = lax.broadcasted_iota(jnp.int32, (T8, NB_DEC), 1)
    cur = (past_len + lax.broadcasted_iota(jnp.int32, (T8, NB_DEC), 0)) // SLC_BLOCK
    forced = (lane == 0) | (lane == cur) | (lane == cur - 1)
    future = lane > cur
    first_row = lax.broadcasted_iota(jnp.int32, (T8, NSA_DH), 0) == 0
    for g in range(NSA_KV):
        kc0, vc0 = g * NSA_DH, (NSA_KV + g) * NSA_DH
        for scr, c0 in ((k_scr, kc0), (v_scr, vc0)):
            scr[...] = jnp.zeros(scr.shape, BF16)
            scr[0:nbp, :] = kvc_ref[:, c0:c0 + NSA_DH].astype(BF16)
            new_rows = jnp.where(real_row, new_ref[:, c0:c0 + NSA_DH], 0.0)
            new_mean = jnp.sum(new_rows, axis=0, keepdims=True) / CMP_BLOCK
            scr[nbp:nbp + T8, :] = jnp.where(first_row, new_mean, 0.0).astype(BF16)
        q = q_ref[g].astype(BF16)
        bt = bias_ref[g]
        lg = lax.dot_general(q, k_scr[...], (((1,), (1,)), ((), ())), preferred_element_type=F32) * scale + bt
        p = jnp.exp(lg - jnp.max(lg, axis=1, keepdims=True))
        p = p / jnp.sum(p, axis=1, keepdims=True)
        p = jnp.where(bt > 0.5 * NEG, p, 0.0)
        ocmp_ref[g] = jnp.dot(p.astype(BF16), v_scr[...], preferred_element_type=F32)
        score = p[0:T8] + p[T8:2 * T8] + p[2 * T8:3 * T8] + p[3 * T8:4 * T8]
        sc = jnp.where(forced, jnp.inf, jnp.where(future, -jnp.inf, score))
        taken = jnp.zeros((T8, NB_DEC), jnp.bool_)
        for _ in range(SLC_TOPK):
            sm = jnp.where(taken, -jnp.inf, sc)
            cand = (sm == jnp.max(sm, axis=1, keepdims=True)) & jnp.logical_not(taken)
            idx = jnp.min(jnp.where(cand, lane, NB_DEC), axis=1, keepdims=True)
            taken = taken | (lane == idx)
        sel_ref[g] = taken.astype(F32)


def _cmp_select_decode(qg, kvc, kv_new, bias, past_len, n_new):
    batch = qg.shape[0]
    rows = NSA_HPG * T8
    width = 2 * NSA_KV * NSA_DH
    nbp = kvc.shape[1]
    return pl.pallas_call(
        functools.partial(_cmp_select_decode_kernel, past_len=past_len, n_new=n_new),
        grid=(batch,),
        in_specs=[pl.BlockSpec((None, NSA_KV, rows, NSA_DH), lambda b: (b, 0, 0, 0)),
                  pl.BlockSpec((None, nbp, width), lambda b: (b, 0, 0)),
                  pl.BlockSpec((None, T8, width), lambda b: (b, 0, 0)),
                  pl.BlockSpec((NSA_KV, rows, NB_DEC), lambda b: (0, 0, 0))],
        out_specs=[pl.BlockSpec((None, NSA_KV, rows, NSA_DH), lambda b: (b, 0, 0, 0)),
                   pl.BlockSpec((None, NSA_KV, T8, NB_DEC), lambda b: (b, 0, 0, 0))],
        out_shape=[jax.ShapeDtypeStruct((batch, NSA_KV, rows, NSA_DH), F32),
                   jax.ShapeDtypeStruct((batch, NSA_KV, T8, NB_DEC), F32)],
        scratch_shapes=[pltpu.VMEM((NB_DEC, NSA_DH), BF16), pltpu.VMEM((NB_DEC, NSA_DH), BF16)],
        compiler_params=_cparams(("parallel",)),
        name="cmp_select_decode",
    )(qg, kvc, kv_new, bias)


def _slc_decode_kernel(pt_ref, q_ref, sel_ref, selnew_ref, e_ref, new_ref, bias_ref, *rest):
    pages = rest[:NSA_PAGES_PER_STEP]
    o_ref, m_scr, l_scr, acc_scr = rest[NSA_PAGES_PER_STEP:]
    s = pl.program_id(1)
    last = pl.num_programs(1) - 1
    scale = NSA_DH ** -0.5
    rows = NSA_HPG * T8
    keys = NSA_PAGES_PER_STEP * PAGE_SIZE

    @pl.when(s == 0)
    def _():
        _flash_init(m_scr, l_scr, acc_scr)

    is_last = (s == last).astype(F32)
    for g in range(NSA_KV):
        kc0, vc0 = g * NSA_DH, (NSA_KV + g) * NSA_DH
        q = q_ref[g].astype(BF16)
        m_g, l_g, acc_g = m_scr.at[g], l_scr.at[g], acc_scr.at[g]

        def update(st, mask, v_tiles):
            st = jnp.where(mask, st.reshape(NSA_HPG, T8, st.shape[-1]), NEG).reshape(rows, st.shape[-1])
            m_prev = m_g[...]
            m_next = jnp.maximum(m_prev, jnp.max(st, axis=1, keepdims=True))
            p = jnp.exp(st - pltpu.repeat(m_next, st.shape[1] // LANES, axis=1))
            alpha = jnp.exp(m_prev - m_next)
            l_g[...] = alpha * l_g[...] + jnp.sum(p, axis=1, keepdims=True)
            pv = jnp.zeros((rows, NSA_DH), F32)
            for k, vt in enumerate(v_tiles):
                pv = pv + jnp.dot(p[:, k * PAGE_SIZE:(k + 1) * PAGE_SIZE].astype(BF16), vt,
                                  preferred_element_type=F32)
            acc_g[...] = alpha * acc_g[...] + pv
            m_g[...] = m_next

        tiles = []
        for k, pg in enumerate(pages):
            st = lax.dot_general(q, pg[:, kc0:kc0 + NSA_DH].astype(BF16), (((1,), (1,)), ((), ())),
                                 preferred_element_type=F32) * scale
            if k == NSA_PAGES_PER_STEP - 1:
                st = st + is_last * bias_ref[0, g]
            tiles.append(st)
        mask = jnp.dot(sel_ref[g].astype(BF16), e_ref[...], preferred_element_type=F32) > 0.5
        update(jnp.concatenate(tiles, axis=1), mask[None], [pg[:, vc0:vc0 + NSA_DH].astype(BF16) for pg in pages])

        @pl.when(s == last)
        def _():
            st = lax.dot_general(q, new_ref[:, kc0:kc0 + NSA_DH].astype(BF16), (((1,), (1,)), ((), ())),
                                 preferred_element_type=F32) * scale + bias_ref[1, g]
            update(st, (selnew_ref[g] > 0.5)[None], [new_ref[:, vc0:vc0 + NSA_DH].astype(BF16)])
            o_ref[g] = acc_g[...] / l_g[...]


def _slc_decode(page_table, qg, sel_steps, sel_new, expand, kv_new, bias, cache, layer):
    batch, n_pages = page_table.shape
    steps = n_pages // NSA_PAGES_PER_STEP
    rows = NSA_HPG * T8
    width = 2 * NSA_KV * NSA_DH
    keys = NSA_PAGES_PER_STEP * PAGE_SIZE
    grid_spec = pltpu.PrefetchScalarGridSpec(
        num_scalar_prefetch=1,
        grid=(batch, steps),
        in_specs=[pl.BlockSpec((None, NSA_KV, rows, NSA_DH), lambda b, s, pt: (b, 0, 0, 0)),
                  pl.BlockSpec((None, NSA_KV, None, T8, LANES), lambda b, s, pt: (b, 0, s, 0, 0)),
                  pl.BlockSpec((None, NSA_KV, T8, LANES), lambda b, s, pt: (b, 0, 0, 0)),
                  pl.BlockSpec((LANES, keys), lambda b, s, pt: (0, 0)),
                  pl.BlockSpec((None, PAGE_SIZE, width), lambda b, s, pt: (b, 0, 0)),
                  pl.BlockSpec((2, NSA_KV, rows, PAGE_SIZE), lambda b, s, pt: (0, 0, 0, 0))]
        + _page_specs(NSA_PAGES_PER_STEP, width, layer, NSA_PAGES_PER_STEP),
        out_specs=pl.BlockSpec((None, NSA_KV, rows, NSA_DH), lambda b, s, pt: (b, 0, 0, 0)),
        scratch_shapes=[pltpu.VMEM((NSA_KV, rows, LANES), F32)] * 3,
    )
    return pl.pallas_call(
        _slc_decode_kernel,
        grid_spec=grid_spec,
        out_shape=jax.ShapeDtypeStruct((batch, NSA_KV, rows, NSA_DH), F32),
        compiler_params=_cparams(("parallel", "arbitrary")),
        name="slc_decode",
    )(page_table, qg, sel_steps, sel_new, expand, kv_new, bias, *([cache] * NSA_PAGES_PER_STEP))


def _win_combine_decode_kernel(q_ref, state_ref, new_ref, bias_ref, gate_ref, ocmp_ref, oslc_ref, o_ref):
    scale = NSA_DH ** -0.5
    sig = jax.nn.sigmoid(gate_ref[...])
    nst = state_ref.shape[0]
    for g in range(NSA_KV):
        kc0, vc0 = g * NSA_DH, (NSA_KV + g) * NSA_DH
        q = q_ref[g].astype(BF16)
        nt = lambda k: lax.dot_general(q, k.astype(BF16), (((1,), (1,)), ((), ())), preferred_element_type=F32)
        st = jnp.concatenate([nt(state_ref[:, kc0:kc0 + NSA_DH]), nt(new_ref[:, kc0:kc0 + NSA_DH])], axis=1)
        st = st * scale + bias_ref[g]
        p = jnp.exp(st - jnp.max(st, axis=1, keepdims=True))
        p = (p / jnp.sum(p, axis=1, keepdims=True)).astype(BF16)
        o_win = (jnp.dot(p[:, 0:nst], state_ref[:, vc0:vc0 + NSA_DH].astype(BF16), preferred_element_type=F32)
                 + jnp.dot(p[:, nst:], new_ref[:, vc0:vc0 + NSA_DH].astype(BF16), preferred_element_type=F32))
        for h in range(NSA_HPG):
            head = g * NSA_HPG + h
            rows = slice(h * T8, (h + 1) * T8)
            gate = lambda branch: sig[:, branch * N_HEADS + head:branch * N_HEADS + head + 1]
            o = gate(0) * ocmp_ref[g, rows, :] + gate(1) * oslc_ref[g, rows, :] + gate(2) * o_win[rows]
            o_ref[:, head * NSA_DH:(head + 1) * NSA_DH] = o.astype(BF16)


def _win_combine_decode(qg, state_win, kv_new, bias, gates, o_cmp, o_slc):
    batch = qg.shape[0]
    rows = NSA_HPG * T8
    width = 2 * NSA_KV * NSA_DH
    nst = state_win.shape[1]
    branch = pl.BlockSpec((None, NSA_KV, rows, NSA_DH), lambda b: (b, 0, 0, 0))
    return pl.pallas_call(
        _win_combine_decode_kernel,
        grid=(batch,),
        in_specs=[branch,
                  pl.BlockSpec((None, nst, width), lambda b: (b, 0, 0)),
                  pl.BlockSpec((None, PAGE_SIZE, width), lambda b: (b, 0, 0)),
                  pl.BlockSpec((NSA_KV, rows, nst + PAGE_SIZE), lambda b: (0, 0, 0)),
                  pl.BlockSpec((T8, LANES), lambda b: (b, COL_GATE)),
                  branch, branch],
        out_specs=pl.BlockSpec((T8, MIX_WIDTH), lambda b: (b, 0)),
        out_shape=jax.ShapeDtypeStruct((batch * T8, MIX_WIDTH), BF16),
        compiler_params=_cparams(("parallel",)),
        name="win_combine_decode",
    )(qg, state_win, kv_new, bias, gates, o_cmp, o_slc)


def _decode_buckets(past_len, wbuf):
    t = np.arange(T8)[:, None]
    j = np.arange(PAGE_SIZE)[None, :]
    last_page = _bucket_or_masked(PAGE_SIZE + t - j, np.ones((T8, PAGE_SIZE), bool))
    new_blk = _bucket_or_masked(t - j, j <= t)
    n = np.arange(NB_DEC)[None, :]
    d_cmp = past_len + t - ((n + 1) * CMP_BLOCK - 1)
    cmp_blk = _bucket_or_masked(d_cmp, d_cmp >= 0)
    i = np.arange(wbuf)[None, :]
    d_win = t + wbuf - i
    win_buf = _bucket_or_masked(d_win, (d_win >= 0) & (d_win < WINDOW) & (past_len - wbuf + i >= 0))
    return np.concatenate([last_page, new_blk, cmp_blk, win_buf, new_blk], axis=1)


def _pad_rows(x, rows):
    return jnp.pad(x, ((0, 0), (0, rows - x.shape[1]), (0, 0)))


def kernel(x_prompt, x_sample, cache_diff_kv, cache_cmp_kv, cache_slc_kv, state_win_kv, state_pool, state_conv,
           page_table, rel_bias, norm_mix, norm_mlp, norm_final, even_w_in, even_w_out, diff_lambda, diff_subln,
           pool_w, pool_scale, odd_w_in, odd_w_out, conv_w, mlp_w1, mlp_w2):
    bp, seq, d = x_prompt.shape
    bs, ts, _ = x_sample.shape
    depth = norm_mix.shape[0]
    n_pages = page_table.shape[1]
    past_len = n_pages * PAGE_SIZE
    n_phys = cache_diff_kv.shape[1]
    wbuf = state_win_kv.shape[2]
    assert seq % TQ == 0 and ts <= T8 and past_len % (NSA_PAGES_PER_STEP * PAGE_SIZE) == 0 and wbuf == WINDOW
    assert (past_len + ts - 1) // SLC_BLOCK == past_len // SLC_BLOCK < NB_DEC

    w_in_e = even_w_in.astype(BF16)
    w_out_e = even_w_out.astype(BF16)
    w_in_o = jnp.stack([_odd_weight_layout(odd_w_in[o]) for o in range(odd_w_in.shape[0])]).astype(BF16)
    w_out_o = odd_w_out.astype(BF16)
    w1 = mlp_w1.astype(BF16)
    w2 = mlp_w2.astype(BF16)
    pool_wb = pool_w.astype(BF16)

    tiles, cmp_t, expand = _nsa_buckets(seq, TQ, TK)
    tile_bias = _bias_tiles(rel_bias, tiles)
    cmp_bias_t = _bias_tiles(rel_bias, cmp_t)
    expand = jnp.asarray(expand, BF16)
    dec = _bias_tiles(rel_bias, _decode_buckets(past_len, wbuf))
    o1, o2, o3, o4 = PAGE_SIZE, 2 * PAGE_SIZE, 2 * PAGE_SIZE + NB_DEC, 2 * PAGE_SIZE + NB_DEC + wbuf
    rows_g = NSA_HPG * T8
    by_group = lambda a: a.reshape(NSA_KV, rows_g, a.shape[-1])
    dec_page_new = jnp.stack([dec[:, :, 0:o1], dec[:, :, o1:o2]])
    bias_diff_dec = jnp.broadcast_to(dec_page_new[:, None], (2, 2, N_HEADS, T8, PAGE_SIZE)).reshape(
        2, 2 * N_HEADS * T8, PAGE_SIZE)
    bias_slc_dec = dec_page_new.reshape(2, NSA_KV, rows_g, PAGE_SIZE)
    bias_cmp_dec = by_group(dec[:, :, o2:o3])
    bias_win_dec = by_group(dec[:, :, o3:])
    key_blk = np.arange(NSA_PAGES_PER_STEP * PAGE_SIZE)[None, :] // SLC_BLOCK
    expand_dec = jnp.asarray(key_blk == np.arange(LANES)[:, None], BF16)
    eye_h = jnp.eye(N_HEADS, dtype=F32)[None, None, :, None, :, None, None]
    eye_m = jnp.eye(2, dtype=F32)[None, :, None, None, None, :, None]

    cache_diff = cache_diff_kv.reshape(cache_diff_kv.shape[0], n_phys, PAGE_SIZE, 2 * N_HEADS * DA_V)
    cache_cmp = cache_cmp_kv.reshape(cache_cmp_kv.shape[0], n_phys, PAGE_SIZE, 2 * NSA_KV * NSA_DH)
    cache_slc = cache_slc_kv.reshape(cache_slc_kv.shape[0], n_phys, PAGE_SIZE, 2 * NSA_KV * NSA_DH)

    mp, ms = bp * seq, bs * T8
    xp = x_prompt.reshape(mp, d)
    xs = _pad_rows(x_sample, T8).reshape(ms, d)
    tm_p = min(1024, mp)
    outs = {k: [] for k in ("diff_p", "diff_s", "pool_p", "pool_s", "cmp_p", "cmp_s", "slc_p", "slc_s",
                            "win_p", "win_s", "conv_p", "conv_s")}
    kvw = 2 * NSA_KV * NSA_DH

    for layer in range(depth):
        if layer % 2 == 0:
            e = layer // 2
            lam_init = 0.8 - 0.6 * math.exp(-0.3 * layer)
            zp = _norm_matmul(xp, norm_mix[layer], w_in_e[e], tm_p, 512)
            zs = _norm_matmul(xs, norm_mix[layer], w_in_e[e], ms, 512)
            zp3 = zp.reshape(bp, seq, -1)
            zs3 = zs.reshape(bs, T8, -1)
            o_attn = _diff_attn_prompt(zp, diff_lambda[e], diff_subln[e], tile_bias, lam_init, bp, seq)
            o_pool = _pool_mix(zp, jnp.zeros((bp, HALO, MIX_WIDTH), F32), pool_wb[e], pool_scale[e], 0, bp, seq, 256)
            xp = _out_proj(xp, o_attn, o_pool, w_out_e[e], tm_p, 1024)
            q_s = zs3[:, :, 0:1024].reshape(bs, T8, N_HEADS, 2, DA_QK).transpose(0, 3, 2, 1, 4)
            qbd = (q_s[:, :, :, :, None, None, :] * eye_h * eye_m).reshape(bs, 2 * N_HEADS * T8, N_HEADS * DA_V)
            kv_new = _pad_rows(zs3[:, :, 1024:3072], PAGE_SIZE)
            o_dec = _diff_decode(page_table, qbd.astype(BF16), kv_new, bias_diff_dec, diff_lambda[e], diff_subln[e],
                                 cache_diff, e, lam_init)
            o_attn_s = o_dec.reshape(bs, N_HEADS, T8, DA_V).transpose(0, 2, 1, 3).reshape(ms, N_HEADS * DA_V)
            prev = jnp.pad(state_pool[e], ((0, 0), (HALO - POOL_STATE, 0), (0, 0)))
            o_pool_s = _pool_mix(zs, prev, pool_wb[e], pool_scale[e], past_len, bs, T8, T8)
            xs = _out_proj(xs, o_attn_s.astype(BF16), o_pool_s, w_out_e[e], ms, 1024)
            outs["diff_p"].append(zp3[:, :, 1024:3072].reshape(bp, seq, 2, N_HEADS, DA_V))
            outs["diff_s"].append(zs3[:, :ts, 1024:3072].reshape(bs, ts, 2, N_HEADS, DA_V))
            outs["pool_p"].append(zp3[:, seq - POOL_STATE:, 3072:])
            outs["pool_s"].append(jnp.concatenate([state_pool[e], zs3[:, :ts, 3072:]], axis=1)[:, -POOL_STATE:])
        else:
            o = layer // 2
            zp = _norm_matmul(xp, norm_mix[layer], w_in_o[o], tm_p, 640)
            zs = _norm_matmul(xs, norm_mix[layer], w_in_o[o], ms, 640)
            zp3 = zp.reshape(bp, seq, -1)
            zs3 = zs.reshape(bs, T8, -1)
            c_kc, c_ks, c_kw = COL_KC * LANES, COL_KS * LANES, COL_KW * LANES
            o_nsa = _nsa_prompt(zp, tile_bias, cmp_bias_t, expand, bp, seq)
            o_conv, tail = _short_conv(zp, jnp.zeros((bp, 8, MIX_WIDTH), F32), conv_w[o], bp, seq, 256)
            xp = _out_proj(xp, o_nsa, o_conv, w_out_o[o], tm_p, 1024)
            qg = zs3[:, :, 0:1024].reshape(bs, T8, NSA_KV, NSA_HPG, NSA_DH).transpose(0, 2, 3, 1, 4).reshape(
                bs, NSA_KV, rows_g, NSA_DH)
            kvc = _cmp_means(page_table, cache_cmp, o)
            o_cmp, sel = _cmp_select_decode(qg, kvc, zs3[:, :, c_kc:c_kc + kvw], bias_cmp_dec, past_len, ts)
            n_past_blk = past_len // SLC_BLOCK
            blk_per_step = NSA_PAGES_PER_STEP * PAGE_SIZE // SLC_BLOCK
            sel_steps = sel[..., :n_past_blk].reshape(bs, NSA_KV, T8, n_past_blk // blk_per_step, blk_per_step)
            sel_steps = jnp.pad(sel_steps.transpose(0, 1, 3, 2, 4), ((0, 0),) * 4 + ((0, LANES - blk_per_step),))
            sel_new = jnp.broadcast_to(sel[..., n_past_blk:n_past_blk + 1], (bs, NSA_KV, T8, LANES))
            o_slc = _slc_decode(page_table, qg, sel_steps, sel_new, expand_dec,
                                _pad_rows(zs3[:, :, c_ks:c_ks + kvw], PAGE_SIZE), bias_slc_dec, cache_slc, o)
            o_nsa_s = _win_combine_decode(qg, state_win_kv[o].reshape(bs, wbuf, kvw),
                                          _pad_rows(zs3[:, :, c_kw:c_kw + kvw], PAGE_SIZE), bias_win_dec, zs,
                                          o_cmp, o_slc)
            prev = jnp.pad(state_conv[o], ((0, 0), (8 - (CONV_WIDTH - 1), 0), (0, 0)))
            o_conv_s, tail_s = _short_conv(zs, prev, conv_w[o], bs, T8, T8)
            xs = _out_proj(xs, o_nsa_s, o_conv_s, w_out_o[o], ms, 1024)
            kv5 = lambda a, n: a.reshape(a.shape[0], n, 2, NSA_KV, NSA_DH)
            outs["cmp_p"].append(kv5(zp3[:, :, c_kc:c_kc + kvw], seq))
            outs["slc_p"].append(kv5(zp3[:, :, c_ks:c_ks + kvw], seq))
            outs["win_p"].append(kv5(zp3[:, seq - min(WINDOW, seq):, c_kw:c_kw + kvw], min(WINDOW, seq)))
            outs["cmp_s"].append(kv5(zs3[:, :ts, c_kc:c_kc + kvw], ts))
            outs["slc_s"].append(kv5(zs3[:, :ts, c_ks:c_ks + kvw], ts))
            n_win = min(WINDOW, past_len + ts)
            outs["win_s"].append(jnp.concatenate([state_win_kv[o], kv5(zs3[:, :ts, c_kw:c_kw + kvw], ts)],
                                                 axis=1)[:, -n_win:])
            outs["conv_p"].append(tail[:, 8 - (CONV_WIDTH - 1):])
            outs["conv_s"].append(tail_s[:, ts - (CONV_WIDTH - 1):ts])
        last = layer == depth - 1
        xp = _mlp(xp, norm_mlp[layer], w1[layer], w2[layer], norm_final, 512, 1024, last)
        xs = _mlp(xs, norm_mlp[layer], w1[layer], w2[layer], norm_final, ms, 1024, last)

    st = {k: jnp.stack(v) for k, v in outs.items()}
    return (xp.reshape(bp, seq, d), xs.reshape(bs, T8, d)[:, :ts],
            st["diff_p"], st["diff_s"], st["pool_p"], st["pool_s"], st["cmp_p"], st["cmp_s"],
            st["slc_p"], st["slc_s"], st["win_p"], st["win_s"], st["conv_p"], st["conv_s"])
```

```python
import functools
import math

import jax
import jax.numpy as jnp
import numpy as np
from jax import lax
from jax.experimental import pallas as pl
from jax.experimental.pallas import tpu as pltpu

F32 = jnp.float32
BF16 = jnp.bfloat16

D_MODEL = 2048
N_HEADS = 8
MIX_WIDTH = D_MODEL // 2
DA_QK = 64
DA_V = 128
POOL_WINDOWS = (2, 4, 8, 16)
POOL_GROUP = MIX_WIDTH // 4
POOL_STATE = 15
NSA_DH = 128
NSA_KV = 2
NSA_HPG = 4
CMP_BLOCK = 64
SLC_BLOCK = 64
SLC_TOPK = 16
WINDOW = 512
CONV_WIDTH = 3
NUM_BUCKETS = 32
MAX_DISTANCE = 128
PAGE_SIZE = 128
EPS = 1e-6
NEG = -1e30
ODD_SIZES = (1024, 512, 512, 512, 24, 1024, 1024, 1024)

LANES = 128
VMEM_LIMIT = 56 * 1024 * 1024
TQ = 256
TK = 256


def _cparams(sem):
    return pltpu.CompilerParams(dimension_semantics=sem, vmem_limit_bytes=VMEM_LIMIT)


def _bucket_np(dist):
    n = np.maximum(dist, 0)
    max_exact = NUM_BUCKETS // 2
    nf = np.maximum(n, 1).astype(np.float32)
    large = max_exact + (np.log(nf / np.float32(max_exact)) / np.float32(math.log(MAX_DISTANCE / max_exact))
                         * np.float32(NUM_BUCKETS - max_exact)).astype(np.int32)
    large = np.minimum(large, NUM_BUCKETS - 1)
    return np.where(n < max_exact, n, large).astype(np.int32)


def _bucket_or_masked(dist, valid):
    return np.where(valid, _bucket_np(dist), -1).astype(np.int32)


def _bias_kernel(table_ref, bucket_ref, o_ref):
    h = pl.program_id(0)
    bt = bucket_ref[...]
    far = table_ref[NUM_BUCKETS - 1, h]
    acc = jnp.zeros(bt.shape, F32)
    for b in range(NUM_BUCKETS - 1):
        acc = jnp.where(bt == b, table_ref[b, h] - far, acc)
    o_ref[...] = jnp.where(bt < 0, NEG, acc)


def _bias_tiles(table, buckets):
    shp = buckets.shape
    flat = jnp.asarray(buckets.reshape(-1, shp[-1]))
    rows = flat.shape[0]
    out = pl.pallas_call(
        _bias_kernel,
        grid=(N_HEADS,),
        in_specs=[pl.BlockSpec(memory_space=pltpu.SMEM),
                  pl.BlockSpec((rows, shp[-1]), lambda h: (0, 0))],
        out_specs=pl.BlockSpec((None, rows, shp[-1]), lambda h: (h, 0, 0)),
        out_shape=jax.ShapeDtypeStruct((N_HEADS, rows, shp[-1]), F32),
        compiler_params=_cparams(("arbitrary",)),
        name="bias_tiles",
    )(table, flat)
    return out.reshape((N_HEADS,) + shp)


def _rms_rows(x, g):
    y = x * lax.rsqrt(jnp.mean(x * x, axis=-1, keepdims=True) + EPS)
    return y * g


def _norm_matmul_kernel(x_ref, g_ref, w_ref, o_ref, h_scr):
    @pl.when(pl.program_id(1) == 0)
    def _():
        h_scr[...] = _rms_rows(x_ref[...], g_ref[...]).astype(BF16)

    o_ref[...] = jnp.dot(h_scr[...], w_ref[...], preferred_element_type=F32)


def _norm_matmul(x, g, w, tm, tn):
    m, d = x.shape
    n = w.shape[1]
    assert m % tm == 0 and n % tn == 0
    return pl.pallas_call(
        _norm_matmul_kernel,
        grid=(m // tm, n // tn),
        in_specs=[pl.BlockSpec((tm, d), lambda i, j: (i, 0)),
                  pl.BlockSpec((1, d), lambda i, j: (0, 0)),
                  pl.BlockSpec((d, tn), lambda i, j: (0, j))],
        out_specs=pl.BlockSpec((tm, tn), lambda i, j: (i, j)),
        out_shape=jax.ShapeDtypeStruct((m, n), F32),
        scratch_shapes=[pltpu.VMEM((tm, d), BF16)],
        compiler_params=_cparams(("parallel", "arbitrary")),
        name="norm_matmul",
    )(x, g.reshape(1, d), w)


def _out_proj_kernel(x_ref, a_ref, b_ref, wa_ref, wb_ref, o_ref):
    acc = jnp.dot(a_ref[...], wa_ref[...], preferred_element_type=F32)
    acc = acc + jnp.dot(b_ref[...], wb_ref[...], preferred_element_type=F32)
    o_ref[...] = x_ref[...] + acc


def _out_proj(x, a, b, w, tm, tn):
    m, d = x.shape
    ka = a.shape[1]
    assert m % tm == 0 and d % tn == 0
    return pl.pallas_call(
        _out_proj_kernel,
        grid=(m // tm, d // tn),
        in_specs=[pl.BlockSpec((tm, tn), lambda i, j: (i, j)),
                  pl.BlockSpec((tm, ka), lambda i, j: (i, 0)),
                  pl.BlockSpec((tm, ka), lambda i, j: (i, 0)),
                  pl.BlockSpec((ka, tn), lambda i, j: (0, j)),
                  pl.BlockSpec((ka, tn), lambda i, j: (1, j))],
        out_specs=pl.BlockSpec((tm, tn), lambda i, j: (i, j)),
        out_shape=jax.ShapeDtypeStruct((m, d), F32),
        compiler_params=_cparams(("parallel", "arbitrary")),
        name="out_proj",
    )(x, a, b, w, w)


def _mlp_kernel(x_ref, g_ref, w1_ref, w2_ref, gf_ref, o_ref, h_scr, *, final_norm):
    f = pl.program_id(1)

    @pl.when(f == 0)
    def _():
        x = x_ref[...]
        h_scr[...] = _rms_rows(x, g_ref[...]).astype(BF16)
        o_ref[...] = x

    a = jnp.maximum(jnp.dot(h_scr[...], w1_ref[...], preferred_element_type=F32), 0.0)
    a = (a * a).astype(BF16)
    o_ref[...] += jnp.dot(a, w2_ref[...], preferred_element_type=F32)

    if final_norm:
        @pl.when(f == pl.num_programs(1) - 1)
        def _():
            o_ref[...] = _rms_rows(o_ref[...], gf_ref[...])


def _mlp(x, g, w1, w2, gf, tm, tf, final_norm):
    m, d = x.shape
    ff = w1.shape[1]
    assert m % tm == 0 and ff % tf == 0
    return pl.pallas_call(
        functools.partial(_mlp_kernel, final_norm=final_norm),
        grid=(m // tm, ff // tf),
        in_specs=[pl.BlockSpec((tm, d), lambda i, f: (i, 0)),
                  pl.BlockSpec((1, d), lambda i, f: (0, 0)),
                  pl.BlockSpec((d, tf), lambda i, f: (0, f)),
                  pl.BlockSpec((tf, d), lambda i, f: (f, 0)),
                  pl.BlockSpec((1, d), lambda i, f: (0, 0))],
        out_specs=pl.BlockSpec((tm, d), lambda i, f: (i, 0)),
        out_shape=jax.ShapeDtypeStruct((m, d), F32),
        scratch_shapes=[pltpu.VMEM((tm, d), BF16)],
        compiler_params=_cparams(("parallel", "arbitrary")),
        name="mlp",
    )(x, g.reshape(1, d), w1, w2, gf.reshape(1, d))


def _flash_init(m_scr, l_scr, acc_scr):
    m_scr[...] = jnp.full(m_scr.shape, NEG, F32)
    l_scr[...] = jnp.zeros(l_scr.shape, F32)
    acc_scr[...] = jnp.zeros(acc_scr.shape, F32)


def _flash_tile(qs, kt, vt, m_scr, l_scr, acc_scr, *, heads, scale=None, bias=None, mask=None):
    s = lax.dot_general(qs, kt, (((1,), (1,)), ((), ())), preferred_element_type=F32)
    rows, tk = s.shape
    if scale is not None:
        s = s * scale
    if bias is not None or mask is not None:
        s3 = s.reshape(heads, rows // heads, tk)
        if bias is not None:
            s3 = s3 + bias
        if mask is not None:
            s3 = jnp.where(mask[None], s3, NEG)
        s = s3.reshape(rows, tk)
    m_prev = m_scr[...]
    m_next = jnp.maximum(m_prev, jnp.max(s, axis=1, keepdims=True))
    p = jnp.exp(s - pltpu.repeat(m_next, tk // LANES, axis=1))
    alpha = jnp.exp(m_prev - m_next)
    l_scr[...] = alpha * l_scr[...] + jnp.sum(p, axis=1, keepdims=True)
    acc_scr[...] = alpha * acc_scr[...] + jnp.dot(p.astype(BF16), vt, preferred_element_type=F32)
    m_scr[...] = m_next


def _kv_tile(k_ref, v_ref, j, tk):
    off = pl.multiple_of(j * tk, tk)
    return k_ref[pl.ds(off, tk), :].astype(BF16), v_ref[pl.ds(off, tk), :].astype(BF16)


def _diff_lambda_in_kernel(lv_ref, lam_init):
    lv = lv_ref[...]
    a = jnp.sum(lv[0:1] * lv[1:2], axis=1, keepdims=True)
    b = jnp.sum(lv[2:3] * lv[3:4], axis=1, keepdims=True)
    return jnp.exp(a) - jnp.exp(b) + lam_init


DIFF_HEADS_PER_STEP = 4


def _diff_attn_kernel(lv_ref, q_ref, k_ref, v_ref, bias_ref, g_ref, o_ref, m_scr, l_scr, acc_scr, *, lam_init):
    qi = pl.program_id(2)
    tq = q_ref.shape[0]
    tk = bias_ref.shape[-1]
    hp = q_ref.shape[1] // DA_V
    lane = lax.broadcasted_iota(jnp.int32, (tq, DA_V), 1)
    qs = []
    for h in range(hp):
        q = q_ref[:, h * DA_V:(h + 1) * DA_V] * (DA_QK ** -0.5)
        qs.append(jnp.concatenate([jnp.where(lane < DA_QK, q, 0.0), jnp.where(lane >= DA_QK, q, 0.0)],
                                  axis=0).astype(BF16))
    _flash_init(m_scr, l_scr, acc_scr)

    def tiles(j, which):
        off = pl.multiple_of(j * tk, tk)
        for h in range(hp):
            cols = slice(h * DA_V, (h + 1) * DA_V)
            kt = k_ref[pl.ds(off, tk), cols].astype(BF16)
            vt = v_ref[pl.ds(off, tk), cols].astype(BF16)
            _flash_tile(qs[h], kt, vt, m_scr.at[h], l_scr.at[h], acc_scr.at[h], heads=2,
                        bias=None if which is None else bias_ref[h, which][None])

    def far(j, c):
        tiles(j, None)
        return c

    lax.fori_loop(0, jnp.maximum(qi - 1, 0), far, 0)

    @pl.when(qi >= 1)
    def _():
        tiles(qi - 1, 1)

    tiles(qi, 0)

    lam = _diff_lambda_in_kernel(lv_ref, lam_init)
    for h in range(hp):
        o = acc_scr[h] / l_scr[h]
        a = o[:tq] - lam * o[tq:]
        o_ref[:, h * DA_V:(h + 1) * DA_V] = (_rms_rows(a, g_ref[...]) * (1.0 - lam_init)).astype(BF16)


def _diff_attn_prompt(z, lam_vec, subln_g, bias, lam_init, batch, seq):
    nq = seq // TQ
    hp = DIFF_HEADS_PER_STEP
    groups = N_HEADS // hp
    wide = hp * DA_V
    return pl.pallas_call(
        functools.partial(_diff_attn_kernel, lam_init=lam_init),
        grid=(batch, groups, nq),
        in_specs=[pl.BlockSpec((4, DA_QK), lambda b, h, i: (0, 0)),
                  pl.BlockSpec((TQ, wide), lambda b, h, i: (b * nq + i, h)),
                  pl.BlockSpec((seq, wide), lambda b, h, i: (b, groups + h)),
                  pl.BlockSpec((seq, wide), lambda b, h, i: (b, 2 * groups + h)),
                  pl.BlockSpec((hp, bias.shape[1], TQ, TK), lambda b, h, i: (h, 0, 0, 0)),
                  pl.BlockSpec((1, DA_V), lambda b, h, i: (0, 0))],
        out_specs=pl.BlockSpec((TQ, wide), lambda b, h, i: (b * nq + i, h)),
        out_shape=jax.ShapeDtypeStruct((batch * seq, N_HEADS * DA_V), BF16),
        scratch_shapes=[pltpu.VMEM((hp, 2 * TQ, LANES), F32)] * 3,
        compiler_params=_cparams(("parallel", "parallel", "arbitrary")),
        name="diff_attn_prompt",
    )(lam_vec, z, z, z, bias, subln_g.reshape(1, DA_V))


def _toeplitz_buckets(tq, tk):
    i = np.arange(tq)[:, None]
    j = np.arange(tk)[None, :]
    diag = _bucket_or_masked(i - j, i >= j)
    sub = _bucket_or_masked(tk + i - j, np.ones((tq, tk), bool))
    return np.stack([diag, sub])


HALO = 16


def _pool_kernel(u_ref, halo_ref, prev_ref, w_ref, scale_ref, o_ref, ext_scr, *, pos0):
    i = pl.program_id(1)
    tm = u_ref.shape[0]
    ext_scr[0:HALO, :] = jnp.where(i == 0, prev_ref[...], halo_ref[...])
    ext_scr[HALO:HALO + tm, :] = u_ref[...]
    pos = pos0 + i * tm + lax.broadcasted_iota(jnp.int32, (tm, 1), 0)
    outs = []
    for g, w in enumerate(POOL_WINDOWS):
        c0, c1 = g * POOL_GROUP, (g + 1) * POOL_GROUP
        x0 = ext_scr[HALO:HALO + tm, c0:c1]
        win = x0
        for k in range(1, w):
            win = win + ext_scr[HALO - k:HALO - k + tm, c0:c1]
        cnt = jnp.minimum(pos + 1, w).astype(F32)
        d = win / cnt - x0
        outs.append(jnp.dot(d.astype(BF16), w_ref[g], preferred_element_type=F32))
    o_ref[...] = (jnp.concatenate(outs, axis=-1) * scale_ref[...]).astype(BF16)


def _pool_mix(z, prev, w_pool, scale, pos0, batch, seq, tm):
    nb = seq // tm
    ucol = 3
    return pl.pallas_call(
        functools.partial(_pool_kernel, pos0=pos0),
        grid=(batch, nb),
        in_specs=[pl.BlockSpec((tm, MIX_WIDTH), lambda b, i: (b * nb + i, ucol)),
                  pl.BlockSpec((HALO, MIX_WIDTH),
                               lambda b, i: (jnp.maximum((b * nb + i) * (tm // HALO) - 1, 0), ucol)),
                  pl.BlockSpec((None, HALO, MIX_WIDTH), lambda b, i: (b, 0, 0)),
                  pl.BlockSpec((4, POOL_GROUP, POOL_GROUP), lambda b, i: (0, 0, 0)),
                  pl.BlockSpec((1, MIX_WIDTH), lambda b, i: (0, 0))],
        out_specs=pl.BlockSpec((tm, MIX_WIDTH), lambda b, i: (b * nb + i, 0)),
        out_shape=jax.ShapeDtypeStruct((batch * seq, MIX_WIDTH), BF16),
        scratch_shapes=[pltpu.VMEM((HALO + tm, MIX_WIDTH), F32)],
        compiler_params=_cparams(("parallel", "arbitrary")),
        name="pool_mix",
    )(z, z, prev, w_pool, scale.reshape(1, MIX_WIDTH))


COL_CB, COL_CC, COL_CH, COL_KC, COL_KS, COL_KW, COL_GATE = 8, 16, 24, 32, 36, 40, 44


def _odd_weight_layout(w_in):
    offs = np.cumsum((0,) + ODD_SIZES)
    q, kc, ks_, kw, gates, cb, cc, ch = [w_in[:, offs[i]:offs[i + 1]] for i in range(8)]
    pad = jnp.zeros((w_in.shape[0], LANES - ODD_SIZES[4]), w_in.dtype)
    return jnp.concatenate([q, cb, cc, ch, kc, ks_, kw, gates, pad], axis=1)


def _select_blocks(score_t, q0, nb):
    nbp, tq = score_t.shape
    blk = lax.broadcasted_iota(jnp.int32, (nbp, tq), 0)
    cur = (q0 + lax.broadcasted_iota(jnp.int32, (nbp, tq), 1)) // SLC_BLOCK
    forced = (blk == 0) | (blk == cur) | (blk == cur - 1)
    future = (blk > cur) | (blk >= nb)
    s = jnp.where(forced, jnp.inf, jnp.where(future, -jnp.inf, score_t))
    cnt = jnp.zeros((nbp, tq), jnp.int32)
    for n in range(nb):
        row = s[n:n + 1, :]
        beats = (row > s) | ((row == s) & (blk > n))
        cnt = cnt + beats.astype(jnp.int32)
    return (cnt < min(SLC_TOPK, nb)).astype(F32)


def _nsa_prompt_kernel(q_ref, kc_ref, vc_ref, ks_ref, vs_ref, kw_ref, vw_ref, gate_ref, tb_ref, cb_ref, e_ref,
                       o_ref, kcm_scr, vcm_scr, m_scr, l_scr, acc_scr):
    g = pl.program_id(1)
    qi = pl.program_id(2)
    tq = q_ref.shape[0]
    tk = tb_ref.shape[-1]
    seq = kc_ref.shape[0]
    nb = seq // CMP_BLOCK
    nbp = -(-nb // 8) * 8
    scale = NSA_DH ** -0.5
    hp = NSA_HPG

    @pl.when(qi == 0)
    def _():
        kcm_scr[...] = jnp.zeros(kcm_scr.shape, BF16)
        vcm_scr[...] = jnp.zeros(vcm_scr.shape, BF16)
        kcm_scr[0:nb, :] = (jnp.sum(kc_ref[...].reshape(nb, CMP_BLOCK, NSA_DH), axis=1) / CMP_BLOCK).astype(BF16)
        vcm_scr[0:nb, :] = (jnp.sum(vc_ref[...].reshape(nb, CMP_BLOCK, NSA_DH), axis=1) / CMP_BLOCK).astype(BF16)

    qh = [q_ref[:, h * NSA_DH:(h + 1) * NSA_DH].astype(BF16) for h in range(hp)]
    qs = jnp.concatenate(qh, axis=0)

    kcm = kcm_scr[...]
    vcm = vcm_scr[...]
    o_cmp = []
    p_grp_t = jnp.zeros((LANES, tq), F32)
    for h in range(hp):
        lt = lax.dot_general(kcm, qh[h], (((1,), (1,)), ((), ())), preferred_element_type=F32) * scale
        bt = cb_ref[h]
        lt = lt + bt
        p = jnp.exp(lt - jnp.max(lt, axis=0, keepdims=True))
        p = p / jnp.sum(p, axis=0, keepdims=True)
        p = jnp.where(bt > 0.5 * NEG, p, 0.0)
        p_grp_t = p_grp_t + p
        o_cmp.append(jnp.dot(p.T.astype(BF16), vcm, preferred_element_type=F32))

    sel_t = _select_blocks(p_grp_t[0:nbp, :], qi * tq, nb)
    if nbp < LANES:
        sel_t = jnp.concatenate([sel_t, jnp.zeros((LANES - nbp, tq), F32)], axis=0)
    sel = sel_t.T.astype(BF16)

    upd = functools.partial(_flash_tile, m_scr=m_scr, l_scr=l_scr, acc_scr=acc_scr, heads=hp, scale=scale)

    def sel_mask(j):
        return jnp.dot(sel, e_ref[j], preferred_element_type=F32) > 0.5

    _flash_init(m_scr, l_scr, acc_scr)

    def far(j, c):
        kt, vt = _kv_tile(ks_ref, vs_ref, j, tk)
        upd(qs, kt, vt, mask=sel_mask(j))
        return c

    lax.fori_loop(0, jnp.maximum(qi - 1, 0), far, 0)

    @pl.when(qi >= 1)
    def _():
        kt, vt = _kv_tile(ks_ref, vs_ref, qi - 1, tk)
        upd(qs, kt, vt, bias=tb_ref[:, 1], mask=sel_mask(qi - 1))

    kt, vt = _kv_tile(ks_ref, vs_ref, qi, tk)
    upd(qs, kt, vt, bias=tb_ref[:, 0], mask=sel_mask(qi))
    o_slc = acc_scr[...] / l_scr[...]

    _flash_init(m_scr, l_scr, acc_scr)

    @pl.when(qi >= 2)
    def _():
        kt, vt = _kv_tile(kw_ref, vw_ref, qi - 2, tk)
        upd(qs, kt, vt, bias=tb_ref[:, 2])

    @pl.when(qi >= 1)
    def _():
        kt, vt = _kv_tile(kw_ref, vw_ref, qi - 1, tk)
        upd(qs, kt, vt, bias=tb_ref[:, 1])

    kt, vt = _kv_tile(kw_ref, vw_ref, qi, tk)
    upd(qs, kt, vt, bias=tb_ref[:, 0])
    o_win = acc_scr[...] / l_scr[...]

    sig = jax.nn.sigmoid(gate_ref[...])
    lane = lax.broadcasted_iota(jnp.int32, sig.shape, 1)

    def gate(branch, h):
        col = branch * N_HEADS + g * hp + h
        return jnp.sum(jnp.where(lane == col, sig, 0.0), axis=1, keepdims=True)

    for h in range(hp):
        rows = slice(h * tq, (h + 1) * tq)
        o = gate(0, h) * o_cmp[h] + gate(1, h) * o_slc[rows] + gate(2, h) * o_win[rows]
        o_ref[:, h * NSA_DH:(h + 1) * NSA_DH] = o.astype(BF16)


def _nsa_buckets(seq, tq, tk):
    i = np.arange(tq)[:, None]
    j = np.arange(tk)[None, :]
    toe = _toeplitz_buckets(tq, tk)
    assert WINDOW == 2 * tk and tq == tk
    win2 = np.where(i < j, NUM_BUCKETS - 1, -1).astype(np.int32)
    tiles = np.concatenate([toe, win2[None]])
    nb = seq // CMP_BLOCK
    blk_end = (np.arange(LANES)[:, None] + 1) * CMP_BLOCK - 1
    dist = np.arange(seq)[None, :] - blk_end
    cmp_t = _bucket_or_masked(dist, (dist >= 0) & (np.arange(LANES)[:, None] < nb))
    nk = seq // tk
    key_blk = (np.arange(nk)[:, None, None] * tk + np.arange(tk)[None, None, :]) // SLC_BLOCK
    expand = (key_blk == np.arange(LANES)[None, :, None]).astype(np.float32)
    return tiles, cmp_t, expand


def _nsa_prompt(z, tile_bias, cmp_bias_t, expand, batch, seq):
    nq = seq // TQ
    nk = seq // TK
    kv = lambda col: pl.BlockSpec((seq, NSA_DH), lambda b, g, i: (b, col + g))
    return pl.pallas_call(
        _nsa_prompt_kernel,
        grid=(batch, NSA_KV, nq),
        in_specs=[pl.BlockSpec((TQ, NSA_HPG * NSA_DH), lambda b, g, i: (b * nq + i, g)),
                  kv(COL_KC), kv(COL_KC + 2), kv(COL_KS), kv(COL_KS + 2), kv(COL_KW), kv(COL_KW + 2),
                  pl.BlockSpec((TQ, LANES), lambda b, g, i: (b * nq + i, COL_GATE)),
                  pl.BlockSpec((NSA_HPG, 3, TQ, TK), lambda b, g, i: (g, 0, 0, 0)),
                  pl.BlockSpec((NSA_HPG, LANES, TQ), lambda b, g, i: (g, 0, i)),
                  pl.BlockSpec((nk, LANES, TK), lambda b, g, i: (0, 0, 0))],
        out_specs=pl.BlockSpec((TQ, NSA_HPG * NSA_DH), lambda b, g, i: (b * nq + i, g)),
        out_shape=jax.ShapeDtypeStruct((batch * seq, MIX_WIDTH), BF16),
        scratch_shapes=[pltpu.VMEM((LANES, NSA_DH), BF16), pltpu.VMEM((LANES, NSA_DH), BF16)]
        + [pltpu.VMEM((NSA_HPG * TQ, LANES), F32)] * 3,
        compiler_params=_cparams(("parallel", "parallel", "arbitrary")),
        name="nsa_prompt",
    )(z, z, z, z, z, z, z, z, tile_bias, cmp_bias_t, expand)


def _conv_kernel(cb_ref, cc_ref, ch_ref, hc_ref, hh_ref, prev_ref, w_ref, o_ref, tail_ref, ext_scr):
    i = pl.program_id(1)
    tm = cb_ref.shape[0]
    e = cc_ref[...] * ch_ref[...]
    ext_scr[0:8, :] = jnp.where(i == 0, prev_ref[...], hc_ref[...] * hh_ref[...])
    ext_scr[8:8 + tm, :] = e
    w = w_ref[...]
    y = w[0:1] * ext_scr[6:6 + tm, :]
    y = y + w[1:2] * ext_scr[7:7 + tm, :]
    y = y + w[2:3] * e
    o_ref[...] = (cb_ref[...] * y).astype(BF16)

    @pl.when(i == pl.num_programs(1) - 1)
    def _():
        tail_ref[...] = e[tm - 8:tm, :]


def _short_conv(z, prev, w_conv, batch, seq, tm):
    nb = seq // tm
    cw = MIX_WIDTH // LANES
    blk = lambda c: pl.BlockSpec((tm, MIX_WIDTH), lambda b, i: (b * nb + i, c // cw))
    halo = lambda c: pl.BlockSpec((8, MIX_WIDTH), lambda b, i: (jnp.maximum((b * nb + i) * (tm // 8) - 1, 0), c // cw))
    return pl.pallas_call(
        _conv_kernel,
        grid=(batch, nb),
        in_specs=[blk(COL_CB), blk(COL_CC), blk(COL_CH), halo(COL_CC), halo(COL_CH),
                  pl.BlockSpec((None, 8, MIX_WIDTH), lambda b, i: (b, 0, 0)),
                  pl.BlockSpec((8, MIX_WIDTH), lambda b, i: (0, 0))],
        out_specs=[pl.BlockSpec((tm, MIX_WIDTH), lambda b, i: (b * nb + i, 0)),
                   pl.BlockSpec((None, 8, MIX_WIDTH), lambda b, i: (b, 0, 0))],
        out_shape=[jax.ShapeDtypeStruct((batch * seq, MIX_WIDTH), BF16),
                   jax.ShapeDtypeStruct((batch, 8, MIX_WIDTH), F32)],
        scratch_shapes=[pltpu.VMEM((8 + tm, MIX_WIDTH), F32)],
        compiler_params=_cparams(("parallel", "arbitrary")),
        name="short_conv",
    )(z, z, z, z, z, prev, jnp.pad(w_conv, ((0, 8 - CONV_WIDTH), (0, 0))))


T8 = 8
NEW_ROWS = 16
DIFF_PAGES_PER_STEP = 8
NSA_PAGES_PER_STEP = 16
NSA_ROW = 2 * NSA_KV
V_SHIFT = NSA_KV
NB_DEC = 288


def _nt(a, b):
    return lax.dot_general(a, b, (((1,), (1,)), ((), ())), preferred_element_type=F32)


def _softmax_update(st, v_tiles, width, m_ref, l_ref, acc_ref):
    m_prev = m_ref[...]
    m_next = jnp.maximum(m_prev, jnp.max(st, axis=1, keepdims=True))
    p = jnp.exp(st - pltpu.repeat(m_next, st.shape[1] // LANES, axis=1))
    alpha = jnp.exp(m_prev - m_next)
    l_ref[...] = alpha * l_ref[...] + jnp.sum(p, axis=1, keepdims=True)
    pv = jnp.zeros(acc_ref.shape, F32)
    for k, vt in enumerate(v_tiles):
        pv = pv + jnp.dot(p[:, k * width:(k + 1) * width].astype(BF16), vt, preferred_element_type=F32)
    acc_ref[...] = alpha * acc_ref[...] + pv
    m_ref[...] = m_next


def _diff_decode_kernel(pt_ref, q_ref, new_ref, mask_ref, newmask_ref, lv_ref, g_ref, *rest, lam_init):
    pages = rest[:DIFF_PAGES_PER_STEP]
    o_ref, m_scr, l_scr, acc_scr = rest[DIFF_PAGES_PER_STEP:]
    s = pl.program_id(1)
    last = pl.num_programs(1) - 1
    width = PAGE_SIZE * N_HEADS

    @pl.when(s == 0)
    def _():
        _flash_init(m_scr, l_scr, acc_scr)

    q = q_ref[...] * (DA_QK ** -0.5)
    lane = lax.broadcasted_iota(jnp.int32, q.shape, 1)
    qs = jnp.concatenate([jnp.where(lane < DA_QK, q, 0.0), jnp.where(lane >= DA_QK, q, 0.0)], axis=0).astype(BF16)
    upd = functools.partial(_softmax_update, m_ref=m_scr, l_ref=l_scr, acc_ref=acc_scr)

    is_last = s == last
    tiles, values = [], []
    for k, pg in enumerate(pages):
        st = _nt(qs, pg[:, 0].reshape(width, DA_V).astype(BF16))
        if k == DIFF_PAGES_PER_STEP - 1:
            st = st + jnp.where(is_last, mask_ref[1], mask_ref[0])
        else:
            st = st + mask_ref[0]
        tiles.append(st)
        values.append(pg[:, 1].reshape(width, DA_V).astype(BF16))
    upd(jnp.concatenate(tiles, axis=1), values, width)

    @pl.when(is_last)
    def _():
        nw = NEW_ROWS * N_HEADS
        st = _nt(qs, new_ref[:, 0].reshape(nw, DA_V).astype(BF16)) + newmask_ref[...]
        upd(st, [new_ref[:, 1].reshape(nw, DA_V).astype(BF16)], nw)
        o = acc_scr[...] / l_scr[...]
        half = o.shape[0] // 2
        a = o[:half] - _diff_lambda_in_kernel(lv_ref, lam_init) * o[half:]
        o_ref[...] = _rms_rows(a, g_ref[...]) * (1.0 - lam_init)


def _diff_decode(page_table, q_ht, kv_new, masks, new_mask, lam_vec, subln_g, cache, layer, lam_init):
    batch, n_pages = page_table.shape
    steps = n_pages // DIFF_PAGES_PER_STEP
    rows = q_ht.shape[1]

    def page_spec(k):
        return pl.BlockSpec((None, None, PAGE_SIZE, 2, N_HEADS, DA_V),
                            lambda b, s, pt: (layer, pt[b, s * DIFF_PAGES_PER_STEP + k], 0, 0, 0, 0))

    grid_spec = pltpu.PrefetchScalarGridSpec(
        num_scalar_prefetch=1,
        grid=(batch, steps),
        in_specs=[pl.BlockSpec((None, rows, DA_V), lambda b, s, pt: (b, 0, 0)),
                  pl.BlockSpec((None, NEW_ROWS, 2, N_HEADS, DA_V), lambda b, s, pt: (b, 0, 0, 0, 0)),
                  pl.BlockSpec(masks.shape, lambda b, s, pt: (0, 0, 0)),
                  pl.BlockSpec(new_mask.shape, lambda b, s, pt: (0, 0)),
                  pl.BlockSpec((4, DA_QK), lambda b, s, pt: (0, 0)),
                  pl.BlockSpec((1, DA_V), lambda b, s, pt: (0, 0))]
        + [page_spec(k) for k in range(DIFF_PAGES_PER_STEP)],
        out_specs=pl.BlockSpec((None, rows, DA_V), lambda b, s, pt: (b, 0, 0)),
        scratch_shapes=[pltpu.VMEM((2 * rows, LANES), F32)] * 3,
    )
    return pl.pallas_call(
        functools.partial(_diff_decode_kernel, lam_init=lam_init),
        grid_spec=grid_spec,
        out_shape=jax.ShapeDtypeStruct((batch, rows, DA_V), F32),
        compiler_params=_cparams(("parallel", "arbitrary")),
        name="diff_decode",
    )(page_table, q_ht, kv_new, masks, new_mask, lam_vec, subln_g.reshape(1, DA_V),
      *([cache] * DIFF_PAGES_PER_STEP))


def _nsa_page_specs(layer):
    def spec(k):
        return pl.BlockSpec((None, None, PAGE_SIZE * NSA_ROW, NSA_DH),
                            lambda b, s, pt: (layer, pt[b, s * NSA_PAGES_PER_STEP + k], 0, 0))
    return [spec(k) for k in range(NSA_PAGES_PER_STEP)]


def _cmp_means_kernel(pt_ref, *rest):
    pages = rest[:NSA_PAGES_PER_STEP]
    o_ref = rest[NSA_PAGES_PER_STEP]
    per_blk = CMP_BLOCK * NSA_ROW // 8
    low = lax.broadcasted_iota(jnp.int32, (8, NSA_DH), 0) < NSA_ROW
    out = []
    for pg in pages:
        x = pg[...].reshape(PAGE_SIZE * NSA_ROW // 8, 8, NSA_DH)
        t0 = jnp.sum(x[0:per_blk], axis=0)
        t1 = jnp.sum(x[per_blk:2 * per_blk], axis=0)
        t0 = t0 + pltpu.roll(t0, NSA_ROW, axis=0)
        t1 = t1 + pltpu.roll(t1, NSA_ROW, axis=0)
        out.append(jnp.where(low, t0, t1) / CMP_BLOCK)
    o_ref[...] = jnp.concatenate(out, axis=0)


def _cmp_means(page_table, cache, layer):
    batch, n_pages = page_table.shape
    steps = n_pages // NSA_PAGES_PER_STEP
    per_step = NSA_PAGES_PER_STEP * 8
    grid_spec = pltpu.PrefetchScalarGridSpec(
        num_scalar_prefetch=1,
        grid=(batch, steps),
        in_specs=_nsa_page_specs(layer),
        out_specs=pl.BlockSpec((None, per_step, NSA_DH), lambda b, s, pt: (b, s, 0)),
    )
    return pl.pallas_call(
        _cmp_means_kernel,
        grid_spec=grid_spec,
        out_shape=jax.ShapeDtypeStruct((batch, steps * per_step, NSA_DH), F32),
        compiler_params=_cparams(("parallel", "arbitrary")),
        name="cmp_means",
    )(page_table, *([cache] * NSA_PAGES_PER_STEP))


def _cmp_select_decode_kernel(q_ref, kvc_ref, new_ref, tile_ref, ocmp_ref, sel_ref, k_scr, *, past_len, n_new):
    n_past = kvc_ref.shape[0]
    width = NB_DEC * NSA_ROW
    scale = NSA_DH ** -0.5
    k_scr[...] = jnp.zeros(k_scr.shape, F32)
    k_scr[0:n_past, :] = kvc_ref[...]
    real = lax.broadcasted_iota(jnp.int32, new_ref.shape, 0) < n_new
    tot = jnp.sum(jnp.where(real, new_ref[...], 0.0), axis=0, keepdims=True) / CMP_BLOCK
    new4 = jnp.concatenate([tot[:, c * NSA_DH:(c + 1) * NSA_DH] for c in range(NSA_ROW)]
                           + [jnp.zeros((8 - NSA_ROW, NSA_DH), F32)], axis=0)
    k_scr[n_past:n_past + 8, :] = new4

    q = q_ref[...].astype(BF16)
    tile = tile_ref[...]
    lg = _nt(q, k_scr[0:width, :].astype(BF16)) * scale + tile
    p = jnp.exp(lg - jnp.max(lg, axis=1, keepdims=True))
    p = p / jnp.sum(p, axis=1, keepdims=True)
    p = jnp.where(tile > 0.5 * NEG, p, 0.0)
    ocmp_ref[...] = jnp.dot(p.astype(BF16), k_scr[V_SHIFT:V_SHIFT + width, :].astype(BF16),
                            preferred_element_type=F32)

    lane = lax.broadcasted_iota(jnp.int32, (T8, width), 1)
    blk = lane // NSA_ROW
    cur = (past_len + lax.broadcasted_iota(jnp.int32, (T8, width), 0)) // SLC_BLOCK
    rows_g = NSA_HPG * T8
    for g in range(NSA_KV):
        mine = (lane % NSA_ROW) == g
        score = sum(p[g * rows_g + h * T8:g * rows_g + (h + 1) * T8] for h in range(NSA_HPG))
        forced = mine & ((blk == 0) | (blk == cur) | (blk == cur - 1))
        dead = (blk > cur) | jnp.logical_not(mine)
        sc = jnp.where(forced, jnp.inf, jnp.where(dead, -jnp.inf, score))
        taken = jnp.zeros((T8, width), jnp.bool_)
        for _ in range(SLC_TOPK):
            sm = jnp.where(taken, -jnp.inf, sc)
            cand = (sm == jnp.max(sm, axis=1, keepdims=True)) & jnp.logical_not(taken)
            idx = jnp.min(jnp.where(cand, lane, width), axis=1, keepdims=True)
            taken = taken | (lane == idx)
        sel_ref[g * T8:(g + 1) * T8, :] = taken.astype(F32)


def _cmp_select_decode(qg, kvc, kv_new, tile, past_len, n_new):
    batch, rows, _ = qg.shape
    width = NB_DEC * NSA_ROW
    n_past = kvc.shape[1]
    return pl.pallas_call(
        functools.partial(_cmp_select_decode_kernel, past_len=past_len, n_new=n_new),
        grid=(batch,),
        in_specs=[pl.BlockSpec((None, rows, NSA_DH), lambda b: (b, 0, 0)),
                  pl.BlockSpec((None, n_past, NSA_DH), lambda b: (b, 0, 0)),
                  pl.BlockSpec((None, T8, NSA_ROW * NSA_DH), lambda b: (b, 0, 0)),
                  pl.BlockSpec((rows, width), lambda b: (0, 0))],
        out_specs=[pl.BlockSpec((None, rows, NSA_DH), lambda b: (b, 0, 0)),
                   pl.BlockSpec((None, NSA_KV * T8, width), lambda b: (b, 0, 0))],
        out_shape=[jax.ShapeDtypeStruct((batch, rows, NSA_DH), F32),
                   jax.ShapeDtypeStruct((batch, NSA_KV * T8, width), F32)],
        scratch_shapes=[pltpu.VMEM((width + 8, NSA_DH), F32)],
        compiler_params=_cparams(("parallel",)),
        name="cmp_select_decode",
    )(qg, kvc, kv_new, tile)


def _masked_scores(st, sel, expand):
    rows, n = st.shape
    add = (jnp.dot(sel.astype(BF16), expand, preferred_element_type=F32) - 1.0) * (-NEG)
    st4 = st.reshape(NSA_KV, NSA_HPG, T8, n) + add.reshape(NSA_KV, 1, T8, n)
    return st4.reshape(rows, n)


def _slc_decode_kernel(pt_ref, q_ref, sel_ref, selnew_ref, e_ref, new_ref, tile_ref, newtile_ref, *rest):
    pages = rest[:NSA_PAGES_PER_STEP]
    o_ref, m_scr, l_scr, acc_scr = rest[NSA_PAGES_PER_STEP:]
    s = pl.program_id(1)
    last = pl.num_programs(1) - 1
    scale = NSA_DH ** -0.5
    width = PAGE_SIZE * NSA_ROW

    @pl.when(s == 0)
    def _():
        _flash_init(m_scr, l_scr, acc_scr)

    q = q_ref[...].astype(BF16)
    upd = functools.partial(_softmax_update, m_ref=m_scr, l_ref=l_scr, acc_ref=acc_scr)
    is_last = (s == last).astype(F32)
    tiles, values = [], []
    for k, pg in enumerate(pages):
        x = pg[...]
        st = _nt(q, x.astype(BF16)) * scale
        if k == NSA_PAGES_PER_STEP - 1:
            st = st + is_last * tile_ref[...]
        tiles.append(st)
        values.append(pltpu.roll(x, width - V_SHIFT, axis=0).astype(BF16))
    upd(_masked_scores(jnp.concatenate(tiles, axis=1), sel_ref[...], e_ref[...]), values, width)

    @pl.when(s == last)
    def _():
        x = new_ref[...]
        n = x.shape[0]
        st = _nt(q, x.astype(BF16)) * scale + newtile_ref[...]
        upd(_masked_scores(st, selnew_ref[...], e_ref[:, 0:n]), [pltpu.roll(x, n - V_SHIFT, axis=0).astype(BF16)], n)
        o_ref[...] = acc_scr[...] / l_scr[...]


def _slc_decode(page_table, qg, sel, expand, kv_new, tile, new_tile, cache, layer):
    batch, n_pages = page_table.shape
    steps = n_pages // NSA_PAGES_PER_STEP
    rows = qg.shape[1]
    keys = NSA_PAGES_PER_STEP * PAGE_SIZE * NSA_ROW
    grid_spec = pltpu.PrefetchScalarGridSpec(
        num_scalar_prefetch=1,
        grid=(batch, steps),
        in_specs=[pl.BlockSpec((None, rows, NSA_DH), lambda b, s, pt: (b, 0, 0)),
                  pl.BlockSpec((None, NSA_KV * T8, LANES), lambda b, s, pt: (b, 0, s)),
                  pl.BlockSpec((None, NSA_KV * T8, LANES), lambda b, s, pt: (b, 0, steps)),
                  pl.BlockSpec((LANES, keys), lambda b, s, pt: (0, 0)),
                  pl.BlockSpec((None,) + kv_new.shape[1:], lambda b, s, pt: (b, 0, 0)),
                  pl.BlockSpec(tile.shape, lambda b, s, pt: (0, 0)),
                  pl.BlockSpec(new_tile.shape, lambda b, s, pt: (0, 0))]
        + _nsa_page_specs(layer),
        out_specs=pl.BlockSpec((None, rows, NSA_DH), lambda b, s, pt: (b, 0, 0)),
        scratch_shapes=[pltpu.VMEM((rows, LANES), F32)] * 3,
    )
    return pl.pallas_call(
        _slc_decode_kernel,
        grid_spec=grid_spec,
        out_shape=jax.ShapeDtypeStruct((batch, rows, NSA_DH), F32),
        compiler_params=_cparams(("parallel", "arbitrary")),
        name="slc_decode",
    )(page_table, qg, sel, sel, expand, kv_new, tile, new_tile, *([cache] * NSA_PAGES_PER_STEP))


def _win_combine_decode_kernel(q_ref, state_ref, new_ref, tile_ref, gate_ref, ocmp_ref, oslc_ref, o_ref):
    scale = NSA_DH ** -0.5
    q = q_ref[...].astype(BF16)
    xs, xn = state_ref[...], new_ref[...]
    ns, nn = xs.shape[0], xn.shape[0]
    st = jnp.concatenate([_nt(q, xs.astype(BF16)), _nt(q, xn.astype(BF16))], axis=1) * scale + tile_ref[...]
    p = jnp.exp(st - jnp.max(st, axis=1, keepdims=True))
    p = (p / jnp.sum(p, axis=1, keepdims=True)).astype(BF16)
    o_win = (jnp.dot(p[:, 0:ns], pltpu.roll(xs, ns - V_SHIFT, axis=0).astype(BF16), preferred_element_type=F32)
             + jnp.dot(p[:, ns:], pltpu.roll(xn, nn - V_SHIFT, axis=0).astype(BF16), preferred_element_type=F32))
    sig = jax.nn.sigmoid(gate_ref[...])
    for g in range(NSA_KV):
        for h in range(NSA_HPG):
            head = g * NSA_HPG + h
            rows = slice(head * T8, (head + 1) * T8)
            gate = lambda branch: sig[:, branch * N_HEADS + head:branch * N_HEADS + head + 1]
            o = gate(0) * ocmp_ref[rows, :] + gate(1) * oslc_ref[rows, :] + gate(2) * o_win[rows]
            o_ref[:, head * NSA_DH:(head + 1) * NSA_DH] = o.astype(BF16)


def _win_combine_decode(qg, state_win, kv_new, tile, gates, o_cmp, o_slc):
    batch, rows, _ = qg.shape
    branch = pl.BlockSpec((None, rows, NSA_DH), lambda b: (b, 0, 0))
    return pl.pallas_call(
        _win_combine_decode_kernel,
        grid=(batch,),
        in_specs=[branch,
                  pl.BlockSpec((None,) + state_win.shape[1:], lambda b: (b, 0, 0)),
                  pl.BlockSpec((None,) + kv_new.shape[1:], lambda b: (b, 0, 0)),
                  pl.BlockSpec(tile.shape, lambda b: (0, 0)),
                  pl.BlockSpec((T8, LANES), lambda b: (b, COL_GATE)),
                  branch, branch],
        out_specs=pl.BlockSpec((T8, MIX_WIDTH), lambda b: (b, 0)),
        out_shape=jax.ShapeDtypeStruct((batch * T8, MIX_WIDTH), BF16),
        compiler_params=_cparams(("parallel",)),
        name="win_combine_decode",
    )(qg, state_win, kv_new, tile, gates, o_cmp, o_slc)


def _decode_buckets(past_len, wbuf):
    t = np.arange(T8)[:, None]
    dpos = np.repeat(np.arange(PAGE_SIZE), N_HEADS)[None, :]
    dnew = np.repeat(np.arange(NEW_ROWS), N_HEADS)[None, :]
    npos = np.repeat(np.arange(PAGE_SIZE), NSA_ROW)[None, :]
    nnew = np.repeat(np.arange(PAGE_SIZE // NSA_ROW), NSA_ROW)[None, :]
    nblk = np.repeat(np.arange(NB_DEC), NSA_ROW)[None, :]
    nwin = np.repeat(np.arange(wbuf), NSA_ROW)[None, :]
    d_cmp = past_len + t - ((nblk + 1) * CMP_BLOCK - 1)
    d_win = t + wbuf - nwin
    parts = [
        _bucket_or_masked(PAGE_SIZE + t - dpos, np.ones((T8, dpos.shape[1]), bool)),
        _bucket_or_masked(t - dnew, dnew <= t),
        _bucket_or_masked(PAGE_SIZE + t - npos, np.ones((T8, npos.shape[1]), bool)),
        _bucket_or_masked(t - nnew, nnew <= t),
        _bucket_or_masked(d_cmp, d_cmp >= 0),
        _bucket_or_masked(d_win, (d_win >= 0) & (d_win < WINDOW) & (past_len - wbuf + nwin >= 0)),
    ]
    return np.concatenate(parts, axis=1), np.cumsum([0] + [p.shape[1] for p in parts])


def _pad_rows(x, rows):
    return jnp.pad(x, ((0, 0), (0, rows - x.shape[1])) + ((0, 0),) * (x.ndim - 2))


def _tile_cfg():
    return dict(tm=1024, tn_even=1024, tn_odd=1152, tn_out=1024, tm_mlp=512, tf=1024, tm_pool=256)


def kernel(x_prompt, x_sample, cache_diff_kv, cache_cmp_kv, cache_slc_kv, state_win_kv, state_pool, state_conv,
           page_table, rel_bias, norm_mix, norm_mlp, norm_final, even_w_in, even_w_out, diff_lambda, diff_subln,
           pool_w, pool_scale, odd_w_in, odd_w_out, conv_w, mlp_w1, mlp_w2):
    bp, seq, d = x_prompt.shape
    bs, ts, _ = x_sample.shape
    depth = norm_mix.shape[0]
    n_pages = page_table.shape[1]
    past_len = n_pages * PAGE_SIZE
    n_phys = cache_diff_kv.shape[1]
    wbuf = state_win_kv.shape[2]
    n_past_blk = past_len // SLC_BLOCK
    assert seq % TQ == 0 and ts <= T8 and n_pages % NSA_PAGES_PER_STEP == 0 and wbuf == WINDOW
    assert (past_len + ts - 1) // SLC_BLOCK == n_past_blk < NB_DEC
    cfg = _tile_cfg()

    w_in_e = even_w_in.astype(BF16)
    w_out_e = even_w_out.astype(BF16)
    w_in_o = jnp.stack([_odd_weight_layout(odd_w_in[o].astype(BF16)) for o in range(odd_w_in.shape[0])])
    w_out_o = odd_w_out.astype(BF16)
    w1 = mlp_w1.astype(BF16)
    w2 = mlp_w2.astype(BF16)
    pool_wb = pool_w.astype(BF16)

    tiles, cmp_t, expand = _nsa_buckets(seq, TQ, TK)
    tile_bias = _bias_tiles(rel_bias, tiles)
    cmp_bias_t = _bias_tiles(rel_bias, cmp_t)
    expand = jnp.asarray(expand, BF16)
    dec_buckets, off = _decode_buckets(past_len, wbuf)
    dec = _bias_tiles(rel_bias, dec_buckets)
    part = lambda i: dec[:, :, off[i]:off[i + 1]]

    def diff_rows(x):
        n = x.shape[-1]
        own = (np.arange(n) % N_HEADS)[None, None, :] == np.arange(N_HEADS)[:, None, None]
        x = jnp.where(own, x, NEG)
        return jnp.broadcast_to(x[None], (2,) + x.shape).reshape(2 * N_HEADS * T8, n)

    def nsa_rows(x):
        n = x.shape[-1]
        own = (np.arange(n) % NSA_ROW)[None, None, :] == (np.arange(N_HEADS) // NSA_HPG)[:, None, None]
        return jnp.where(own, x, NEG).reshape(N_HEADS * T8, n)

    diff_masks = jnp.stack([diff_rows(jnp.zeros_like(part(0))), diff_rows(part(0))])
    diff_new_mask = diff_rows(part(1))
    slc_tile = nsa_rows(part(2))
    nsa_new_tile = nsa_rows(part(3))
    cmp_tile = nsa_rows(part(4))
    win_tile = jnp.concatenate([nsa_rows(part(5)), nsa_new_tile], axis=1)
    key = np.arange(NSA_PAGES_PER_STEP * PAGE_SIZE * NSA_ROW)
    sel_lane = (key // (NSA_ROW * SLC_BLOCK)) * NSA_ROW + key % NSA_ROW
    is_key_row = (key % NSA_ROW) < NSA_KV
    expand_dec = jnp.asarray((np.arange(LANES)[:, None] == sel_lane[None, :]) & is_key_row[None, :], BF16)

    cache_cmp = cache_cmp_kv.reshape(cache_cmp_kv.shape[0], n_phys, PAGE_SIZE * NSA_ROW, NSA_DH)
    cache_slc = cache_slc_kv.reshape(cache_slc_kv.shape[0], n_phys, PAGE_SIZE * NSA_ROW, NSA_DH)

    mp, ms = bp * seq, bs * T8
    xp = x_prompt.reshape(mp, d)
    xs = _pad_rows(x_sample, T8).reshape(ms, d)
    tm_p = min(cfg["tm"], mp)
    outs = {k: [] for k in ("diff_p", "diff_s", "pool_p", "pool_s", "cmp_p", "cmp_s", "slc_p", "slc_s",
                            "win_p", "win_s", "conv_p", "conv_s")}
    kvw = NSA_ROW * NSA_DH

    for layer in range(depth):
        if layer % 2 == 0:
            e = layer // 2
            lam_init = 0.8 - 0.6 * math.exp(-0.3 * layer)
            zp = _norm_matmul(xp, norm_mix[layer], w_in_e[e], tm_p, cfg["tn_even"])
            zs = _norm_matmul(xs, norm_mix[layer], w_in_e[e], ms, cfg["tn_even"])
            zp3 = zp.reshape(bp, seq, -1)
            zs3 = zs.reshape(bs, T8, -1)
            o_attn = _diff_attn_prompt(zp, diff_lambda[e], diff_subln[e], tile_bias, lam_init, bp, seq)
            o_pool = _pool_mix(zp, jnp.zeros((bp, HALO, MIX_WIDTH), F32), pool_wb[e], pool_scale[e], 0, bp, seq,
                               cfg["tm_pool"])
            xp = _out_proj(xp, o_attn, o_pool, w_out_e[e], tm_p, cfg["tn_out"])
            q_ht = zs3[:, :, 0:1024].reshape(bs, T8, N_HEADS, DA_V).transpose(0, 2, 1, 3).reshape(
                bs, N_HEADS * T8, DA_V)
            kv_new = _pad_rows(zs3[:, :, 1024:3072].reshape(bs, T8, 2, N_HEADS, DA_V), NEW_ROWS)
            o_dec = _diff_decode(page_table, q_ht, kv_new, diff_masks, diff_new_mask, diff_lambda[e], diff_subln[e],
                                 cache_diff_kv, e, lam_init)
            o_attn_s = o_dec.reshape(bs, N_HEADS, T8, DA_V).transpose(0, 2, 1, 3).reshape(ms, N_HEADS * DA_V)
            prev = jnp.pad(state_pool[e], ((0, 0), (HALO - POOL_STATE, 0), (0, 0)))
            o_pool_s = _pool_mix(zs, prev, pool_wb[e], pool_scale[e], past_len, bs, T8, T8)
            xs = _out_proj(xs, o_attn_s.astype(BF16), o_pool_s, w_out_e[e], ms, cfg["tn_out"])
            outs["diff_p"].append(zp3[:, :, 1024:3072].reshape(bp, seq, 2, N_HEADS, DA_V))
            outs["diff_s"].append(zs3[:, :ts, 1024:3072].reshape(bs, ts, 2, N_HEADS, DA_V))
            outs["pool_p"].append(zp3[:, seq - POOL_STATE:, 3072:])
            outs["pool_s"].append(jnp.concatenate([state_pool[e], zs3[:, :ts, 3072:]], axis=1)[:, -POOL_STATE:])
        else:
            o = layer // 2
            zp = _norm_matmul(xp, norm_mix[layer], w_in_o[o], tm_p, cfg["tn_odd"])
            zs = _norm_matmul(xs, norm_mix[layer], w_in_o[o], ms, cfg["tn_odd"])
            zp3 = zp.reshape(bp, seq, -1)
            zs3 = zs.reshape(bs, T8, -1)
            c_kc, c_ks, c_kw = COL_KC * LANES, COL_KS * LANES, COL_KW * LANES
            o_nsa = _nsa_prompt(zp, tile_bias, cmp_bias_t, expand, bp, seq)
            o_conv, tail = _short_conv(zp, jnp.zeros((bp, 8, MIX_WIDTH), F32), conv_w[o], bp, seq, cfg["tm_pool"])
            xp = _out_proj(xp, o_nsa, o_conv, w_out_o[o], tm_p, cfg["tn_out"])
            qg = zs3[:, :, 0:1024].reshape(bs, T8, N_HEADS, NSA_DH).transpose(0, 2, 1, 3).reshape(
                bs, N_HEADS * T8, NSA_DH)
            new_rows = lambda c0: _pad_rows(zs3[:, :, c0:c0 + kvw].reshape(bs, T8 * NSA_ROW, NSA_DH), PAGE_SIZE)
            kvc = _cmp_means(page_table, cache_cmp, o)
            o_cmp, sel = _cmp_select_decode(qg, kvc, zs3[:, :, c_kc:c_kc + kvw], cmp_tile, past_len, ts)
            o_slc = _slc_decode(page_table, qg, sel, expand_dec, new_rows(c_ks), slc_tile, nsa_new_tile, cache_slc, o)
            o_nsa_s = _win_combine_decode(qg, state_win_kv[o].reshape(bs, wbuf * NSA_ROW, NSA_DH), new_rows(c_kw),
                                          win_tile, zs, o_cmp, o_slc)
            prev = jnp.pad(state_conv[o], ((0, 0), (8 - (CONV_WIDTH - 1), 0), (0, 0)))
            o_conv_s, tail_s = _short_conv(zs, prev, conv_w[o], bs, T8, T8)
            xs = _out_proj(xs, o_nsa_s, o_conv_s, w_out_o[o], ms, cfg["tn_out"])
            kv5 = lambda a, n: a.reshape(a.shape[0], n, 2, NSA_KV, NSA_DH)
            outs["cmp_p"].append(kv5(zp3[:, :, c_kc:c_kc + kvw], seq))
            outs["slc_p"].append(kv5(zp3[:, :, c_ks:c_ks + kvw], seq))
            outs["win_p"].append(kv5(zp3[:, seq - min(WINDOW, seq):, c_kw:c_kw + kvw], min(WINDOW, seq)))
            outs["cmp_s"].append(kv5(zs3[:, :ts, c_kc:c_kc + kvw], ts))
            outs["slc_s"].append(kv5(zs3[:, :ts, c_ks:c_ks + kvw], ts))
            n_win = min(WINDOW, past_len + ts)
            outs["win_s"].append(jnp.concatenate([state_win_kv[o], kv5(zs3[:, :ts, c_kw:c_kw + kvw], ts)],
                                                 axis=1)[:, -n_win:])
            outs["conv_p"].append(tail[:, 8 - (CONV_WIDTH - 1):])
            outs["conv_s"].append(tail_s[:, ts - (CONV_WIDTH - 1):ts])
        last = layer == depth - 1
        xp = _mlp(xp, norm_mlp[layer], w1[layer], w2[layer], norm_final, min(cfg["tm_mlp"], mp), cfg["tf"], last)
        xs = _mlp(xs, norm_mlp[layer], w1[layer], w2[layer], norm_final, ms, cfg["tf"], last)

    st = {k: jnp.stack(v) for k, v in outs.items()}
    return (xp.reshape(bp, seq, d), xs.reshape(bs, T8, d)[:, :ts],
            st["diff_p"], st["diff_s"], st["pool_p"], st["pool_s"], st["cmp_p"], st["cmp_s"],
            st["slc_p"], st["slc_s"], st["win_p"], st["win_s"], st["conv_p"], st["conv_s"])
```

```python
import functools
import math

import jax
import jax.numpy as jnp
import numpy as np
from jax import lax
from jax.experimental import pallas as pl
from jax.experimental.pallas import tpu as pltpu

F32 = jnp.float32
BF16 = jnp.bfloat16

D_MODEL = 2048
N_HEADS = 8
MIX_WIDTH = D_MODEL // 2
DA_QK = 64
DA_V = 128
POOL_WINDOWS = (2, 4, 8, 16)
POOL_GROUP = MIX_WIDTH // 4
POOL_STATE = 15
NSA_DH = 128
NSA_KV = 2
NSA_HPG = 4
CMP_BLOCK = 64
SLC_BLOCK = 64
SLC_TOPK = 16
WINDOW = 512
CONV_WIDTH = 3
NUM_BUCKETS = 32
MAX_DISTANCE = 128
PAGE_SIZE = 128
EPS = 1e-6
NEG = -1e30
MASK_BIG = 2.0 ** 100
ODD_SIZES = (1024, 512, 512, 512, 24, 1024, 1024, 1024)

LANES = 128
VMEM_LIMIT = 56 * 1024 * 1024
TQ = 256
TK = 256


def _cparams(sem):
    return pltpu.CompilerParams(dimension_semantics=sem, vmem_limit_bytes=VMEM_LIMIT)


def _bucket_np(dist):
    n = np.maximum(dist, 0)
    max_exact = NUM_BUCKETS // 2
    nf = np.maximum(n, 1).astype(np.float32)
    large = max_exact + (np.log(nf / np.float32(max_exact)) / np.float32(math.log(MAX_DISTANCE / max_exact))
                         * np.float32(NUM_BUCKETS - max_exact)).astype(np.int32)
    large = np.minimum(large, NUM_BUCKETS - 1)
    return np.where(n < max_exact, n, large).astype(np.int32)


def _bucket_or_masked(dist, valid):
    return np.where(valid, _bucket_np(dist), -1).astype(np.int32)


LOG2E = math.log2(math.e)


def _bias_kernel(table_ref, bucket_ref, o_ref, *, unit):
    h = pl.program_id(0)
    bt = bucket_ref[...]
    far = table_ref[NUM_BUCKETS - 1, h]
    acc = jnp.zeros(bt.shape, F32)
    for b in range(NUM_BUCKETS - 1):
        acc = jnp.where(bt == b, (table_ref[b, h] - far) * unit, acc)
    o_ref[...] = jnp.where(bt < 0, NEG, acc)


def _bias_tiles(table, buckets, unit=1.0):
    shp = buckets.shape
    flat = jnp.asarray(buckets.reshape(-1, shp[-1]))
    rows = flat.shape[0]
    out = pl.pallas_call(
        functools.partial(_bias_kernel, unit=unit),
        grid=(N_HEADS,),
        in_specs=[pl.BlockSpec(memory_space=pltpu.SMEM),
                  pl.BlockSpec((rows, shp[-1]), lambda h: (0, 0))],
        out_specs=pl.BlockSpec((None, rows, shp[-1]), lambda h: (h, 0, 0)),
        out_shape=jax.ShapeDtypeStruct((N_HEADS, rows, shp[-1]), F32),
        compiler_params=_cparams(("arbitrary",)),
        name="bias_tiles",
    )(table, flat)
    return out.reshape((N_HEADS,) + shp)


def _rms_rows(x, g):
    y = x * lax.rsqrt(jnp.mean(x * x, axis=-1, keepdims=True) + EPS)
    return y * g


def _norm_matmul_kernel(x_ref, g_ref, w_ref, *rest, taps, n_alias):
    o_ref = rest[n_alias]
    tap_refs = rest[n_alias + 1:n_alias + 1 + len(taps)]
    h_scr = rest[-1]
    j = pl.program_id(1)
    tm, tn = o_ref.shape

    @pl.when(j == 0)
    def _():
        h_scr[...] = _rms_rows(x_ref[...], g_ref[...]).astype(BF16)

    acc = jnp.dot(h_scr[...], w_ref[...], preferred_element_type=F32)
    o_ref[...] = acc

    def whole_blocks(t_ref, col0, width):
        @pl.when((j >= col0 // tn) & (j < (col0 + width) // tn))
        def _():
            t_ref[...] = acc

    def token_rows(t_ref, col0, per_tok):
        loc = col0 % tn

        @pl.when(j == col0 // tn)
        def _():
            for c in range(per_tok):
                t_ref[pl.ds(c, tm, stride=per_tok), :] = acc[:, loc + c * LANES:loc + (c + 1) * LANES]

    for (col0, width, per_tok), t_ref in zip(taps, tap_refs):
        if per_tok == 1:
            whole_blocks(t_ref, col0, width)
        else:
            token_rows(t_ref, col0, per_tok)


def _norm_matmul(x, g, w, tm, tn, taps=(), stack=1, slot=0, prev=()):
    m, d = x.shape
    n = w.shape[1]
    assert m % tm == 0 and n % tn == 0 and (not prev or len(prev) == len(taps))
    out_specs = [pl.BlockSpec((tm, tn), lambda i, j: (i, j))]
    out_shape = [jax.ShapeDtypeStruct((m, n), F32)]
    for col0, width, per_tok in taps:
        if per_tok == 1:
            assert col0 % tn == 0 and width % tn == 0
            b0, nb = col0 // tn, width // tn
            out_specs.append(pl.BlockSpec((None, tm, tn), lambda i, j, b0=b0, nb=nb: (slot, i, jnp.clip(j - b0, 0, nb - 1))))
            out_shape.append(jax.ShapeDtypeStruct((stack, m, width), F32))
        else:
            assert width == per_tok * LANES and col0 // tn == (col0 + width - 1) // tn
            out_specs.append(pl.BlockSpec((None, tm * per_tok, LANES), lambda i, j: (slot, i, 0)))
            out_shape.append(jax.ShapeDtypeStruct((stack, m * per_tok, LANES), F32))
    n_alias = len(prev)
    outs = pl.pallas_call(
        functools.partial(_norm_matmul_kernel, taps=tuple(taps), n_alias=n_alias),
        grid=(m // tm, n // tn),
        in_specs=[pl.BlockSpec((tm, d), lambda i, j: (i, 0)),
                  pl.BlockSpec((1, d), lambda i, j: (0, 0)),
                  pl.BlockSpec((d, tn), lambda i, j: (0, j))]
        + [pl.BlockSpec(memory_space=pl.ANY)] * n_alias,
        out_specs=out_specs,
        out_shape=out_shape,
        input_output_aliases={3 + k: 1 + k for k in range(n_alias)},
        scratch_shapes=[pltpu.VMEM((tm, d), BF16)],
        compiler_params=_cparams(("parallel", "arbitrary")),
        name="norm_matmul",
    )(x, g.reshape(1, d), w, *prev)
    return outs[0], tuple(outs[1:])


def _out_proj_kernel(x_ref, a_ref, b_ref, wa_ref, wb_ref, o_ref):
    acc = jnp.dot(a_ref[...], wa_ref[...], preferred_element_type=F32)
    acc = acc + jnp.dot(b_ref[...], wb_ref[...], preferred_element_type=F32)
    o_ref[...] = x_ref[...] + acc


def _out_proj(x, a, b, w, tm, tn):
    m, d = x.shape
    ka = a.shape[1]
    assert m % tm == 0 and d % tn == 0
    return pl.pallas_call(
        _out_proj_kernel,
        grid=(m // tm, d // tn),
        in_specs=[pl.BlockSpec((tm, tn), lambda i, j: (i, j)),
                  pl.BlockSpec((tm, ka), lambda i, j: (i, 0)),
                  pl.BlockSpec((tm, ka), lambda i, j: (i, 0)),
                  pl.BlockSpec((ka, tn), lambda i, j: (0, j)),
                  pl.BlockSpec((ka, tn), lambda i, j: (1, j))],
        out_specs=pl.BlockSpec((tm, tn), lambda i, j: (i, j)),
        out_shape=jax.ShapeDtypeStruct((m, d), F32),
        compiler_params=_cparams(("parallel", "arbitrary")),
        name="out_proj",
    )(x, a, b, w, w)


def _mlp_kernel(x_ref, g_ref, w1_ref, w2_ref, gf_ref, o_ref, h_scr, *, final_norm):
    f = pl.program_id(1)

    @pl.when(f == 0)
    def _():
        x = x_ref[...]
        h_scr[...] = _rms_rows(x, g_ref[...]).astype(BF16)
        o_ref[...] = x

    a = jnp.maximum(jnp.dot(h_scr[...], w1_ref[...], preferred_element_type=F32), 0.0)
    a = (a * a).astype(BF16)
    o_ref[...] += jnp.dot(a, w2_ref[...], preferred_element_type=F32)

    if final_norm:
        @pl.when(f == pl.num_programs(1) - 1)
        def _():
            o_ref[...] = _rms_rows(o_ref[...], gf_ref[...])


def _mlp(x, g, w1, w2, gf, tm, tf, final_norm):
    m, d = x.shape
    ff = w1.shape[1]
    assert m % tm == 0 and ff % tf == 0
    return pl.pallas_call(
        functools.partial(_mlp_kernel, final_norm=final_norm),
        grid=(m // tm, ff // tf),
        in_specs=[pl.BlockSpec((tm, d), lambda i, f: (i, 0)),
                  pl.BlockSpec((1, d), lambda i, f: (0, 0)),
                  pl.BlockSpec((d, tf), lambda i, f: (0, f)),
                  pl.BlockSpec((tf, d), lambda i, f: (f, 0)),
                  pl.BlockSpec((1, d), lambda i, f: (0, 0))],
        out_specs=pl.BlockSpec((tm, d), lambda i, f: (i, 0)),
        out_shape=jax.ShapeDtypeStruct((m, d), F32),
        scratch_shapes=[pltpu.VMEM((tm, d), BF16)],
        compiler_params=_cparams(("parallel", "arbitrary")),
        name="mlp",
    )(x, g.reshape(1, d), w1, w2, gf.reshape(1, d))


def _flash_init(m_scr, l_scr, acc_scr):
    m_scr[...] = jnp.full(m_scr.shape, NEG, F32)
    l_scr[...] = jnp.zeros(l_scr.shape, F32)
    acc_scr[...] = jnp.zeros(acc_scr.shape, F32)


def _flash_tile(qs, kt, vt, m_scr, l_scr, acc_scr, *, heads, bias=None, mask_add=None):
    s = lax.dot_general(qs, kt, (((1,), (1,)), ((), ())), preferred_element_type=F32)
    rows, tk = s.shape
    if bias is not None or mask_add is not None:
        s3 = s.reshape(heads, rows // heads, tk)
        if bias is not None:
            s3 = s3 + bias
        if mask_add is not None:
            s3 = s3 + mask_add[None]
        s = s3.reshape(rows, tk)
    m_prev = m_scr[...]
    m_next = jnp.maximum(m_prev, jnp.max(s, axis=1, keepdims=True))
    p = jnp.exp2(s - jnp.tile(m_next, (1, tk // LANES)))
    alpha = jnp.exp2(m_prev - m_next)
    l_scr[...] = alpha * l_scr[...] + jnp.sum(p, axis=1, keepdims=True)
    acc_scr[...] = alpha * acc_scr[...] + jnp.dot(p.astype(BF16), vt, preferred_element_type=F32)
    m_scr[...] = m_next


def _kv_tile(k_ref, v_ref, j, tk):
    off = pl.multiple_of(j * tk, tk)
    return k_ref[pl.ds(off, tk), :].astype(BF16), v_ref[pl.ds(off, tk), :].astype(BF16)


def _diff_lambda_in_kernel(lv_ref, lam_init):
    lv = lv_ref[...]
    a = jnp.sum(lv[0:1] * lv[1:2], axis=1, keepdims=True)
    b = jnp.sum(lv[2:3] * lv[3:4], axis=1, keepdims=True)
    return jnp.exp(a) - jnp.exp(b) + lam_init


DIFF_HEADS_PER_STEP = 4


def _diff_attn_kernel(lv_ref, q_ref, k_ref, v_ref, bias_ref, g_ref, o_ref, m_scr, l_scr, acc_scr, *, lam_init):
    qi = pl.program_id(2)
    tq = q_ref.shape[0]
    tk = bias_ref.shape[-1]
    hp = q_ref.shape[1] // DA_V
    lane = lax.broadcasted_iota(jnp.int32, (tq, DA_V), 1)
    qs = []
    for h in range(hp):
        q = q_ref[:, h * DA_V:(h + 1) * DA_V] * (DA_QK ** -0.5 * LOG2E)
        qs.append(jnp.concatenate([jnp.where(lane < DA_QK, q, 0.0), jnp.where(lane >= DA_QK, q, 0.0)],
                                  axis=0).astype(BF16))
    _flash_init(m_scr, l_scr, acc_scr)

    def tiles(j, which):
        off = pl.multiple_of(j * tk, tk)
        for h in range(hp):
            cols = slice(h * DA_V, (h + 1) * DA_V)
            kt = k_ref[pl.ds(off, tk), cols].astype(BF16)
            vt = v_ref[pl.ds(off, tk), cols].astype(BF16)
            _flash_tile(qs[h], kt, vt, m_scr.at[h], l_scr.at[h], acc_scr.at[h], heads=2,
                        bias=None if which is None else bias_ref[h, which][None])

    def far(j, c):
        tiles(j, None)
        return c

    lax.fori_loop(0, jnp.maximum(qi - 1, 0), far, 0)

    @pl.when(qi == 0)
    def _():
        tiles(qi, 0)

    @pl.when(qi >= 1)
    def _():
        tiles(qi - 1, 1)
        tiles(qi, 0)

    lam = _diff_lambda_in_kernel(lv_ref, lam_init)
    for h in range(hp):
        o = acc_scr[h] / l_scr[h]
        a = o[:tq] - lam * o[tq:]
        o_ref[:, h * DA_V:(h + 1) * DA_V] = (_rms_rows(a, g_ref[...]) * (1.0 - lam_init)).astype(BF16)


def _diff_attn_prompt(z, lam_vec, subln_g, bias, lam_init, batch, seq):
    nq = seq // TQ
    hp = DIFF_HEADS_PER_STEP
    groups = N_HEADS // hp
    wide = hp * DA_V
    return pl.pallas_call(
        functools.partial(_diff_attn_kernel, lam_init=lam_init),
        grid=(batch, groups, nq),
        in_specs=[pl.BlockSpec((4, DA_QK), lambda b, h, i: (0, 0)),
                  pl.BlockSpec((TQ, wide), lambda b, h, i: (b * nq + i, h)),
                  pl.BlockSpec((seq, wide), lambda b, h, i: (b, groups + h)),
                  pl.BlockSpec((seq, wide), lambda b, h, i: (b, 2 * groups + h)),
                  pl.BlockSpec((hp, bias.shape[1], TQ, TK), lambda b, h, i: (h, 0, 0, 0)),
                  pl.BlockSpec((1, DA_V), lambda b, h, i: (0, 0))],
        out_specs=pl.BlockSpec((TQ, wide), lambda b, h, i: (b * nq + i, h)),
        out_shape=jax.ShapeDtypeStruct((batch * seq, N_HEADS * DA_V), BF16),
        scratch_shapes=[pltpu.VMEM((hp, 2 * TQ, LANES), F32)] * 3,
        compiler_params=_cparams(("parallel", "parallel", "arbitrary")),
        name="diff_attn_prompt",
    )(lam_vec, z, z, z, bias, subln_g.reshape(1, DA_V))


def _toeplitz_buckets(tq, tk):
    i = np.arange(tq)[:, None]
    j = np.arange(tk)[None, :]
    diag = _bucket_or_masked(i - j, i >= j)
    sub = _bucket_or_masked(tk + i - j, np.ones((tq, tk), bool))
    return np.stack([diag, sub])


HALO = 16


def _pool_kernel(u_ref, halo_ref, prev_ref, w_ref, scale_ref, o_ref, ext_scr, *, pos0):
    i = pl.program_id(1)
    tm = u_ref.shape[0]
    ext_scr[0:HALO, :] = jnp.where(i == 0, prev_ref[...], halo_ref[...])
    ext_scr[HALO:HALO + tm, :] = u_ref[...]
    pos = pos0 + i * tm + lax.broadcasted_iota(jnp.int32, (tm, 1), 0)
    outs = []
    for g, w in enumerate(POOL_WINDOWS):
        c0, c1 = g * POOL_GROUP, (g + 1) * POOL_GROUP
        x0 = ext_scr[HALO:HALO + tm, c0:c1]
        win = x0
        for k in range(1, w):
            win = win + ext_scr[HALO - k:HALO - k + tm, c0:c1]
        cnt = jnp.minimum(pos + 1, w).astype(F32)
        d = win / cnt - x0
        outs.append(jnp.dot(d.astype(BF16), w_ref[g], preferred_element_type=F32))
    o_ref[...] = (jnp.concatenate(outs, axis=-1) * scale_ref[...]).astype(BF16)


def _pool_mix(z, prev, w_pool, scale, pos0, batch, seq, tm):
    nb = seq // tm
    ucol = 3
    return pl.pallas_call(
        functools.partial(_pool_kernel, pos0=pos0),
        grid=(batch, nb),
        in_specs=[pl.BlockSpec((tm, MIX_WIDTH), lambda b, i: (b * nb + i, ucol)),
                  pl.BlockSpec((HALO, MIX_WIDTH),
                               lambda b, i: (jnp.maximum((b * nb + i) * (tm // HALO) - 1, 0), ucol)),
                  pl.BlockSpec((None, HALO, MIX_WIDTH), lambda b, i: (b, 0, 0)),
                  pl.BlockSpec((4, POOL_GROUP, POOL_GROUP), lambda b, i: (0, 0, 0)),
                  pl.BlockSpec((1, MIX_WIDTH), lambda b, i: (0, 0))],
        out_specs=pl.BlockSpec((tm, MIX_WIDTH), lambda b, i: (b * nb + i, 0)),
        out_shape=jax.ShapeDtypeStruct((batch * seq, MIX_WIDTH), BF16),
        scratch_shapes=[pltpu.VMEM((HALO + tm, MIX_WIDTH), F32)],
        compiler_params=_cparams(("parallel", "arbitrary")),
        name="pool_mix",
    )(z, z, prev, w_pool, scale.reshape(1, MIX_WIDTH))


COL_CB, COL_CC, COL_CH, COL_KC, COL_KS, COL_KW, COL_GATE = 8, 16, 24, 32, 36, 40, 44


def _odd_weight_layout(w_in):
    offs = np.cumsum((0,) + ODD_SIZES)
    q, kc, ks_, kw, gates, cb, cc, ch = [w_in[:, offs[i]:offs[i + 1]] for i in range(8)]
    pad = jnp.zeros((w_in.shape[0], LANES - ODD_SIZES[4]), w_in.dtype)
    return jnp.concatenate([q, cb, cc, ch, kc, ks_, kw, gates, pad], axis=1)


def _select_blocks(score_t, q0, nb):
    nbp, tq = score_t.shape
    blk = lax.broadcasted_iota(jnp.int32, (nbp, tq), 0)
    cur = (q0 + lax.broadcasted_iota(jnp.int32, (nbp, tq), 1)) // SLC_BLOCK
    forced = (blk == 0) | (blk == cur) | (blk == cur - 1)
    future = (blk > cur) | (blk >= nb)
    s = jnp.where(forced, jnp.inf, jnp.where(future, -jnp.inf, score_t))
    cnt = jnp.zeros((nbp, tq), jnp.int32)
    for n in range(nb):
        row = s[n:n + 1, :]
        beats = (row > s) | ((row == s) & (blk > n))
        cnt = cnt + beats.astype(jnp.int32)
    return (cnt < min(SLC_TOPK, nb)).astype(F32)


def _nsa_prompt_kernel(q_ref, kc_ref, vc_ref, ks_ref, vs_ref, kw_ref, vw_ref, gate_ref, tb_ref, cb_ref, e_ref,
                       o_ref, kcm_scr, vcm_scr, m_scr, l_scr, acc_scr):
    g = pl.program_id(1)
    qi = pl.program_id(2)
    tq = q_ref.shape[0]
    tk = tb_ref.shape[-1]
    seq = kc_ref.shape[0]
    nb = seq // CMP_BLOCK
    nbp = -(-nb // 8) * 8
    scale = NSA_DH ** -0.5
    hp = NSA_HPG

    @pl.when(qi == 0)
    def _():
        kcm_scr[...] = jnp.zeros(kcm_scr.shape, BF16)
        vcm_scr[...] = jnp.zeros(vcm_scr.shape, BF16)
        kcm_scr[0:nb, :] = (jnp.sum(kc_ref[...].reshape(nb, CMP_BLOCK, NSA_DH), axis=1) / CMP_BLOCK).astype(BF16)
        vcm_scr[0:nb, :] = (jnp.sum(vc_ref[...].reshape(nb, CMP_BLOCK, NSA_DH), axis=1) / CMP_BLOCK).astype(BF16)

    qh = [q_ref[:, h * NSA_DH:(h + 1) * NSA_DH].astype(BF16) for h in range(hp)]
    qs = jnp.concatenate([(q_ref[:, h * NSA_DH:(h + 1) * NSA_DH] * (scale * LOG2E)).astype(BF16)
                          for h in range(hp)], axis=0)

    kcm = kcm_scr[...]
    vcm = vcm_scr[...]
    o_cmp = []
    p_grp_t = jnp.zeros((LANES, tq), F32)
    for h in range(hp):
        lt = lax.dot_general(kcm, qh[h], (((1,), (1,)), ((), ())), preferred_element_type=F32) * scale
        bt = cb_ref[h]
        lt = lt + bt
        p = jnp.exp(lt - jnp.max(lt, axis=0, keepdims=True))
        p = p / jnp.sum(p, axis=0, keepdims=True)
        p = jnp.where(bt > 0.5 * NEG, p, 0.0)
        p_grp_t = p_grp_t + p
        o_cmp.append(jnp.dot(p.T.astype(BF16), vcm, preferred_element_type=F32))

    sel_t = _select_blocks(p_grp_t[0:nbp, :], qi * tq, nb)
    if nbp < LANES:
        sel_t = jnp.concatenate([sel_t, jnp.zeros((LANES - nbp, tq), F32)], axis=0)
    sel_neg = ((sel_t.T - 1.0) * MASK_BIG).astype(BF16)

    def sel_mask(j):
        return jnp.dot(sel_neg, e_ref[j], preferred_element_type=F32)

    def slc_tile(j, which):
        kt, vt = _kv_tile(ks_ref, vs_ref, j, tk)
        _flash_tile(qs, kt, vt, m_scr.at[0], l_scr.at[0], acc_scr.at[0], heads=hp,
                    bias=None if which is None else tb_ref[:, which], mask_add=sel_mask(j))

    def win_tile(j, which):
        kt, vt = _kv_tile(kw_ref, vw_ref, j, tk)
        _flash_tile(qs, kt, vt, m_scr.at[1], l_scr.at[1], acc_scr.at[1], heads=hp, bias=tb_ref[:, which])

    _flash_init(m_scr, l_scr, acc_scr)
    n_far = jnp.maximum(qi - 1, 0)

    def far_pair(jj, c):
        slc_tile(2 * jj, None)
        slc_tile(2 * jj + 1, None)
        return c

    lax.fori_loop(0, n_far // 2, far_pair, 0)

    @pl.when(n_far % 2 == 1)
    def _():
        slc_tile(n_far - 1, None)

    def near_tiles(before):
        if before >= 2:
            win_tile(qi - 2, 2)
        if before >= 1:
            slc_tile(qi - 1, 1)
            win_tile(qi - 1, 1)
        slc_tile(qi, 0)
        win_tile(qi, 0)

    @pl.when(qi == 0)
    def _():
        near_tiles(0)

    @pl.when(qi == 1)
    def _():
        near_tiles(1)

    @pl.when(qi >= 2)
    def _():
        near_tiles(2)

    o_slc = acc_scr[0] / l_scr[0]
    o_win = acc_scr[1] / l_scr[1]

    sig = jax.nn.sigmoid(gate_ref[...])
    lane = lax.broadcasted_iota(jnp.int32, sig.shape, 1)

    def gate(branch, h):
        col = branch * N_HEADS + g * hp + h
        return jnp.sum(jnp.where(lane == col, sig, 0.0), axis=1, keepdims=True)

    for h in range(hp):
        rows = slice(h * tq, (h + 1) * tq)
        o = gate(0, h) * o_cmp[h] + gate(1, h) * o_slc[rows] + gate(2, h) * o_win[rows]
        o_ref[:, h * NSA_DH:(h + 1) * NSA_DH] = o.astype(BF16)


def _nsa_buckets(seq, tq, tk):
    i = np.arange(tq)[:, None]
    j = np.arange(tk)[None, :]
    toe = _toeplitz_buckets(tq, tk)
    assert WINDOW == 2 * tk and tq == tk
    win2 = np.where(i < j, NUM_BUCKETS - 1, -1).astype(np.int32)
    tiles = np.concatenate([toe, win2[None]])
    nb = seq // CMP_BLOCK
    blk_end = (np.arange(LANES)[:, None] + 1) * CMP_BLOCK - 1
    dist = np.arange(seq)[None, :] - blk_end
    cmp_t = _bucket_or_masked(dist, (dist >= 0) & (np.arange(LANES)[:, None] < nb))
    nk = seq // tk
    key_blk = (np.arange(nk)[:, None, None] * tk + np.arange(tk)[None, None, :]) // SLC_BLOCK
    expand = (key_blk == np.arange(LANES)[None, :, None]).astype(np.float32)
    return tiles, cmp_t, expand


def _nsa_prompt(z, tile_bias, cmp_bias_t, expand, batch, seq):
    nq = seq // TQ
    nk = seq // TK
    kv = lambda col: pl.BlockSpec((seq, NSA_DH), lambda b, g, i: (b, col + g))
    return pl.pallas_call(
        _nsa_prompt_kernel,
        grid=(batch, NSA_KV, nq),
        in_specs=[pl.BlockSpec((TQ, NSA_HPG * NSA_DH), lambda b, g, i: (b * nq + i, g)),
                  kv(COL_KC), kv(COL_KC + 2), kv(COL_KS), kv(COL_KS + 2), kv(COL_KW), kv(COL_KW + 2),
                  pl.BlockSpec((TQ, LANES), lambda b, g, i: (b * nq + i, COL_GATE)),
                  pl.BlockSpec((NSA_HPG, 3, TQ, TK), lambda b, g, i: (g, 0, 0, 0)),
                  pl.BlockSpec((NSA_HPG, LANES, TQ), lambda b, g, i: (g, 0, i)),
                  pl.BlockSpec((nk, LANES, TK), lambda b, g, i: (0, 0, 0))],
        out_specs=pl.BlockSpec((TQ, NSA_HPG * NSA_DH), lambda b, g, i: (b * nq + i, g)),
        out_shape=jax.ShapeDtypeStruct((batch * seq, MIX_WIDTH), BF16),
        scratch_shapes=[pltpu.VMEM((LANES, NSA_DH), BF16), pltpu.VMEM((LANES, NSA_DH), BF16)]
        + [pltpu.VMEM((2, NSA_HPG * TQ, LANES), F32)] * 3,
        compiler_params=_cparams(("parallel", "parallel", "arbitrary")),
        name="nsa_prompt",
    )(z, z, z, z, z, z, z, z, tile_bias, cmp_bias_t, expand)


def _conv_kernel(cb_ref, cc_ref, ch_ref, hc_ref, hh_ref, prev_ref, w_ref, o_ref, tail_ref, ext_scr):
    i = pl.program_id(1)
    tm = cb_ref.shape[0]
    e = cc_ref[...] * ch_ref[...]
    ext_scr[0:8, :] = jnp.where(i == 0, prev_ref[...], hc_ref[...] * hh_ref[...])
    ext_scr[8:8 + tm, :] = e
    w = w_ref[...]
    y = w[0:1] * ext_scr[6:6 + tm, :]
    y = y + w[1:2] * ext_scr[7:7 + tm, :]
    y = y + w[2:3] * e
    o_ref[...] = (cb_ref[...] * y).astype(BF16)

    @pl.when(i == pl.num_programs(1) - 1)
    def _():
        tail_ref[...] = e[tm - 8:tm, :]


def _short_conv(z, prev, w_conv, batch, seq, tm):
    nb = seq // tm
    cw = MIX_WIDTH // LANES
    blk = lambda c: pl.BlockSpec((tm, MIX_WIDTH), lambda b, i: (b * nb + i, c // cw))
    halo = lambda c: pl.BlockSpec((8, MIX_WIDTH), lambda b, i: (jnp.maximum((b * nb + i) * (tm // 8) - 1, 0), c // cw))
    return pl.pallas_call(
        _conv_kernel,
        grid=(batch, nb),
        in_specs=[blk(COL_CB), blk(COL_CC), blk(COL_CH), halo(COL_CC), halo(COL_CH),
                  pl.BlockSpec((None, 8, MIX_WIDTH), lambda b, i: (b, 0, 0)),
                  pl.BlockSpec((8, MIX_WIDTH), lambda b, i: (0, 0))],
        out_specs=[pl.BlockSpec((tm, MIX_WIDTH), lambda b, i: (b * nb + i, 0)),
                   pl.BlockSpec((None, 8, MIX_WIDTH), lambda b, i: (b, 0, 0))],
        out_shape=[jax.ShapeDtypeStruct((batch * seq, MIX_WIDTH), BF16),
                   jax.ShapeDtypeStruct((batch, 8, MIX_WIDTH), F32)],
        scratch_shapes=[pltpu.VMEM((8 + tm, MIX_WIDTH), F32)],
        compiler_params=_cparams(("parallel", "arbitrary")),
        name="short_conv",
    )(z, z, z, z, z, prev, jnp.pad(w_conv, ((0, 8 - CONV_WIDTH), (0, 0))))


T8 = 8
NEW_ROWS = 16
DIFF_PAGES_PER_STEP = 8
NSA_PAGES_PER_STEP = 16
NSA_ROW = 2 * NSA_KV
V_SHIFT = NSA_KV
NB_DEC = 288


def _nt(a, b):
    return lax.dot_general(a, b, (((1,), (1,)), ((), ())), preferred_element_type=F32)


def _softmax_update(st, v_tiles, width, m_ref, l_ref, acc_ref):
    m_prev = m_ref[...]
    m_next = jnp.maximum(m_prev, jnp.max(st, axis=1, keepdims=True))
    p = jnp.exp(st - jnp.tile(m_next, (1, st.shape[1] // LANES)))
    alpha = jnp.exp(m_prev - m_next)
    l_ref[...] = alpha * l_ref[...] + jnp.sum(p, axis=1, keepdims=True)
    pv = jnp.zeros(acc_ref.shape, F32)
    for k, vt in enumerate(v_tiles):
        pv = pv + jnp.dot(p[:, k * width:(k + 1) * width].astype(BF16), vt, preferred_element_type=F32)
    acc_ref[...] = alpha * acc_ref[...] + pv
    m_ref[...] = m_next


def _diff_decode_kernel(pt_ref, q_ref, new_ref, mask_ref, newmask_ref, lv_ref, g_ref, *rest, lam_init):
    pages = rest[:DIFF_PAGES_PER_STEP]
    o_ref, m_scr, l_scr, acc_scr = rest[DIFF_PAGES_PER_STEP:]
    s = pl.program_id(1)
    last = pl.num_programs(1) - 1
    width = PAGE_SIZE * N_HEADS

    @pl.when(s == 0)
    def _():
        _flash_init(m_scr, l_scr, acc_scr)

    q = q_ref[...] * (DA_QK ** -0.5)
    lane = lax.broadcasted_iota(jnp.int32, q.shape, 1)
    qs = jnp.concatenate([jnp.where(lane < DA_QK, q, 0.0), jnp.where(lane >= DA_QK, q, 0.0)], axis=0).astype(BF16)
    upd = functools.partial(_softmax_update, m_ref=m_scr, l_ref=l_scr, acc_ref=acc_scr)

    is_last = s == last
    tiles, values = [], []
    for k, pg in enumerate(pages):
        st = _nt(qs, pg[:, 0].reshape(width, DA_V).astype(BF16))
        if k == DIFF_PAGES_PER_STEP - 1:
            st = st + jnp.where(is_last, mask_ref[1], mask_ref[0])
        else:
            st = st + mask_ref[0]
        tiles.append(st)
        values.append(pg[:, 1].reshape(width, DA_V).astype(BF16))
    upd(jnp.concatenate(tiles, axis=1), values, width)

    @pl.when(is_last)
    def _():
        nw = NEW_ROWS * N_HEADS
        st = _nt(qs, new_ref[:, 0].reshape(nw, DA_V).astype(BF16)) + newmask_ref[...]
        upd(st, [new_ref[:, 1].reshape(nw, DA_V).astype(BF16)], nw)
        o = acc_scr[...] / l_scr[...]
        half = o.shape[0] // 2
        a = o[:half] - _diff_lambda_in_kernel(lv_ref, lam_init) * o[half:]
        o_ref[...] = _rms_rows(a, g_ref[...]) * (1.0 - lam_init)


def _diff_decode(page_table, q_ht, kv_new, masks, new_mask, lam_vec, subln_g, cache, layer, lam_init):
    batch, n_pages = page_table.shape
    steps = n_pages // DIFF_PAGES_PER_STEP
    rows = q_ht.shape[1]

    def page_spec(k):
        return pl.BlockSpec((None, None, PAGE_SIZE, 2, N_HEADS, DA_V),
                            lambda b, s, pt: (layer, pt[b, s * DIFF_PAGES_PER_STEP + k], 0, 0, 0, 0))

    grid_spec = pltpu.PrefetchScalarGridSpec(
        num_scalar_prefetch=1,
        grid=(batch, steps),
        in_specs=[pl.BlockSpec((None, rows, DA_V), lambda b, s, pt: (b, 0, 0)),
                  pl.BlockSpec((None, NEW_ROWS, 2, N_HEADS, DA_V), lambda b, s, pt: (b, 0, 0, 0, 0)),
                  pl.BlockSpec(masks.shape, lambda b, s, pt: (0, 0, 0)),
                  pl.BlockSpec(new_mask.shape, lambda b, s, pt: (0, 0)),
                  pl.BlockSpec((4, DA_QK), lambda b, s, pt: (0, 0)),
                  pl.BlockSpec((1, DA_V), lambda b, s, pt: (0, 0))]
        + [page_spec(k) for k in range(DIFF_PAGES_PER_STEP)],
        out_specs=pl.BlockSpec((None, rows, DA_V), lambda b, s, pt: (b, 0, 0)),
        scratch_shapes=[pltpu.VMEM((2 * rows, LANES), F32)] * 3,
    )
    return pl.pallas_call(
        functools.partial(_diff_decode_kernel, lam_init=lam_init),
        grid_spec=grid_spec,
        out_shape=jax.ShapeDtypeStruct((batch, rows, DA_V), F32),
        compiler_params=_cparams(("parallel", "arbitrary")),
        name="diff_decode",
    )(page_table, q_ht, kv_new, masks, new_mask, lam_vec, subln_g.reshape(1, DA_V),
      *([cache] * DIFF_PAGES_PER_STEP))


def _nsa_page_specs(layer):
    def spec(k):
        return pl.BlockSpec((None, None, PAGE_SIZE * NSA_ROW, NSA_DH),
                            lambda b, s, pt: (layer, pt[b, s * NSA_PAGES_PER_STEP + k], 0, 0))
    return [spec(k) for k in range(NSA_PAGES_PER_STEP)]


def _cmp_means_kernel(pt_ref, *rest):
    pages = rest[:NSA_PAGES_PER_STEP]
    o_ref = rest[NSA_PAGES_PER_STEP]
    per_blk = CMP_BLOCK * NSA_ROW // 8
    low = lax.broadcasted_iota(jnp.int32, (8, NSA_DH), 0) < NSA_ROW
    out = []
    for pg in pages:
        x = pg[...].reshape(PAGE_SIZE * NSA_ROW // 8, 8, NSA_DH)
        t0 = jnp.sum(x[0:per_blk], axis=0)
        t1 = jnp.sum(x[per_blk:2 * per_blk], axis=0)
        t0 = t0 + pltpu.roll(t0, NSA_ROW, axis=0)
        t1 = t1 + pltpu.roll(t1, NSA_ROW, axis=0)
        out.append(jnp.where(low, t0, t1) / CMP_BLOCK)
    o_ref[...] = jnp.concatenate(out, axis=0)


def _cmp_means(page_table, cache, layer):
    batch, n_pages = page_table.shape
    steps = n_pages // NSA_PAGES_PER_STEP
    per_step = NSA_PAGES_PER_STEP * 8
    grid_spec = pltpu.PrefetchScalarGridSpec(
        num_scalar_prefetch=1,
        grid=(batch, steps),
        in_specs=_nsa_page_specs(layer),
        out_specs=pl.BlockSpec((None, per_step, NSA_DH), lambda b, s, pt: (b, s, 0)),
    )
    return pl.pallas_call(
        _cmp_means_kernel,
        grid_spec=grid_spec,
        out_shape=jax.ShapeDtypeStruct((batch, steps * per_step, NSA_DH), F32),
        compiler_params=_cparams(("parallel", "arbitrary")),
        name="cmp_means",
    )(page_table, *([cache] * NSA_PAGES_PER_STEP))


def _cmp_select_decode_kernel(q_ref, kvc_ref, new_ref, tile_ref, ocmp_ref, sel_ref, k_scr, *, past_len, n_new):
    n_past = kvc_ref.shape[0]
    width = NB_DEC * NSA_ROW
    scale = NSA_DH ** -0.5
    k_scr[...] = jnp.zeros(k_scr.shape, F32)
    k_scr[0:n_past, :] = kvc_ref[...]
    real = lax.broadcasted_iota(jnp.int32, new_ref.shape, 0) < n_new
    tot = jnp.sum(jnp.where(real, new_ref[...], 0.0), axis=0, keepdims=True) / CMP_BLOCK
    new4 = jnp.concatenate([tot[:, c * NSA_DH:(c + 1) * NSA_DH] for c in range(NSA_ROW)]
                           + [jnp.zeros((8 - NSA_ROW, NSA_DH), F32)], axis=0)
    k_scr[n_past:n_past + 8, :] = new4

    q = q_ref[...].astype(BF16)
    tile = tile_ref[...]
    lg = _nt(q, k_scr[0:width, :].astype(BF16)) * scale + tile
    p = jnp.exp(lg - jnp.max(lg, axis=1, keepdims=True))
    p = p / jnp.sum(p, axis=1, keepdims=True)
    p = jnp.where(tile > 0.5 * NEG, p, 0.0)
    ocmp_ref[...] = jnp.dot(p.astype(BF16), k_scr[V_SHIFT:V_SHIFT + width, :].astype(BF16),
                            preferred_element_type=F32)

    lane = lax.broadcasted_iota(jnp.int32, (T8, width), 1)
    blk = lane // NSA_ROW
    cur = (past_len + lax.broadcasted_iota(jnp.int32, (T8, width), 0)) // SLC_BLOCK
    rows_g = NSA_HPG * T8
    for g in range(NSA_KV):
        mine = (lane % NSA_ROW) == g
        score = sum(p[g * rows_g + h * T8:g * rows_g + (h + 1) * T8] for h in range(NSA_HPG))
        forced = mine & ((blk == 0) | (blk == cur) | (blk == cur - 1))
        dead = (blk > cur) | jnp.logical_not(mine)
        sc = jnp.where(forced, jnp.inf, jnp.where(dead, -jnp.inf, score))
        taken = jnp.zeros((T8, width), jnp.bool_)
        for _ in range(SLC_TOPK):
            sm = jnp.where(taken, -jnp.inf, sc)
            cand = (sm == jnp.max(sm, axis=1, keepdims=True)) & jnp.logical_not(taken)
            idx = jnp.min(jnp.where(cand, lane, width), axis=1, keepdims=True)
            taken = taken | (lane == idx)
        sel_ref[g * T8:(g + 1) * T8, :] = taken.astype(F32)


def _cmp_select_decode(qg, kvc, kv_new, tile, past_len, n_new):
    batch, rows, _ = qg.shape
    width = NB_DEC * NSA_ROW
    n_past = kvc.shape[1]
    return pl.pallas_call(
        functools.partial(_cmp_select_decode_kernel, past_len=past_len, n_new=n_new),
        grid=(batch,),
        in_specs=[pl.BlockSpec((None, rows, NSA_DH), lambda b: (b, 0, 0)),
                  pl.BlockSpec((None, n_past, NSA_DH), lambda b: (b, 0, 0)),
                  pl.BlockSpec((None, T8, NSA_ROW * NSA_DH), lambda b: (b, 0, 0)),
                  pl.BlockSpec((rows, width), lambda b: (0, 0))],
        out_specs=[pl.BlockSpec((None, rows, NSA_DH), lambda b: (b, 0, 0)),
                   pl.BlockSpec((None, NSA_KV * T8, width), lambda b: (b, 0, 0))],
        out_shape=[jax.ShapeDtypeStruct((batch, rows, NSA_DH), F32),
                   jax.ShapeDtypeStruct((batch, NSA_KV * T8, width), F32)],
        scratch_shapes=[pltpu.VMEM((width + 8, NSA_DH), F32)],
        compiler_params=_cparams(("parallel",)),
        name="cmp_select_decode",
    )(qg, kvc, kv_new, tile)


def _masked_scores(st, sel, expand):
    rows, n = st.shape
    add = (jnp.dot(sel.astype(BF16), expand, preferred_element_type=F32) - 1.0) * (-NEG)
    st4 = st.reshape(NSA_KV, NSA_HPG, T8, n) + add.reshape(NSA_KV, 1, T8, n)
    return st4.reshape(rows, n)


def _slc_decode_kernel(pt_ref, q_ref, sel_ref, selnew_ref, e_ref, new_ref, tile_ref, newtile_ref, *rest):
    pages = rest[:NSA_PAGES_PER_STEP]
    o_ref, m_scr, l_scr, acc_scr = rest[NSA_PAGES_PER_STEP:]
    s = pl.program_id(1)
    last = pl.num_programs(1) - 1
    scale = NSA_DH ** -0.5
    width = PAGE_SIZE * NSA_ROW

    @pl.when(s == 0)
    def _():
        _flash_init(m_scr, l_scr, acc_scr)

    q = q_ref[...].astype(BF16)
    upd = functools.partial(_softmax_update, m_ref=m_scr, l_ref=l_scr, acc_ref=acc_scr)
    is_last = (s == last).astype(F32)
    tiles, values = [], []
    for k, pg in enumerate(pages):
        x = pg[...]
        st = _nt(q, x.astype(BF16)) * scale
        if k == NSA_PAGES_PER_STEP - 1:
            st = st + is_last * tile_ref[...]
        tiles.append(st)
        values.append(pltpu.roll(x, width - V_SHIFT, axis=0).astype(BF16))
    upd(_masked_scores(jnp.concatenate(tiles, axis=1), sel_ref[...], e_ref[...]), values, width)

    @pl.when(s == last)
    def _():
        x = new_ref[...]
        n = x.shape[0]
        st = _nt(q, x.astype(BF16)) * scale + newtile_ref[...]
        upd(_masked_scores(st, selnew_ref[...], e_ref[:, 0:n]), [pltpu.roll(x, n - V_SHIFT, axis=0).astype(BF16)], n)
        o_ref[...] = acc_scr[...] / l_scr[...]


def _slc_decode(page_table, qg, sel, expand, kv_new, tile, new_tile, cache, layer):
    batch, n_pages = page_table.shape
    steps = n_pages // NSA_PAGES_PER_STEP
    rows = qg.shape[1]
    keys = NSA_PAGES_PER_STEP * PAGE_SIZE * NSA_ROW
    grid_spec = pltpu.PrefetchScalarGridSpec(
        num_scalar_prefetch=1,
        grid=(batch, steps),
        in_specs=[pl.BlockSpec((None, rows, NSA_DH), lambda b, s, pt: (b, 0, 0)),
                  pl.BlockSpec((None, NSA_KV * T8, LANES), lambda b, s, pt: (b, 0, s)),
                  pl.BlockSpec((None, NSA_KV * T8, LANES), lambda b, s, pt: (b, 0, steps)),
                  pl.BlockSpec((LANES, keys), lambda b, s, pt: (0, 0)),
                  pl.BlockSpec((None,) + kv_new.shape[1:], lambda b, s, pt: (b, 0, 0)),
                  pl.BlockSpec(tile.shape, lambda b, s, pt: (0, 0)),
                  pl.BlockSpec(new_tile.shape, lambda b, s, pt: (0, 0))]
        + _nsa_page_specs(layer),
        out_specs=pl.BlockSpec((None, rows, NSA_DH), lambda b, s, pt: (b, 0, 0)),
        scratch_shapes=[pltpu.VMEM((rows, LANES), F32)] * 3,
    )
    return pl.pallas_call(
        _slc_decode_kernel,
        grid_spec=grid_spec,
        out_shape=jax.ShapeDtypeStruct((batch, rows, NSA_DH), F32),
        compiler_params=_cparams(("parallel", "arbitrary")),
        name="slc_decode",
    )(page_table, qg, sel, sel, expand, kv_new, tile, new_tile, *([cache] * NSA_PAGES_PER_STEP))


def _win_combine_decode_kernel(q_ref, state_ref, new_ref, tile_ref, gate_ref, ocmp_ref, oslc_ref, o_ref):
    scale = NSA_DH ** -0.5
    q = q_ref[...].astype(BF16)
    xs, xn = state_ref[...], new_ref[...]
    ns, nn = xs.shape[0], xn.shape[0]
    st = jnp.concatenate([_nt(q, xs.astype(BF16)), _nt(q, xn.astype(BF16))], axis=1) * scale + tile_ref[...]
    p = jnp.exp(st - jnp.max(st, axis=1, keepdims=True))
    p = (p / jnp.sum(p, axis=1, keepdims=True)).astype(BF16)
    o_win = (jnp.dot(p[:, 0:ns], pltpu.roll(xs, ns - V_SHIFT, axis=0).astype(BF16), preferred_element_type=F32)
             + jnp.dot(p[:, ns:], pltpu.roll(xn, nn - V_SHIFT, axis=0).astype(BF16), preferred_element_type=F32))
    sig = jax.nn.sigmoid(gate_ref[...])
    for g in range(NSA_KV):
        for h in range(NSA_HPG):
            head = g * NSA_HPG + h
            rows = slice(head * T8, (head + 1) * T8)
            gate = lambda branch: sig[:, branch * N_HEADS + head:branch * N_HEADS + head + 1]
            o = gate(0) * ocmp_ref[rows, :] + gate(1) * oslc_ref[rows, :] + gate(2) * o_win[rows]
            o_ref[:, head * NSA_DH:(head + 1) * NSA_DH] = o.astype(BF16)


def _win_combine_decode(qg, state_win, kv_new, tile, gates, o_cmp, o_slc):
    batch, rows, _ = qg.shape
    branch = pl.BlockSpec((None, rows, NSA_DH), lambda b: (b, 0, 0))
    return pl.pallas_call(
        _win_combine_decode_kernel,
        grid=(batch,),
        in_specs=[branch,
                  pl.BlockSpec((None,) + state_win.shape[1:], lambda b: (b, 0, 0)),
                  pl.BlockSpec((None,) + kv_new.shape[1:], lambda b: (b, 0, 0)),
                  pl.BlockSpec(tile.shape, lambda b: (0, 0)),
                  pl.BlockSpec((T8, LANES), lambda b: (b, COL_GATE)),
                  branch, branch],
        out_specs=pl.BlockSpec((T8, MIX_WIDTH), lambda b: (b, 0)),
        out_shape=jax.ShapeDtypeStruct((batch * T8, MIX_WIDTH), BF16),
        compiler_params=_cparams(("parallel",)),
        name="win_combine_decode",
    )(qg, state_win, kv_new, tile, gates, o_cmp, o_slc)


def _decode_buckets(past_len, wbuf):
    t = np.arange(T8)[:, None]
    dpos = np.repeat(np.arange(PAGE_SIZE), N_HEADS)[None, :]
    dnew = np.repeat(np.arange(NEW_ROWS), N_HEADS)[None, :]
    npos = np.repeat(np.arange(PAGE_SIZE), NSA_ROW)[None, :]
    nnew = np.repeat(np.arange(PAGE_SIZE // NSA_ROW), NSA_ROW)[None, :]
    nblk = np.repeat(np.arange(NB_DEC), NSA_ROW)[None, :]
    nwin = np.repeat(np.arange(wbuf), NSA_ROW)[None, :]
    d_cmp = past_len + t - ((nblk + 1) * CMP_BLOCK - 1)
    d_win = t + wbuf - nwin
    parts = [
        _bucket_or_masked(PAGE_SIZE + t - dpos, np.ones((T8, dpos.shape[1]), bool)),
        _bucket_or_masked(t - dnew, dnew <= t),
        _bucket_or_masked(PAGE_SIZE + t - npos, np.ones((T8, npos.shape[1]), bool)),
        _bucket_or_masked(t - nnew, nnew <= t),
        _bucket_or_masked(d_cmp, d_cmp >= 0),
        _bucket_or_masked(d_win, (d_win >= 0) & (d_win < WINDOW) & (past_len - wbuf + nwin >= 0)),
    ]
    return np.concatenate(parts, axis=1), np.cumsum([0] + [p.shape[1] for p in parts])


def _pad_rows(x, rows):
    return jnp.pad(x, ((0, 0), (0, rows - x.shape[1])) + ((0, 0),) * (x.ndim - 2))


def _tile_cfg():
    return dict(tm=1024, tn_even=512, tm_odd=512, tn_odd=1152, tn_out=1024, tm_mlp=512, tf=1024, tm_pool=256)


def kernel(x_prompt, x_sample, cache_diff_kv, cache_cmp_kv, cache_slc_kv, state_win_kv, state_pool, state_conv,
           page_table, rel_bias, norm_mix, norm_mlp, norm_final, even_w_in, even_w_out, diff_lambda, diff_subln,
           pool_w, pool_scale, odd_w_in, odd_w_out, conv_w, mlp_w1, mlp_w2):
    bp, seq, d = x_prompt.shape
    bs, ts, _ = x_sample.shape
    depth = norm_mix.shape[0]
    n_pages = page_table.shape[1]
    past_len = n_pages * PAGE_SIZE
    n_phys = cache_diff_kv.shape[1]
    wbuf = state_win_kv.shape[2]
    n_past_blk = past_len // SLC_BLOCK
    assert seq % TQ == 0 and ts <= T8 and n_pages % NSA_PAGES_PER_STEP == 0 and wbuf == WINDOW
    assert (past_len + ts - 1) // SLC_BLOCK == n_past_blk < NB_DEC
    cfg = _tile_cfg()

    w_in_e = even_w_in.astype(BF16)
    w_out_e = even_w_out.astype(BF16)
    w_in_o = jnp.stack([_odd_weight_layout(odd_w_in[o].astype(BF16)) for o in range(odd_w_in.shape[0])])
    w_out_o = odd_w_out.astype(BF16)
    w1 = mlp_w1.astype(BF16)
    w2 = mlp_w2.astype(BF16)
    pool_wb = pool_w.astype(BF16)

    tiles, cmp_t, expand = _nsa_buckets(seq, TQ, TK)
    tile_bias = _bias_tiles(rel_bias, tiles, LOG2E)
    cmp_bias_t = _bias_tiles(rel_bias, cmp_t)
    expand = jnp.asarray(expand, BF16)
    dec_buckets, off = _decode_buckets(past_len, wbuf)
    dec = _bias_tiles(rel_bias, dec_buckets)
    part = lambda i: dec[:, :, off[i]:off[i + 1]]

    def diff_rows(x):
        n = x.shape[-1]
        own = (np.arange(n) % N_HEADS)[None, None, :] == np.arange(N_HEADS)[:, None, None]
        x = jnp.where(own, x, NEG)
        return jnp.broadcast_to(x[None], (2,) + x.shape).reshape(2 * N_HEADS * T8, n)

    def nsa_rows(x):
        n = x.shape[-1]
        own = (np.arange(n) % NSA_ROW)[None, None, :] == (np.arange(N_HEADS) // NSA_HPG)[:, None, None]
        return jnp.where(own, x, NEG).reshape(N_HEADS * T8, n)

    diff_masks = jnp.stack([diff_rows(jnp.zeros_like(part(0))), diff_rows(part(0))])
    diff_new_mask = diff_rows(part(1))
    slc_tile = nsa_rows(part(2))
    nsa_new_tile = nsa_rows(part(3))
    cmp_tile = nsa_rows(part(4))
    win_tile = jnp.concatenate([nsa_rows(part(5)), nsa_new_tile], axis=1)
    key = np.arange(NSA_PAGES_PER_STEP * PAGE_SIZE * NSA_ROW)
    sel_lane = (key // (NSA_ROW * SLC_BLOCK)) * NSA_ROW + key % NSA_ROW
    is_key_row = (key % NSA_ROW) < NSA_KV
    expand_dec = jnp.asarray((np.arange(LANES)[:, None] == sel_lane[None, :]) & is_key_row[None, :], BF16)

    cache_cmp = cache_cmp_kv.reshape(cache_cmp_kv.shape[0], n_phys, PAGE_SIZE * NSA_ROW, NSA_DH)
    cache_slc = cache_slc_kv.reshape(cache_slc_kv.shape[0], n_phys, PAGE_SIZE * NSA_ROW, NSA_DH)

    mp, ms = bp * seq, bs * T8
    xp = x_prompt.reshape(mp, d)
    xs = _pad_rows(x_sample, T8).reshape(ms, d)
    tm_p = min(cfg["tm"], mp)
    outs = {k: [] for k in ("diff_s", "pool_p", "pool_s", "cmp_s", "slc_s", "win_s", "conv_p", "conv_s")}
    kvw = NSA_ROW * NSA_DH
    n_even, n_odd = (depth + 1) // 2, depth // 2
    even_taps = ((N_HEADS * DA_V, 2 * N_HEADS * DA_V, 1),)
    odd_taps = tuple((c * LANES, kvw, NSA_ROW) for c in (COL_KC, COL_KS, COL_KW))
    diff_kv_p, nsa_kv_p = (), ()

    for layer in range(depth):
        if layer % 2 == 0:
            e = layer // 2
            lam_init = 0.8 - 0.6 * math.exp(-0.3 * layer)
            zp, diff_kv_p = _norm_matmul(xp, norm_mix[layer], w_in_e[e], tm_p, cfg["tn_even"], even_taps, n_even, e,
                                         diff_kv_p)
            zs, _ = _norm_matmul(xs, norm_mix[layer], w_in_e[e], ms, cfg["tn_even"])
            zp3 = zp.reshape(bp, seq, -1)
            zs3 = zs.reshape(bs, T8, -1)
            o_attn = _diff_attn_prompt(zp, diff_lambda[e], diff_subln[e], tile_bias, lam_init, bp, seq)
            o_pool = _pool_mix(zp, jnp.zeros((bp, HALO, MIX_WIDTH), F32), pool_wb[e], pool_scale[e], 0, bp, seq,
                               cfg["tm_pool"])
            xp = _out_proj(xp, o_attn, o_pool, w_out_e[e], tm_p, cfg["tn_out"])
            q_ht = zs3[:, :, 0:1024].reshape(bs, T8, N_HEADS, DA_V).transpose(0, 2, 1, 3).reshape(
                bs, N_HEADS * T8, DA_V)
            kv_new = _pad_rows(zs3[:, :, 1024:3072].reshape(bs, T8, 2, N_HEADS, DA_V), NEW_ROWS)
            o_dec = _diff_decode(page_table, q_ht, kv_new, diff_masks, diff_new_mask, diff_lambda[e], diff_subln[e],
                                 cache_diff_kv, e, lam_init)
            o_attn_s = o_dec.reshape(bs, N_HEADS, T8, DA_V).transpose(0, 2, 1, 3).reshape(ms, N_HEADS * DA_V)
            prev = jnp.pad(state_pool[e], ((0, 0), (HALO - POOL_STATE, 0), (0, 0)))
            o_pool_s = _pool_mix(zs, prev, pool_wb[e], pool_scale[e], past_len, bs, T8, T8)
            xs = _out_proj(xs, o_attn_s.astype(BF16), o_pool_s, w_out_e[e], ms, cfg["tn_out"])
            outs["diff_s"].append(zs3[:, :ts, 1024:3072].reshape(bs, ts, 2, N_HEADS, DA_V))
            outs["pool_p"].append(zp3[:, seq - POOL_STATE:, 3072:])
            outs["pool_s"].append(jnp.concatenate([state_pool[e], zs3[:, :ts, 3072:]], axis=1)[:, -POOL_STATE:])
        else:
            o = layer // 2
            zp, nsa_kv_p = _norm_matmul(xp, norm_mix[layer], w_in_o[o], min(cfg["tm_odd"], mp), cfg["tn_odd"],
                                        odd_taps, n_odd, o, nsa_kv_p)
            zs, _ = _norm_matmul(xs, norm_mix[layer], w_in_o[o], ms, cfg["tn_odd"])
            zp3 = zp.reshape(bp, seq, -1)
            zs3 = zs.reshape(bs, T8, -1)
            c_kc, c_ks, c_kw = COL_KC * LANES, COL_KS * LANES, COL_KW * LANES
            o_nsa = _nsa_prompt(zp, tile_bias, cmp_bias_t, expand, bp, seq)
            o_conv, tail = _short_conv(zp, jnp.zeros((bp, 8, MIX_WIDTH), F32), conv_w[o], bp, seq, cfg["tm_pool"])
            xp = _out_proj(xp, o_nsa, o_conv, w_out_o[o], tm_p, cfg["tn_out"])
            qg = zs3[:, :, 0:1024].reshape(bs, T8, N_HEADS, NSA_DH).transpose(0, 2, 1, 3).reshape(
                bs, N_HEADS * T8, NSA_DH)
            new_rows = lambda c0: _pad_rows(zs3[:, :, c0:c0 + kvw].reshape(bs, T8 * NSA_ROW, NSA_DH), PAGE_SIZE)
            kvc = _cmp_means(page_table, cache_cmp, o)
            o_cmp, sel = _cmp_select_decode(qg, kvc, zs3[:, :, c_kc:c_kc + kvw], cmp_tile, past_len, ts)
            o_slc = _slc_decode(page_table, qg, sel, expand_dec, new_rows(c_ks), slc_tile, nsa_new_tile, cache_slc, o)
            o_nsa_s = _win_combine_decode(qg, state_win_kv[o].reshape(bs, wbuf * NSA_ROW, NSA_DH), new_rows(c_kw),
                                          win_tile, zs, o_cmp, o_slc)
            prev = jnp.pad(state_conv[o], ((0, 0), (8 - (CONV_WIDTH - 1), 0), (0, 0)))
            o_conv_s, tail_s = _short_conv(zs, prev, conv_w[o], bs, T8, T8)
            xs = _out_proj(xs, o_nsa_s, o_conv_s, w_out_o[o], ms, cfg["tn_out"])
            kv5 = lambda a, n: a.reshape(a.shape[0], n, 2, NSA_KV, NSA_DH)
            outs["cmp_s"].append(kv5(zs3[:, :ts, c_kc:c_kc + kvw], ts))
            outs["slc_s"].append(kv5(zs3[:, :ts, c_ks:c_ks + kvw], ts))
            n_win = min(WINDOW, past_len + ts)
            outs["win_s"].append(jnp.concatenate([state_win_kv[o], kv5(zs3[:, :ts, c_kw:c_kw + kvw], ts)],
                                                 axis=1)[:, -n_win:])
            outs["conv_p"].append(tail[:, 8 - (CONV_WIDTH - 1):])
            outs["conv_s"].append(tail_s[:, ts - (CONV_WIDTH - 1):ts])
        last = layer == depth - 1
        xp = _mlp(xp, norm_mlp[layer], w1[layer], w2[layer], norm_final, min(cfg["tm_mlp"], mp), cfg["tf"], last)
        xs = _mlp(xs, norm_mlp[layer], w1[layer], w2[layer], norm_final, ms, cfg["tf"], last)

    st = {k: jnp.stack(v) for k, v in outs.items()}
    st["diff_p"] = diff_kv_p[0].reshape(n_even, bp, seq, 2, N_HEADS, DA_V)
    nsa6 = lambda a: a.reshape(n_odd, bp, seq, 2, NSA_KV, NSA_DH)
    st["cmp_p"], st["slc_p"] = nsa6(nsa_kv_p[0]), nsa6(nsa_kv_p[1])
    st["win_p"] = nsa6(nsa_kv_p[2])[:, :, seq - min(WINDOW, seq):]
    return (xp.reshape(bp, seq, d), xs.reshape(bs, T8, d)[:, :ts],
            st["diff_p"], st["diff_s"], st["pool_p"], st["pool_s"], st["cmp_p"], st["cmp_s"],
            st["slc_p"], st["slc_s"], st["win_p"], st["win_s"], st["conv_p"], st["conv_s"])
```

```python
import functools
import math

import jax
import jax.numpy as jnp
import numpy as np
from jax import lax
from jax.experimental import pallas as pl
from jax.experimental.pallas import tpu as pltpu

F32 = jnp.float32
BF16 = jnp.bfloat16

D_MODEL = 2048
N_HEADS = 8
MIX_WIDTH = D_MODEL // 2
DA_QK = 64
DA_V = 128
POOL_WINDOWS = (2, 4, 8, 16)
POOL_GROUP = MIX_WIDTH // 4
POOL_STATE = 15
NSA_DH = 128
NSA_KV = 2
NSA_HPG = 4
CMP_BLOCK = 64
SLC_BLOCK = 64
SLC_TOPK = 16
WINDOW = 512
CONV_WIDTH = 3
NUM_BUCKETS = 32
MAX_DISTANCE = 128
PAGE_SIZE = 128
EPS = 1e-6
NEG = -1e30
MASK_BIG = 2.0 ** 100
ODD_SIZES = (1024, 512, 512, 512, 24, 1024, 1024, 1024)

LANES = 128
VMEM_LIMIT = 56 * 1024 * 1024
TQ = 256
TK = 256


def _cparams(sem):
    return pltpu.CompilerParams(dimension_semantics=sem, vmem_limit_bytes=VMEM_LIMIT)


def _bucket_np(dist):
    n = np.maximum(dist, 0)
    max_exact = NUM_BUCKETS // 2
    nf = np.maximum(n, 1).astype(np.float32)
    large = max_exact + (np.log(nf / np.float32(max_exact)) / np.float32(math.log(MAX_DISTANCE / max_exact))
                         * np.float32(NUM_BUCKETS - max_exact)).astype(np.int32)
    large = np.minimum(large, NUM_BUCKETS - 1)
    return np.where(n < max_exact, n, large).astype(np.int32)


def _bucket_or_masked(dist, valid):
    return np.where(valid, _bucket_np(dist), -1).astype(np.int32)


LOG2E = math.log2(math.e)


def _bias_kernel(table_ref, bucket_ref, o_ref, *, unit):
    h = pl.program_id(0)
    bt = bucket_ref[...]
    far = table_ref[NUM_BUCKETS - 1, h]
    acc = jnp.zeros(bt.shape, F32)
    for b in range(NUM_BUCKETS - 1):
        acc = jnp.where(bt == b, (table_ref[b, h] - far) * unit, acc)
    o_ref[...] = jnp.where(bt < 0, NEG, acc)


def _bias_tiles(table, buckets, unit=1.0):
    shp = buckets.shape
    flat = jnp.asarray(buckets.reshape(-1, shp[-1]))
    rows = flat.shape[0]
    out = pl.pallas_call(
        functools.partial(_bias_kernel, unit=unit),
        grid=(N_HEADS,),
        in_specs=[pl.BlockSpec(memory_space=pltpu.SMEM),
                  pl.BlockSpec((rows, shp[-1]), lambda h: (0, 0))],
        out_specs=pl.BlockSpec((None, rows, shp[-1]), lambda h: (h, 0, 0)),
        out_shape=jax.ShapeDtypeStruct((N_HEADS, rows, shp[-1]), F32),
        compiler_params=_cparams(("arbitrary",)),
        name="bias_tiles",
    )(table, flat)
    return out.reshape((N_HEADS,) + shp)


def _rms_rows(x, g):
    y = x * lax.rsqrt(jnp.mean(x * x, axis=-1, keepdims=True) + EPS)
    return y * g


def _norm_matmul_kernel(x_ref, g_ref, w_ref, *rest, taps, n_alias, emit):
    o_ref = rest[n_alias]
    tap_refs = rest[n_alias + 1:n_alias + 1 + len(taps)]
    h_scr = rest[-1]
    j = pl.program_id(1)
    tm, tn = o_ref.shape

    @pl.when(j == 0)
    def _():
        h_scr[...] = _rms_rows(x_ref[...], g_ref[...]).astype(BF16)

    wb = w_ref[...].astype(BF16)
    if emit:
        rest[n_alias + 1 + len(taps)][...] = wb
    acc = jnp.dot(h_scr[...], wb, preferred_element_type=F32)
    o_ref[...] = acc

    def whole_blocks(t_ref, col0, width):
        @pl.when((j >= col0 // tn) & (j < (col0 + width) // tn))
        def _():
            t_ref[...] = acc

    def token_rows(t_ref, col0, per_tok):
        loc = col0 % tn

        @pl.when(j == col0 // tn)
        def _():
            for c in range(per_tok):
                t_ref[pl.ds(c, tm, stride=per_tok), :] = acc[:, loc + c * LANES:loc + (c + 1) * LANES]

    for (col0, width, per_tok), t_ref in zip(taps, tap_refs):
        if per_tok == 1:
            whole_blocks(t_ref, col0, width)
        else:
            token_rows(t_ref, col0, per_tok)


def _norm_matmul(x, g, w, tm, tn, taps=(), stack=1, slot=0, prev=(), w_layer=None):
    m, d = x.shape
    n = w.shape[-1]
    emit = w_layer is not None
    assert m % tm == 0 and n % tn == 0 and (not prev or len(prev) == len(taps)) and (not emit or m == tm)
    out_specs = [pl.BlockSpec((tm, tn), lambda i, j: (i, j))]
    out_shape = [jax.ShapeDtypeStruct((m, n), F32)]
    for col0, width, per_tok in taps:
        if per_tok == 1:
            assert col0 % tn == 0 and width % tn == 0
            b0, nb = col0 // tn, width // tn
            out_specs.append(pl.BlockSpec((None, tm, tn), lambda i, j, b0=b0, nb=nb: (slot, i, jnp.clip(j - b0, 0, nb - 1))))
            out_shape.append(jax.ShapeDtypeStruct((stack, m, width), F32))
        else:
            assert width == per_tok * LANES and col0 // tn == (col0 + width - 1) // tn
            out_specs.append(pl.BlockSpec((None, tm * per_tok, LANES), lambda i, j: (slot, i, 0)))
            out_shape.append(jax.ShapeDtypeStruct((stack, m * per_tok, LANES), F32))
    n_alias = len(prev)
    if emit:
        w_spec = pl.BlockSpec((None, d, tn), lambda i, j: (w_layer, 0, j))
        out_specs.append(pl.BlockSpec((d, tn), lambda i, j: (0, j)))
        out_shape.append(jax.ShapeDtypeStruct((d, n), BF16))
    else:
        w_spec = pl.BlockSpec((d, tn), lambda i, j: (0, j))
    outs = pl.pallas_call(
        functools.partial(_norm_matmul_kernel, taps=tuple(taps), n_alias=n_alias, emit=emit),
        grid=(m // tm, n // tn),
        in_specs=[pl.BlockSpec((tm, d), lambda i, j: (i, 0)),
                  pl.BlockSpec((1, d), lambda i, j: (0, 0)),
                  w_spec]
        + [pl.BlockSpec(memory_space=pl.ANY)] * n_alias,
        out_specs=out_specs,
        out_shape=out_shape,
        input_output_aliases={3 + k: 1 + k for k in range(n_alias)},
        scratch_shapes=[pltpu.VMEM((tm, d), BF16)],
        compiler_params=_cparams(("parallel", "arbitrary")),
        name="norm_matmul",
    )(x, g.reshape(1, d), w, *prev)
    return outs[0], tuple(outs[1:])


def _out_proj_kernel(x_ref, a_ref, b_ref, wa_ref, wb_ref, o_ref, *rest, emit):
    wa = wa_ref[...].astype(BF16)
    wb = wb_ref[...].astype(BF16)
    if emit:
        ka = wa.shape[0]
        rest[0][0:ka, :] = wa
        rest[0][ka:2 * ka, :] = wb
    acc = jnp.dot(a_ref[...], wa, preferred_element_type=F32)
    acc = acc + jnp.dot(b_ref[...], wb, preferred_element_type=F32)
    o_ref[...] = x_ref[...] + acc


def _out_proj(x, a, b, w, tm, tn, w_layer=None):
    m, d = x.shape
    ka = a.shape[1]
    emit = w_layer is not None
    assert m % tm == 0 and d % tn == 0 and (not emit or m == tm)
    if emit:
        w_specs = [pl.BlockSpec((None, ka, tn), lambda i, j: (w_layer, 0, j)),
                   pl.BlockSpec((None, ka, tn), lambda i, j: (w_layer, 1, j))]
    else:
        w_specs = [pl.BlockSpec((ka, tn), lambda i, j: (0, j)), pl.BlockSpec((ka, tn), lambda i, j: (1, j))]
    out_specs = [pl.BlockSpec((tm, tn), lambda i, j: (i, j))]
    out_shape = [jax.ShapeDtypeStruct((m, d), F32)]
    if emit:
        out_specs.append(pl.BlockSpec((2 * ka, tn), lambda i, j: (0, j)))
        out_shape.append(jax.ShapeDtypeStruct((2 * ka, d), BF16))
    outs = pl.pallas_call(
        functools.partial(_out_proj_kernel, emit=emit),
        grid=(m // tm, d // tn),
        in_specs=[pl.BlockSpec((tm, tn), lambda i, j: (i, j)),
                  pl.BlockSpec((tm, ka), lambda i, j: (i, 0)),
                  pl.BlockSpec((tm, ka), lambda i, j: (i, 0))] + w_specs,
        out_specs=out_specs,
        out_shape=out_shape,
        compiler_params=_cparams(("parallel", "arbitrary")),
        name="out_proj",
    )(x, a, b, w, w)
    return outs if emit else outs[0]


def _mlp_kernel(x_ref, g_ref, w1_ref, w2_ref, gf_ref, o_ref, *rest, final_norm, emit):
    h_scr = rest[-1]
    f = pl.program_id(1)

    @pl.when(f == 0)
    def _():
        x = x_ref[...]
        h_scr[...] = _rms_rows(x, g_ref[...]).astype(BF16)
        o_ref[...] = x

    w1b = w1_ref[...].astype(BF16)
    w2b = w2_ref[...].astype(BF16)
    if emit:
        rest[0][...] = w1b
        rest[1][...] = w2b
    a = jnp.maximum(jnp.dot(h_scr[...], w1b, preferred_element_type=F32), 0.0)
    a = (a * a).astype(BF16)
    o_ref[...] += jnp.dot(a, w2b, preferred_element_type=F32)

    if final_norm:
        @pl.when(f == pl.num_programs(1) - 1)
        def _():
            o_ref[...] = _rms_rows(o_ref[...], gf_ref[...])


def _mlp(x, g, w1, w2, gf, tm, tf, final_norm, layer=None):
    m, d = x.shape
    ff = w1.shape[-1]
    emit = layer is not None
    assert m % tm == 0 and ff % tf == 0 and (not emit or m == tm)
    if emit:
        w_specs = [pl.BlockSpec((None, d, tf), lambda i, f: (layer, 0, f)),
                   pl.BlockSpec((None, tf, d), lambda i, f: (layer, f, 0))]
    else:
        w_specs = [pl.BlockSpec((d, tf), lambda i, f: (0, f)), pl.BlockSpec((tf, d), lambda i, f: (f, 0))]
    out_specs = [pl.BlockSpec((tm, d), lambda i, f: (i, 0))]
    out_shape = [jax.ShapeDtypeStruct((m, d), F32)]
    if emit:
        out_specs += [pl.BlockSpec((d, tf), lambda i, f: (0, f)), pl.BlockSpec((tf, d), lambda i, f: (f, 0))]
        out_shape += [jax.ShapeDtypeStruct((d, ff), BF16), jax.ShapeDtypeStruct((ff, d), BF16)]
    outs = pl.pallas_call(
        functools.partial(_mlp_kernel, final_norm=final_norm, emit=emit),
        grid=(m // tm, ff // tf),
        in_specs=[pl.BlockSpec((tm, d), lambda i, f: (i, 0)),
                  pl.BlockSpec((1, d), lambda i, f: (0, 0))] + w_specs
        + [pl.BlockSpec((1, d), lambda i, f: (0, 0))],
        out_specs=out_specs,
        out_shape=out_shape,
        scratch_shapes=[pltpu.VMEM((tm, d), BF16)],
        compiler_params=_cparams(("parallel", "arbitrary")),
        name="mlp",
    )(x, g.reshape(1, d), w1, w2, gf.reshape(1, d))
    return outs if emit else outs[0]


def _flash_init(m_scr, l_scr, acc_scr):
    m_scr[...] = jnp.full(m_scr.shape, NEG, F32)
    l_scr[...] = jnp.zeros(l_scr.shape, F32)
    acc_scr[...] = jnp.zeros(acc_scr.shape, F32)


def _flash_tile(qs, kt, vt, m_scr, l_scr, acc_scr, *, heads, bias=None, mask_add=None):
    s = lax.dot_general(qs, kt, (((1,), (1,)), ((), ())), preferred_element_type=F32)
    rows, tk = s.shape
    if bias is not None or mask_add is not None:
        s3 = s.reshape(heads, rows // heads, tk)
        if bias is not None:
            s3 = s3 + bias
        if mask_add is not None:
            s3 = s3 + mask_add[None]
        s = s3.reshape(rows, tk)
    m_prev = m_scr[...]
    m_next = jnp.maximum(m_prev, jnp.max(s, axis=1, keepdims=True))
    p = jnp.exp2(s - jnp.tile(m_next, (1, tk // LANES)))
    alpha = jnp.exp2(m_prev - m_next)
    l_scr[...] = alpha * l_scr[...] + jnp.sum(p, axis=1, keepdims=True)
    acc_scr[...] = alpha * acc_scr[...] + jnp.dot(p.astype(BF16), vt, preferred_element_type=F32)
    m_scr[...] = m_next


def _kv_tile(k_ref, v_ref, j, tk):
    off = pl.multiple_of(j * tk, tk)
    return k_ref[pl.ds(off, tk), :].astype(BF16), v_ref[pl.ds(off, tk), :].astype(BF16)


def _diff_lambda_in_kernel(lv_ref, lam_init):
    lv = lv_ref[...]
    a = jnp.sum(lv[0:1] * lv[1:2], axis=1, keepdims=True)
    b = jnp.sum(lv[2:3] * lv[3:4], axis=1, keepdims=True)
    return jnp.exp(a) - jnp.exp(b) + lam_init


DIFF_HEADS_PER_STEP = 4


def _diff_attn_kernel(lv_ref, q_ref, k_ref, v_ref, bias_ref, g_ref, o_ref, m_scr, l_scr, acc_scr, *, lam_init):
    qi = pl.program_id(2)
    tq = q_ref.shape[0]
    tk = bias_ref.shape[-1]
    hp = q_ref.shape[1] // DA_V
    lane = lax.broadcasted_iota(jnp.int32, (tq, DA_V), 1)
    qs = []
    for h in range(hp):
        q = q_ref[:, h * DA_V:(h + 1) * DA_V] * (DA_QK ** -0.5 * LOG2E)
        qs.append(jnp.concatenate([jnp.where(lane < DA_QK, q, 0.0), jnp.where(lane >= DA_QK, q, 0.0)],
                                  axis=0).astype(BF16))
    _flash_init(m_scr, l_scr, acc_scr)

    def tiles(j, which):
        off = pl.multiple_of(j * tk, tk)
        for h in range(hp):
            cols = slice(h * DA_V, (h + 1) * DA_V)
            kt = k_ref[pl.ds(off, tk), cols].astype(BF16)
            vt = v_ref[pl.ds(off, tk), cols].astype(BF16)
            _flash_tile(qs[h], kt, vt, m_scr.at[h], l_scr.at[h], acc_scr.at[h], heads=2,
                        bias=None if which is None else bias_ref[h, which][None])

    def far(j, c):
        tiles(j, None)
        return c

    lax.fori_loop(0, jnp.maximum(qi - 1, 0), far, 0)

    @pl.when(qi == 0)
    def _():
        tiles(qi, 0)

    @pl.when(qi >= 1)
    def _():
        tiles(qi - 1, 1)
        tiles(qi, 0)

    lam = _diff_lambda_in_kernel(lv_ref, lam_init)
    for h in range(hp):
        o = acc_scr[h] / l_scr[h]
        a = o[:tq] - lam * o[tq:]
        o_ref[:, h * DA_V:(h + 1) * DA_V] = (_rms_rows(a, g_ref[...]) * (1.0 - lam_init)).astype(BF16)


def _diff_attn_prompt(z, lam_vec, subln_g, bias, lam_init, batch, seq):
    nq = seq // TQ
    hp = DIFF_HEADS_PER_STEP
    groups = N_HEADS // hp
    wide = hp * DA_V
    return pl.pallas_call(
        functools.partial(_diff_attn_kernel, lam_init=lam_init),
        grid=(batch, groups, nq),
        in_specs=[pl.BlockSpec((4, DA_QK), lambda b, h, i: (0, 0)),
                  pl.BlockSpec((TQ, wide), lambda b, h, i: (b * nq + i, h)),
                  pl.BlockSpec((seq, wide), lambda b, h, i: (b, groups + h)),
                  pl.BlockSpec((seq, wide), lambda b, h, i: (b, 2 * groups + h)),
                  pl.BlockSpec((hp, bias.shape[1], TQ, TK), lambda b, h, i: (h, 0, 0, 0)),
                  pl.BlockSpec((1, DA_V), lambda b, h, i: (0, 0))],
        out_specs=pl.BlockSpec((TQ, wide), lambda b, h, i: (b * nq + i, h)),
        out_shape=jax.ShapeDtypeStruct((batch * seq, N_HEADS * DA_V), BF16),
        scratch_shapes=[pltpu.VMEM((hp, 2 * TQ, LANES), F32)] * 3,
        compiler_params=_cparams(("parallel", "parallel", "arbitrary")),
        name="diff_attn_prompt",
    )(lam_vec, z, z, z, bias, subln_g.reshape(1, DA_V))


def _toeplitz_buckets(tq, tk):
    i = np.arange(tq)[:, None]
    j = np.arange(tk)[None, :]
    diag = _bucket_or_masked(i - j, i >= j)
    sub = _bucket_or_masked(tk + i - j, np.ones((tq, tk), bool))
    return np.stack([diag, sub])


HALO = 16


def _pool_kernel(u_ref, halo_ref, prev_ref, w_ref, scale_ref, o_ref, ext_scr, *, pos0):
    i = pl.program_id(1)
    tm = u_ref.shape[0]
    ext_scr[0:HALO, :] = jnp.where(i == 0, prev_ref[...], halo_ref[...])
    ext_scr[HALO:HALO + tm, :] = u_ref[...]
    pos = pos0 + i * tm + lax.broadcasted_iota(jnp.int32, (tm, 1), 0)
    outs = []
    for g, w in enumerate(POOL_WINDOWS):
        c0, c1 = g * POOL_GROUP, (g + 1) * POOL_GROUP
        x0 = ext_scr[HALO:HALO + tm, c0:c1]
        win = x0
        for k in range(1, w):
            win = win + ext_scr[HALO - k:HALO - k + tm, c0:c1]
        cnt = jnp.minimum(pos + 1, w).astype(F32)
        d = win / cnt - x0
        outs.append(jnp.dot(d.astype(BF16), w_ref[g], preferred_element_type=F32))
    o_ref[...] = (jnp.concatenate(outs, axis=-1) * scale_ref[...]).astype(BF16)


def _pool_mix(z, prev, w_pool, scale, pos0, batch, seq, tm):
    nb = seq // tm
    ucol = 3
    return pl.pallas_call(
        functools.partial(_pool_kernel, pos0=pos0),
        grid=(batch, nb),
        in_specs=[pl.BlockSpec((tm, MIX_WIDTH), lambda b, i: (b * nb + i, ucol)),
                  pl.BlockSpec((HALO, MIX_WIDTH),
                               lambda b, i: (jnp.maximum((b * nb + i) * (tm // HALO) - 1, 0), ucol)),
                  pl.BlockSpec((None, HALO, MIX_WIDTH), lambda b, i: (b, 0, 0)),
                  pl.BlockSpec((4, POOL_GROUP, POOL_GROUP), lambda b, i: (0, 0, 0)),
                  pl.BlockSpec((1, MIX_WIDTH), lambda b, i: (0, 0))],
        out_specs=pl.BlockSpec((tm, MIX_WIDTH), lambda b, i: (b * nb + i, 0)),
        out_shape=jax.ShapeDtypeStruct((batch * seq, MIX_WIDTH), BF16),
        scratch_shapes=[pltpu.VMEM((HALO + tm, MIX_WIDTH), F32)],
        compiler_params=_cparams(("parallel", "arbitrary")),
        name="pool_mix",
    )(z, z, prev, w_pool, scale.reshape(1, MIX_WIDTH))


COL_CB, COL_CC, COL_CH, COL_KC, COL_KS, COL_KW, COL_GATE = 8, 16, 24, 32, 36, 40, 44


def _odd_weight_layout(w_in):
    offs = np.cumsum((0,) + ODD_SIZES)
    q, kc, ks_, kw, gates, cb, cc, ch = [w_in[:, offs[i]:offs[i + 1]] for i in range(8)]
    pad = jnp.zeros((w_in.shape[0], LANES - ODD_SIZES[4]), w_in.dtype)
    return jnp.concatenate([q, cb, cc, ch, kc, ks_, kw, gates, pad], axis=1)


def _select_blocks(score_t, q0, nb):
    nbp, tq = score_t.shape
    blk = lax.broadcasted_iota(jnp.int32, (nbp, tq), 0)
    cur = (q0 + lax.broadcasted_iota(jnp.int32, (nbp, tq), 1)) // SLC_BLOCK
    forced = (blk == 0) | (blk == cur) | (blk == cur - 1)
    future = (blk > cur) | (blk >= nb)
    s = jnp.where(forced, jnp.inf, jnp.where(future, -jnp.inf, score_t))
    cnt = jnp.zeros((nbp, tq), jnp.int32)
    for n in range(nb):
        row = s[n:n + 1, :]
        beats = (row > s) | ((row == s) & (blk > n))
        cnt = cnt + beats.astype(jnp.int32)
    return (cnt < min(SLC_TOPK, nb)).astype(F32)


def _nsa_prompt_kernel(q_ref, kc_ref, vc_ref, ks_ref, vs_ref, kw_ref, vw_ref, gate_ref, tb_ref, cb_ref, e_ref,
                       o_ref, kcm_scr, vcm_scr, m_scr, l_scr, acc_scr):
    g = pl.program_id(1)
    qi = pl.program_id(2)
    tq = q_ref.shape[0]
    tk = tb_ref.shape[-1]
    seq = kc_ref.shape[0]
    nb = seq // CMP_BLOCK
    nbp = -(-nb // 8) * 8
    scale = NSA_DH ** -0.5
    hp = NSA_HPG

    @pl.when(qi == 0)
    def _():
        kcm_scr[...] = jnp.zeros(kcm_scr.shape, BF16)
        vcm_scr[...] = jnp.zeros(vcm_scr.shape, BF16)
        kcm_scr[0:nb, :] = (jnp.sum(kc_ref[...].reshape(nb, CMP_BLOCK, NSA_DH), axis=1) / CMP_BLOCK).astype(BF16)
        vcm_scr[0:nb, :] = (jnp.sum(vc_ref[...].reshape(nb, CMP_BLOCK, NSA_DH), axis=1) / CMP_BLOCK).astype(BF16)

    qh = [q_ref[:, h * NSA_DH:(h + 1) * NSA_DH].astype(BF16) for h in range(hp)]
    qs = jnp.concatenate([(q_ref[:, h * NSA_DH:(h + 1) * NSA_DH] * (scale * LOG2E)).astype(BF16)
                          for h in range(hp)], axis=0)

    kcm = kcm_scr[...]
    vcm = vcm_scr[...]
    o_cmp = []
    p_grp_t = jnp.zeros((LANES, tq), F32)
    for h in range(hp):
        lt = lax.dot_general(kcm, qh[h], (((1,), (1,)), ((), ())), preferred_element_type=F32) * scale
        bt = cb_ref[h]
        lt = lt + bt
        p = jnp.exp(lt - jnp.max(lt, axis=0, keepdims=True))
        p = p / jnp.sum(p, axis=0, keepdims=True)
        p = jnp.where(bt > 0.5 * NEG, p, 0.0)
        p_grp_t = p_grp_t + p
        o_cmp.append(jnp.dot(p.T.astype(BF16), vcm, preferred_element_type=F32))

    sel_t = _select_blocks(p_grp_t[0:nbp, :], qi * tq, nb)
    if nbp < LANES:
        sel_t = jnp.concatenate([sel_t, jnp.zeros((LANES - nbp, tq), F32)], axis=0)
    sel_neg = ((sel_t.T - 1.0) * MASK_BIG).astype(BF16)

    def sel_mask(j):
        return jnp.dot(sel_neg, e_ref[j], preferred_element_type=F32)

    def slc_tile(j, which):
        kt, vt = _kv_tile(ks_ref, vs_ref, j, tk)
        _flash_tile(qs, kt, vt, m_scr.at[0], l_scr.at[0], acc_scr.at[0], heads=hp,
                    bias=None if which is None else tb_ref[:, which], mask_add=sel_mask(j))

    def win_tile(j, which):
        kt, vt = _kv_tile(kw_ref, vw_ref, j, tk)
        _flash_tile(qs, kt, vt, m_scr.at[1], l_scr.at[1], acc_scr.at[1], heads=hp, bias=tb_ref[:, which])

    _flash_init(m_scr, l_scr, acc_scr)
    n_far = jnp.maximum(qi - 1, 0)

    def far_pair(jj, c):
        slc_tile(2 * jj, None)
        slc_tile(2 * jj + 1, None)
        return c

    lax.fori_loop(0, n_far // 2, far_pair, 0)

    @pl.when(n_far % 2 == 1)
    def _():
        slc_tile(n_far - 1, None)

    def near_tiles(before):
        if before >= 2:
            win_tile(qi - 2, 2)
        if before >= 1:
            slc_tile(qi - 1, 1)
            win_tile(qi - 1, 1)
        slc_tile(qi, 0)
        win_tile(qi, 0)

    @pl.when(qi == 0)
    def _():
        near_tiles(0)

    @pl.when(qi == 1)
    def _():
        near_tiles(1)

    @pl.when(qi >= 2)
    def _():
        near_tiles(2)

    o_slc = acc_scr[0] / l_scr[0]
    o_win = acc_scr[1] / l_scr[1]

    sig = jax.nn.sigmoid(gate_ref[...])
    lane = lax.broadcasted_iota(jnp.int32, sig.shape, 1)

    def gate(branch, h):
        col = branch * N_HEADS + g * hp + h
        return jnp.sum(jnp.where(lane == col, sig, 0.0), axis=1, keepdims=True)

    for h in range(hp):
        rows = slice(h * tq, (h + 1) * tq)
        o = gate(0, h) * o_cmp[h] + gate(1, h) * o_slc[rows] + gate(2, h) * o_win[rows]
        o_ref[:, h * NSA_DH:(h + 1) * NSA_DH] = o.astype(BF16)


def _nsa_buckets(seq, tq, tk):
    i = np.arange(tq)[:, None]
    j = np.arange(tk)[None, :]
    toe = _toeplitz_buckets(tq, tk)
    assert WINDOW == 2 * tk and tq == tk
    win2 = np.where(i < j, NUM_BUCKETS - 1, -1).astype(np.int32)
    tiles = np.concatenate([toe, win2[None]])
    nb = seq // CMP_BLOCK
    blk_end = (np.arange(LANES)[:, None] + 1) * CMP_BLOCK - 1
    dist = np.arange(seq)[None, :] - blk_end
    cmp_t = _bucket_or_masked(dist, (dist >= 0) & (np.arange(LANES)[:, None] < nb))
    nk = seq // tk
    key_blk = (np.arange(nk)[:, None, None] * tk + np.arange(tk)[None, None, :]) // SLC_BLOCK
    expand = (key_blk == np.arange(LANES)[None, :, None]).astype(np.float32)
    return tiles, cmp_t, expand


def _nsa_prompt(z, tile_bias, cmp_bias_t, expand, batch, seq):
    nq = seq // TQ
    nk = seq // TK
    kv = lambda col: pl.BlockSpec((seq, NSA_DH), lambda b, g, i: (b, col + g))
    return pl.pallas_call(
        _nsa_prompt_kernel,
        grid=(batch, NSA_KV, nq),
        in_specs=[pl.BlockSpec((TQ, NSA_HPG * NSA_DH), lambda b, g, i: (b * nq + i, g)),
                  kv(COL_KC), kv(COL_KC + 2), kv(COL_KS), kv(COL_KS + 2), kv(COL_KW), kv(COL_KW + 2),
                  pl.BlockSpec((TQ, LANES), lambda b, g, i: (b * nq + i, COL_GATE)),
                  pl.BlockSpec((NSA_HPG, 3, TQ, TK), lambda b, g, i: (g, 0, 0, 0)),
                  pl.BlockSpec((NSA_HPG, LANES, TQ), lambda b, g, i: (g, 0, i)),
                  pl.BlockSpec((nk, LANES, TK), lambda b, g, i: (0, 0, 0))],
        out_specs=pl.BlockSpec((TQ, NSA_HPG * NSA_DH), lambda b, g, i: (b * nq + i, g)),
        out_shape=jax.ShapeDtypeStruct((batch * seq, MIX_WIDTH), BF16),
        scratch_shapes=[pltpu.VMEM((LANES, NSA_DH), BF16), pltpu.VMEM((LANES, NSA_DH), BF16)]
        + [pltpu.VMEM((2, NSA_HPG * TQ, LANES), F32)] * 3,
        compiler_params=_cparams(("parallel", "parallel", "arbitrary")),
        name="nsa_prompt",
    )(z, z, z, z, z, z, z, z, tile_bias, cmp_bias_t, expand)


def _conv_kernel(cb_ref, cc_ref, ch_ref, hc_ref, hh_ref, prev_ref, w_ref, o_ref, tail_ref, ext_scr):
    i = pl.program_id(1)
    tm = cb_ref.shape[0]
    e = cc_ref[...] * ch_ref[...]
    ext_scr[0:8, :] = jnp.where(i == 0, prev_ref[...], hc_ref[...] * hh_ref[...])
    ext_scr[8:8 + tm, :] = e
    w = w_ref[...]
    y = w[0:1] * ext_scr[6:6 + tm, :]
    y = y + w[1:2] * ext_scr[7:7 + tm, :]
    y = y + w[2:3] * e
    o_ref[...] = (cb_ref[...] * y).astype(BF16)

    @pl.when(i == pl.num_programs(1) - 1)
    def _():
        tail_ref[...] = e[tm - 8:tm, :]


def _short_conv(z, prev, w_conv, batch, seq, tm):
    nb = seq // tm
    cw = MIX_WIDTH // LANES
    blk = lambda c: pl.BlockSpec((tm, MIX_WIDTH), lambda b, i: (b * nb + i, c // cw))
    halo = lambda c: pl.BlockSpec((8, MIX_WIDTH), lambda b, i: (jnp.maximum((b * nb + i) * (tm // 8) - 1, 0), c // cw))
    return pl.pallas_call(
        _conv_kernel,
        grid=(batch, nb),
        in_specs=[blk(COL_CB), blk(COL_CC), blk(COL_CH), halo(COL_CC), halo(COL_CH),
                  pl.BlockSpec((None, 8, MIX_WIDTH), lambda b, i: (b, 0, 0)),
                  pl.BlockSpec((8, MIX_WIDTH), lambda b, i: (0, 0))],
        out_specs=[pl.BlockSpec((tm, MIX_WIDTH), lambda b, i: (b * nb + i, 0)),
                   pl.BlockSpec((None, 8, MIX_WIDTH), lambda b, i: (b, 0, 0))],
        out_shape=[jax.ShapeDtypeStruct((batch * seq, MIX_WIDTH), BF16),
                   jax.ShapeDtypeStruct((batch, 8, MIX_WIDTH), F32)],
        scratch_shapes=[pltpu.VMEM((8 + tm, MIX_WIDTH), F32)],
        compiler_params=_cparams(("parallel", "arbitrary")),
        name="short_conv",
    )(z, z, z, z, z, prev, jnp.pad(w_conv, ((0, 8 - CONV_WIDTH), (0, 0))))


T8 = 8
NEW_ROWS = 16
DIFF_PAGES_PER_STEP = 8
NSA_PAGES_PER_STEP = 16
NSA_ROW = 2 * NSA_KV
V_SHIFT = NSA_KV
NB_DEC = 288


def _nt(a, b):
    return lax.dot_general(a, b, (((1,), (1,)), ((), ())), preferred_element_type=F32)


def _softmax_update(st, v_tiles, width, m_ref, l_ref, acc_ref):
    m_prev = m_ref[...]
    m_next = jnp.maximum(m_prev, jnp.max(st, axis=1, keepdims=True))
    p = jnp.exp(st - jnp.tile(m_next, (1, st.shape[1] // LANES)))
    alpha = jnp.exp(m_prev - m_next)
    l_ref[...] = alpha * l_ref[...] + jnp.sum(p, axis=1, keepdims=True)
    pv = jnp.zeros(acc_ref.shape, F32)
    for k, vt in enumerate(v_tiles):
        pv = pv + jnp.dot(p[:, k * width:(k + 1) * width].astype(BF16), vt, preferred_element_type=F32)
    acc_ref[...] = alpha * acc_ref[...] + pv
    m_ref[...] = m_next


def _diff_decode_kernel(pt_ref, q_ref, new_ref, mask_ref, newmask_ref, lv_ref, g_ref, *rest, lam_init):
    pages = rest[:DIFF_PAGES_PER_STEP]
    o_ref, m_scr, l_scr, acc_scr = rest[DIFF_PAGES_PER_STEP:]
    s = pl.program_id(1)
    last = pl.num_programs(1) - 1
    width = PAGE_SIZE * N_HEADS

    @pl.when(s == 0)
    def _():
        _flash_init(m_scr, l_scr, acc_scr)

    q = q_ref[...] * (DA_QK ** -0.5)
    lane = lax.broadcasted_iota(jnp.int32, q.shape, 1)
    qs = jnp.concatenate([jnp.where(lane < DA_QK, q, 0.0), jnp.where(lane >= DA_QK, q, 0.0)], axis=0).astype(BF16)
    upd = functools.partial(_softmax_update, m_ref=m_scr, l_ref=l_scr, acc_ref=acc_scr)

    is_last = s == last
    tiles, values = [], []
    for k, pg in enumerate(pages):
        st = _nt(qs, pg[:, 0].reshape(width, DA_V).astype(BF16))
        if k == DIFF_PAGES_PER_STEP - 1:
            st = st + jnp.where(is_last, mask_ref[1], mask_ref[0])
        else:
            st = st + mask_ref[0]
        tiles.append(st)
        values.append(pg[:, 1].reshape(width, DA_V).astype(BF16))
    upd(jnp.concatenate(tiles, axis=1), values, width)

    @pl.when(is_last)
    def _():
        nw = NEW_ROWS * N_HEADS
        st = _nt(qs, new_ref[:, 0].reshape(nw, DA_V).astype(BF16)) + newmask_ref[...]
        upd(st, [new_ref[:, 1].reshape(nw, DA_V).astype(BF16)], nw)
        o = acc_scr[...] / l_scr[...]
        half = o.shape[0] // 2
        a = o[:half] - _diff_lambda_in_kernel(lv_ref, lam_init) * o[half:]
        o_ref[...] = _rms_rows(a, g_ref[...]) * (1.0 - lam_init)


def _diff_decode(page_table, q_ht, kv_new, masks, new_mask, lam_vec, subln_g, cache, layer, lam_init):
    batch, n_pages = page_table.shape
    steps = n_pages // DIFF_PAGES_PER_STEP
    rows = q_ht.shape[1]

    def page_spec(k):
        return pl.BlockSpec((None, None, PAGE_SIZE, 2, N_HEADS, DA_V),
                            lambda b, s, pt: (layer, pt[b, s * DIFF_PAGES_PER_STEP + k], 0, 0, 0, 0))

    grid_spec = pltpu.PrefetchScalarGridSpec(
        num_scalar_prefetch=1,
        grid=(batch, steps),
        in_specs=[pl.BlockSpec((None, rows, DA_V), lambda b, s, pt: (b, 0, 0)),
                  pl.BlockSpec((None, NEW_ROWS, 2, N_HEADS, DA_V), lambda b, s, pt: (b, 0, 0, 0, 0)),
                  pl.BlockSpec(masks.shape, lambda b, s, pt: (0, 0, 0)),
                  pl.BlockSpec(new_mask.shape, lambda b, s, pt: (0, 0)),
                  pl.BlockSpec((4, DA_QK), lambda b, s, pt: (0, 0)),
                  pl.BlockSpec((1, DA_V), lambda b, s, pt: (0, 0))]
        + [page_spec(k) for k in range(DIFF_PAGES_PER_STEP)],
        out_specs=pl.BlockSpec((None, rows, DA_V), lambda b, s, pt: (b, 0, 0)),
        scratch_shapes=[pltpu.VMEM((2 * rows, LANES), F32)] * 3,
    )
    return pl.pallas_call(
        functools.partial(_diff_decode_kernel, lam_init=lam_init),
        grid_spec=grid_spec,
        out_shape=jax.ShapeDtypeStruct((batch, rows, DA_V), F32),
        compiler_params=_cparams(("parallel", "arbitrary")),
        name="diff_decode",
    )(page_table, q_ht, kv_new, masks, new_mask, lam_vec, subln_g.reshape(1, DA_V),
      *([cache] * DIFF_PAGES_PER_STEP))


def _nsa_page_specs(layer):
    def spec(k):
        return pl.BlockSpec((None, None, PAGE_SIZE * NSA_ROW, NSA_DH),
                            lambda b, s, pt: (layer, pt[b, s * NSA_PAGES_PER_STEP + k], 0, 0))
    return [spec(k) for k in range(NSA_PAGES_PER_STEP)]


def _cmp_means_kernel(pt_ref, *rest):
    pages = rest[:NSA_PAGES_PER_STEP]
    o_ref = rest[NSA_PAGES_PER_STEP]
    per_blk = CMP_BLOCK * NSA_ROW // 8
    low = lax.broadcasted_iota(jnp.int32, (8, NSA_DH), 0) < NSA_ROW
    out = []
    for pg in pages:
        x = pg[...].reshape(PAGE_SIZE * NSA_ROW // 8, 8, NSA_DH)
        t0 = jnp.sum(x[0:per_blk], axis=0)
        t1 = jnp.sum(x[per_blk:2 * per_blk], axis=0)
        t0 = t0 + pltpu.roll(t0, NSA_ROW, axis=0)
        t1 = t1 + pltpu.roll(t1, NSA_ROW, axis=0)
        out.append(jnp.where(low, t0, t1) / CMP_BLOCK)
    o_ref[...] = jnp.concatenate(out, axis=0)


def _cmp_means(page_table, cache, layer):
    batch, n_pages = page_table.shape
    steps = n_pages // NSA_PAGES_PER_STEP
    per_step = NSA_PAGES_PER_STEP * 8
    grid_spec = pltpu.PrefetchScalarGridSpec(
        num_scalar_prefetch=1,
        grid=(batch, steps),
        in_specs=_nsa_page_specs(layer),
        out_specs=pl.BlockSpec((None, per_step, NSA_DH), lambda b, s, pt: (b, s, 0)),
    )
    return pl.pallas_call(
        _cmp_means_kernel,
        grid_spec=grid_spec,
        out_shape=jax.ShapeDtypeStruct((batch, steps * per_step, NSA_DH), F32),
        compiler_params=_cparams(("parallel", "arbitrary")),
        name="cmp_means",
    )(page_table, *([cache] * NSA_PAGES_PER_STEP))


def _cmp_select_decode_kernel(q_ref, kvc_ref, new_ref, tile_ref, ocmp_ref, sel_ref, k_scr, *, past_len, n_new):
    n_past = kvc_ref.shape[0]
    width = NB_DEC * NSA_ROW
    scale = NSA_DH ** -0.5
    k_scr[...] = jnp.zeros(k_scr.shape, F32)
    k_scr[0:n_past, :] = kvc_ref[...]
    real = lax.broadcasted_iota(jnp.int32, new_ref.shape, 0) < n_new
    tot = jnp.sum(jnp.where(real, new_ref[...], 0.0), axis=0, keepdims=True) / CMP_BLOCK
    new4 = jnp.concatenate([tot[:, c * NSA_DH:(c + 1) * NSA_DH] for c in range(NSA_ROW)]
                           + [jnp.zeros((8 - NSA_ROW, NSA_DH), F32)], axis=0)
    k_scr[n_past:n_past + 8, :] = new4

    q = q_ref[...].astype(BF16)
    tile = tile_ref[...]
    lg = _nt(q, k_scr[0:width, :].astype(BF16)) * scale + tile
    p = jnp.exp(lg - jnp.max(lg, axis=1, keepdims=True))
    p = p / jnp.sum(p, axis=1, keepdims=True)
    p = jnp.where(tile > 0.5 * NEG, p, 0.0)
    ocmp_ref[...] = jnp.dot(p.astype(BF16), k_scr[V_SHIFT:V_SHIFT + width, :].astype(BF16),
                            preferred_element_type=F32)

    lane = lax.broadcasted_iota(jnp.int32, (T8, width), 1)
    blk = lane // NSA_ROW
    cur = (past_len + lax.broadcasted_iota(jnp.int32, (T8, width), 0)) // SLC_BLOCK
    rows_g = NSA_HPG * T8
    for g in range(NSA_KV):
        mine = (lane % NSA_ROW) == g
        score = sum(p[g * rows_g + h * T8:g * rows_g + (h + 1) * T8] for h in range(NSA_HPG))
        forced = mine & ((blk == 0) | (blk == cur) | (blk == cur - 1))
        dead = (blk > cur) | jnp.logical_not(mine)
        sc = jnp.where(forced, jnp.inf, jnp.where(dead, -jnp.inf, score))
        taken = jnp.zeros((T8, width), jnp.bool_)
        for _ in range(SLC_TOPK):
            sm = jnp.where(taken, -jnp.inf, sc)
            cand = (sm == jnp.max(sm, axis=1, keepdims=True)) & jnp.logical_not(taken)
            idx = jnp.min(jnp.where(cand, lane, width), axis=1, keepdims=True)
            taken = taken | (lane == idx)
        sel_ref[g * T8:(g + 1) * T8, :] = taken.astype(F32)


def _cmp_select_decode(qg, kvc, kv_new, tile, past_len, n_new):
    batch, rows, _ = qg.shape
    width = NB_DEC * NSA_ROW
    n_past = kvc.shape[1]
    return pl.pallas_call(
        functools.partial(_cmp_select_decode_kernel, past_len=past_len, n_new=n_new),
        grid=(batch,),
        in_specs=[pl.BlockSpec((None, rows, NSA_DH), lambda b: (b, 0, 0)),
                  pl.BlockSpec((None, n_past, NSA_DH), lambda b: (b, 0, 0)),
                  pl.BlockSpec((None, T8, NSA_ROW * NSA_DH), lambda b: (b, 0, 0)),
                  pl.BlockSpec((rows, width), lambda b: (0, 0))],
        out_specs=[pl.BlockSpec((None, rows, NSA_DH), lambda b: (b, 0, 0)),
                   pl.BlockSpec((None, NSA_KV * T8, width), lambda b: (b, 0, 0))],
        out_shape=[jax.ShapeDtypeStruct((batch, rows, NSA_DH), F32),
                   jax.ShapeDtypeStruct((batch, NSA_KV * T8, width), F32)],
        scratch_shapes=[pltpu.VMEM((width + 8, NSA_DH), F32)],
        compiler_params=_cparams(("parallel",)),
        name="cmp_select_decode",
    )(qg, kvc, kv_new, tile)


def _masked_scores(st, sel, expand):
    rows, n = st.shape
    add = (jnp.dot(sel.astype(BF16), expand, preferred_element_type=F32) - 1.0) * (-NEG)
    st4 = st.reshape(NSA_KV, NSA_HPG, T8, n) + add.reshape(NSA_KV, 1, T8, n)
    return st4.reshape(rows, n)


def _slc_decode_kernel(pt_ref, q_ref, sel_ref, selnew_ref, e_ref, new_ref, tile_ref, newtile_ref, *rest):
    pages = rest[:NSA_PAGES_PER_STEP]
    o_ref, m_scr, l_scr, acc_scr = rest[NSA_PAGES_PER_STEP:]
    s = pl.program_id(1)
    last = pl.num_programs(1) - 1
    scale = NSA_DH ** -0.5
    width = PAGE_SIZE * NSA_ROW

    @pl.when(s == 0)
    def _():
        _flash_init(m_scr, l_scr, acc_scr)

    q = q_ref[...].astype(BF16)
    upd = functools.partial(_softmax_update, m_ref=m_scr, l_ref=l_scr, acc_ref=acc_scr)
    is_last = (s == last).astype(F32)
    tiles, values = [], []
    for k, pg in enumerate(pages):
        x = pg[...]
        st = _nt(q, x.astype(BF16)) * scale
        if k == NSA_PAGES_PER_STEP - 1:
            st = st + is_last * tile_ref[...]
        tiles.append(st)
        values.append(pltpu.roll(x, width - V_SHIFT, axis=0).astype(BF16))
    upd(_masked_scores(jnp.concatenate(tiles, axis=1), sel_ref[...], e_ref[...]), values, width)

    @pl.when(s == last)
    def _():
        x = new_ref[...]
        n = x.shape[0]
        st = _nt(q, x.astype(BF16)) * scale + newtile_ref[...]
        upd(_masked_scores(st, selnew_ref[...], e_ref[:, 0:n]), [pltpu.roll(x, n - V_SHIFT, axis=0).astype(BF16)], n)
        o_ref[...] = acc_scr[...] / l_scr[...]


def _slc_decode(page_table, qg, sel, expand, kv_new, tile, new_tile, cache, layer):
    batch, n_pages = page_table.shape
    steps = n_pages // NSA_PAGES_PER_STEP
    rows = qg.shape[1]
    keys = NSA_PAGES_PER_STEP * PAGE_SIZE * NSA_ROW
    grid_spec = pltpu.PrefetchScalarGridSpec(
        num_scalar_prefetch=1,
        grid=(batch, steps),
        in_specs=[pl.BlockSpec((None, rows, NSA_DH), lambda b, s, pt: (b, 0, 0)),
                  pl.BlockSpec((None, NSA_KV * T8, LANES), lambda b, s, pt: (b, 0, s)),
                  pl.BlockSpec((None, NSA_KV * T8, LANES), lambda b, s, pt: (b, 0, steps)),
                  pl.BlockSpec((LANES, keys), lambda b, s, pt: (0, 0)),
                  pl.BlockSpec((None,) + kv_new.shape[1:], lambda b, s, pt: (b, 0, 0)),
                  pl.BlockSpec(tile.shape, lambda b, s, pt: (0, 0)),
                  pl.BlockSpec(new_tile.shape, lambda b, s, pt: (0, 0))]
        + _nsa_page_specs(layer),
        out_specs=pl.BlockSpec((None, rows, NSA_DH), lambda b, s, pt: (b, 0, 0)),
        scratch_shapes=[pltpu.VMEM((rows, LANES), F32)] * 3,
    )
    return pl.pallas_call(
        _slc_decode_kernel,
        grid_spec=grid_spec,
        out_shape=jax.ShapeDtypeStruct((batch, rows, NSA_DH), F32),
        compiler_params=_cparams(("parallel", "arbitrary")),
        name="slc_decode",
    )(page_table, qg, sel, sel, expand, kv_new, tile, new_tile, *([cache] * NSA_PAGES_PER_STEP))


def _win_combine_decode_kernel(q_ref, state_ref, new_ref, tile_ref, gate_ref, ocmp_ref, oslc_ref, o_ref):
    scale = NSA_DH ** -0.5
    q = q_ref[...].astype(BF16)
    xs, xn = state_ref[...], new_ref[...]
    ns, nn = xs.shape[0], xn.shape[0]
    st = jnp.concatenate([_nt(q, xs.astype(BF16)), _nt(q, xn.astype(BF16))], axis=1) * scale + tile_ref[...]
    p = jnp.exp(st - jnp.max(st, axis=1, keepdims=True))
    p = (p / jnp.sum(p, axis=1, keepdims=True)).astype(BF16)
    o_win = (jnp.dot(p[:, 0:ns], pltpu.roll(xs, ns - V_SHIFT, axis=0).astype(BF16), preferred_element_type=F32)
             + jnp.dot(p[:, ns:], pltpu.roll(xn, nn - V_SHIFT, axis=0).astype(BF16), preferred_element_type=F32))
    sig = jax.nn.sigmoid(gate_ref[...])
    for g in range(NSA_KV):
        for h in range(NSA_HPG):
            head = g * NSA_HPG + h
            rows = slice(head * T8, (head + 1) * T8)
            gate = lambda branch: sig[:, branch * N_HEADS + head:branch * N_HEADS + head + 1]
            o = gate(0) * ocmp_ref[rows, :] + gate(1) * oslc_ref[rows, :] + gate(2) * o_win[rows]
            o_ref[:, head * NSA_DH:(head + 1) * NSA_DH] = o.astype(BF16)


def _win_combine_decode(qg, state_win, kv_new, tile, gates, o_cmp, o_slc):
    batch, rows, _ = qg.shape
    branch = pl.BlockSpec((None, rows, NSA_DH), lambda b: (b, 0, 0))
    return pl.pallas_call(
        _win_combine_decode_kernel,
        grid=(batch,),
        in_specs=[branch,
                  pl.BlockSpec((None,) + state_win.shape[1:], lambda b: (b, 0, 0)),
                  pl.BlockSpec((None,) + kv_new.shape[1:], lambda b: (b, 0, 0)),
                  pl.BlockSpec(tile.shape, lambda b: (0, 0)),
                  pl.BlockSpec((T8, LANES), lambda b: (b, COL_GATE)),
                  branch, branch],
        out_specs=pl.BlockSpec((T8, MIX_WIDTH), lambda b: (b, 0)),
        out_shape=jax.ShapeDtypeStruct((batch * T8, MIX_WIDTH), BF16),
        compiler_params=_cparams(("parallel",)),
        name="win_combine_decode",
    )(qg, state_win, kv_new, tile, gates, o_cmp, o_slc)


def _decode_buckets(past_len, wbuf):
    t = np.arange(T8)[:, None]
    dpos = np.repeat(np.arange(PAGE_SIZE), N_HEADS)[None, :]
    dnew = np.repeat(np.arange(NEW_ROWS), N_HEADS)[None, :]
    npos = np.repeat(np.arange(PAGE_SIZE), NSA_ROW)[None, :]
    nnew = np.repeat(np.arange(PAGE_SIZE // NSA_ROW), NSA_ROW)[None, :]
    nblk = np.repeat(np.arange(NB_DEC), NSA_ROW)[None, :]
    nwin = np.repeat(np.arange(wbuf), NSA_ROW)[None, :]
    d_cmp = past_len + t - ((nblk + 1) * CMP_BLOCK - 1)
    d_win = t + wbuf - nwin
    parts = [
        _bucket_or_masked(PAGE_SIZE + t - dpos, np.ones((T8, dpos.shape[1]), bool)),
        _bucket_or_masked(t - dnew, dnew <= t),
        _bucket_or_masked(PAGE_SIZE + t - npos, np.ones((T8, npos.shape[1]), bool)),
        _bucket_or_masked(t - nnew, nnew <= t),
        _bucket_or_masked(d_cmp, d_cmp >= 0),
        _bucket_or_masked(d_win, (d_win >= 0) & (d_win < WINDOW) & (past_len - wbuf + nwin >= 0)),
    ]
    return np.concatenate(parts, axis=1), np.cumsum([0] + [p.shape[1] for p in parts])


def _pad_rows(x, rows):
    return jnp.pad(x, ((0, 0), (0, rows - x.shape[1])) + ((0, 0),) * (x.ndim - 2))


def _tile_cfg():
    return dict(tm=1024, tn_even=512, tm_odd=512, tn_odd=1152, tn_out=1024, tm_mlp=512, tf=1024, tf_cast=512,
                tm_pool=256)


def kernel(x_prompt, x_sample, cache_diff_kv, cache_cmp_kv, cache_slc_kv, state_win_kv, state_pool, state_conv,
           page_table, rel_bias, norm_mix, norm_mlp, norm_final, even_w_in, even_w_out, diff_lambda, diff_subln,
           pool_w, pool_scale, odd_w_in, odd_w_out, conv_w, mlp_w1, mlp_w2):
    bp, seq, d = x_prompt.shape
    bs, ts, _ = x_sample.shape
    depth = norm_mix.shape[0]
    n_pages = page_table.shape[1]
    past_len = n_pages * PAGE_SIZE
    n_phys = cache_diff_kv.shape[1]
    wbuf = state_win_kv.shape[2]
    n_past_blk = past_len // SLC_BLOCK
    assert seq % TQ == 0 and ts <= T8 and n_pages % NSA_PAGES_PER_STEP == 0 and wbuf == WINDOW
    assert (past_len + ts - 1) // SLC_BLOCK == n_past_blk < NB_DEC
    cfg = _tile_cfg()

    w_in_o = [_odd_weight_layout(odd_w_in[o].astype(BF16)) for o in range(odd_w_in.shape[0])]
    pool_wb = pool_w.astype(BF16)

    tiles, cmp_t, expand = _nsa_buckets(seq, TQ, TK)
    tile_bias = _bias_tiles(rel_bias, tiles, LOG2E)
    cmp_bias_t = _bias_tiles(rel_bias, cmp_t)
    expand = jnp.asarray(expand, BF16)
    dec_buckets, off = _decode_buckets(past_len, wbuf)
    dec = _bias_tiles(rel_bias, dec_buckets)
    part = lambda i: dec[:, :, off[i]:off[i + 1]]

    def diff_rows(x):
        n = x.shape[-1]
        own = (np.arange(n) % N_HEADS)[None, None, :] == np.arange(N_HEADS)[:, None, None]
        x = jnp.where(own, x, NEG)
        return jnp.broadcast_to(x[None], (2,) + x.shape).reshape(2 * N_HEADS * T8, n)

    def nsa_rows(x):
        n = x.shape[-1]
        own = (np.arange(n) % NSA_ROW)[None, None, :] == (np.arange(N_HEADS) // NSA_HPG)[:, None, None]
        return jnp.where(own, x, NEG).reshape(N_HEADS * T8, n)

    diff_masks = jnp.stack([diff_rows(jnp.zeros_like(part(0))), diff_rows(part(0))])
    diff_new_mask = diff_rows(part(1))
    slc_tile = nsa_rows(part(2))
    nsa_new_tile = nsa_rows(part(3))
    cmp_tile = nsa_rows(part(4))
    win_tile = jnp.concatenate([nsa_rows(part(5)), nsa_new_tile], axis=1)
    key = np.arange(NSA_PAGES_PER_STEP * PAGE_SIZE * NSA_ROW)
    sel_lane = (key // (NSA_ROW * SLC_BLOCK)) * NSA_ROW + key % NSA_ROW
    is_key_row = (key % NSA_ROW) < NSA_KV
    expand_dec = jnp.asarray((np.arange(LANES)[:, None] == sel_lane[None, :]) & is_key_row[None, :], BF16)

    cache_cmp = cache_cmp_kv.reshape(cache_cmp_kv.shape[0], n_phys, PAGE_SIZE * NSA_ROW, NSA_DH)
    cache_slc = cache_slc_kv.reshape(cache_slc_kv.shape[0], n_phys, PAGE_SIZE * NSA_ROW, NSA_DH)

    mp, ms = bp * seq, bs * T8
    xp = x_prompt.reshape(mp, d)
    xs = _pad_rows(x_sample, T8).reshape(ms, d)
    tm_p = min(cfg["tm"], mp)
    outs = {k: [] for k in ("diff_s", "pool_p", "pool_s", "cmp_s", "slc_s", "win_s", "conv_p", "conv_s")}
    kvw = NSA_ROW * NSA_DH
    n_even, n_odd = (depth + 1) // 2, depth // 2
    even_taps = ((N_HEADS * DA_V, 2 * N_HEADS * DA_V, 1),)
    odd_taps = tuple((c * LANES, kvw, NSA_ROW) for c in (COL_KC, COL_KS, COL_KW))
    diff_kv_p, nsa_kv_p = (), ()

    for layer in range(depth):
        if layer % 2 == 0:
            e = layer // 2
            lam_init = 0.8 - 0.6 * math.exp(-0.3 * layer)
            zs, (w_in_b,) = _norm_matmul(xs, norm_mix[layer], even_w_in, ms, cfg["tn_even"], w_layer=e)
            zp, diff_kv_p = _norm_matmul(xp, norm_mix[layer], w_in_b, tm_p, cfg["tn_even"], even_taps, n_even, e,
                                         diff_kv_p)
            zp3 = zp.reshape(bp, seq, -1)
            zs3 = zs.reshape(bs, T8, -1)
            o_attn = _diff_attn_prompt(zp, diff_lambda[e], diff_subln[e], tile_bias, lam_init, bp, seq)
            o_pool = _pool_mix(zp, jnp.zeros((bp, HALO, MIX_WIDTH), F32), pool_wb[e], pool_scale[e], 0, bp, seq,
                               cfg["tm_pool"])
            q_ht = zs3[:, :, 0:1024].reshape(bs, T8, N_HEADS, DA_V).transpose(0, 2, 1, 3).reshape(
                bs, N_HEADS * T8, DA_V)
            kv_new = _pad_rows(zs3[:, :, 1024:3072].reshape(bs, T8, 2, N_HEADS, DA_V), NEW_ROWS)
            o_dec = _diff_decode(page_table, q_ht, kv_new, diff_masks, diff_new_mask, diff_lambda[e], diff_subln[e],
                                 cache_diff_kv, e, lam_init)
            o_attn_s = o_dec.reshape(bs, N_HEADS, T8, DA_V).transpose(0, 2, 1, 3).reshape(ms, N_HEADS * DA_V)
            prev = jnp.pad(state_pool[e], ((0, 0), (HALO - POOL_STATE, 0), (0, 0)))
            o_pool_s = _pool_mix(zs, prev, pool_wb[e], pool_scale[e], past_len, bs, T8, T8)
            xs, w_out_b = _out_proj(xs, o_attn_s.astype(BF16), o_pool_s, even_w_out, ms, cfg["tn_out"], w_layer=e)
            xp = _out_proj(xp, o_attn, o_pool, w_out_b, tm_p, cfg["tn_out"])
            outs["diff_s"].append(zs3[:, :ts, 1024:3072].reshape(bs, ts, 2, N_HEADS, DA_V))
            outs["pool_p"].append(zp3[:, seq - POOL_STATE:, 3072:])
            outs["pool_s"].append(jnp.concatenate([state_pool[e], zs3[:, :ts, 3072:]], axis=1)[:, -POOL_STATE:])
        else:
            o = layer // 2
            zp, nsa_kv_p = _norm_matmul(xp, norm_mix[layer], w_in_o[o], min(cfg["tm_odd"], mp), cfg["tn_odd"],
                                        odd_taps, n_odd, o, nsa_kv_p)
            zs, _ = _norm_matmul(xs, norm_mix[layer], w_in_o[o], ms, cfg["tn_odd"])
            zp3 = zp.reshape(bp, seq, -1)
            zs3 = zs.reshape(bs, T8, -1)
            c_kc, c_ks, c_kw = COL_KC * LANES, COL_KS * LANES, COL_KW * LANES
            o_nsa = _nsa_prompt(zp, tile_bias, cmp_bias_t, expand, bp, seq)
            o_conv, tail = _short_conv(zp, jnp.zeros((bp, 8, MIX_WIDTH), F32), conv_w[o], bp, seq, cfg["tm_pool"])
            qg = zs3[:, :, 0:1024].reshape(bs, T8, N_HEADS, NSA_DH).transpose(0, 2, 1, 3).reshape(
                bs, N_HEADS * T8, NSA_DH)
            new_rows = lambda c0: _pad_rows(zs3[:, :, c0:c0 + kvw].reshape(bs, T8 * NSA_ROW, NSA_DH), PAGE_SIZE)
            kvc = _cmp_means(page_table, cache_cmp, o)
            o_cmp, sel = _cmp_select_decode(qg, kvc, zs3[:, :, c_kc:c_kc + kvw], cmp_tile, past_len, ts)
            o_slc = _slc_decode(page_table, qg, sel, expand_dec, new_rows(c_ks), slc_tile, nsa_new_tile, cache_slc, o)
            o_nsa_s = _win_combine_decode(qg, state_win_kv[o].reshape(bs, wbuf * NSA_ROW, NSA_DH), new_rows(c_kw),
                                          win_tile, zs, o_cmp, o_slc)
            prev = jnp.pad(state_conv[o], ((0, 0), (8 - (CONV_WIDTH - 1), 0), (0, 0)))
            o_conv_s, tail_s = _short_conv(zs, prev, conv_w[o], bs, T8, T8)
            xs, w_out_b = _out_proj(xs, o_nsa_s, o_conv_s, odd_w_out, ms, cfg["tn_out"], w_layer=o)
            xp = _out_proj(xp, o_nsa, o_conv, w_out_b, tm_p, cfg["tn_out"])
            kv5 = lambda a, n: a.reshape(a.shape[0], n, 2, NSA_KV, NSA_DH)
            outs["cmp_s"].append(kv5(zs3[:, :ts, c_kc:c_kc + kvw], ts))
            outs["slc_s"].append(kv5(zs3[:, :ts, c_ks:c_ks + kvw], ts))
            n_win = min(WINDOW, past_len + ts)
            outs["win_s"].append(jnp.concatenate([state_win_kv[o], kv5(zs3[:, :ts, c_kw:c_kw + kvw], ts)],
                                                 axis=1)[:, -n_win:])
            outs["conv_p"].append(tail[:, 8 - (CONV_WIDTH - 1):])
            outs["conv_s"].append(tail_s[:, ts - (CONV_WIDTH - 1):ts])
        last = layer == depth - 1
        xs, w1b, w2b = _mlp(xs, norm_mlp[layer], mlp_w1, mlp_w2, norm_final, ms, cfg["tf_cast"], last, layer=layer)
        xp = _mlp(xp, norm_mlp[layer], w1b, w2b, norm_final, min(cfg["tm_mlp"], mp), cfg["tf"], last)

    st = {k: jnp.stack(v) for k, v in outs.items()}
    st["diff_p"] = diff_kv_p[0].reshape(n_even, bp, seq, 2, N_HEADS, DA_V)
    nsa6 = lambda a: a.reshape(n_odd, bp, seq, 2, NSA_KV, NSA_DH)
    st["cmp_p"], st["slc_p"] = nsa6(nsa_kv_p[0]), nsa6(nsa_kv_p[1])
    st["win_p"] = nsa6(nsa_kv_p[2])[:, :, seq - min(WINDOW, seq):]
    return (xp.reshape(bp, seq, d), xs.reshape(bs, T8, d)[:, :ts],
            st["diff_p"], st["diff_s"], st["pool_p"], st["pool_s"], st["cmp_p"], st["cmp_s"],
            st["slc_p"], st["slc_s"], st["win_p"], st["win_s"], st["conv_p"], st["conv_s"])
```

```python
import functools
import math

import jax
import jax.numpy as jnp
import numpy as np
from jax import lax
from jax.experimental import pallas as pl
from jax.experimental.pallas import tpu as pltpu

F32 = jnp.float32
BF16 = jnp.bfloat16

D_MODEL = 2048
N_HEADS = 8
MIX_WIDTH = D_MODEL // 2
DA_QK = 64
DA_V = 128
POOL_WINDOWS = (2, 4, 8, 16)
POOL_GROUP = MIX_WIDTH // 4
POOL_STATE = 15
NSA_DH = 128
NSA_KV = 2
NSA_HPG = 4
CMP_BLOCK = 64
SLC_BLOCK = 64
SLC_TOPK = 16
WINDOW = 512
CONV_WIDTH = 3
NUM_BUCKETS = 32
MAX_DISTANCE = 128
PAGE_SIZE = 128
EPS = 1e-6
NEG = -1e30
MASK_BIG = 2.0 ** 100
ODD_SIZES = (1024, 512, 512, 512, 24, 1024, 1024, 1024)

LANES = 128
VMEM_LIMIT = 56 * 1024 * 1024
TQ = 256
TK = 256


def _cparams(sem):
    return pltpu.CompilerParams(dimension_semantics=sem, vmem_limit_bytes=VMEM_LIMIT)


def _bucket_np(dist):
    n = np.maximum(dist, 0)
    max_exact = NUM_BUCKETS // 2
    nf = np.maximum(n, 1).astype(np.float32)
    large = max_exact + (np.log(nf / np.float32(max_exact)) / np.float32(math.log(MAX_DISTANCE / max_exact))
                         * np.float32(NUM_BUCKETS - max_exact)).astype(np.int32)
    large = np.minimum(large, NUM_BUCKETS - 1)
    return np.where(n < max_exact, n, large).astype(np.int32)


def _bucket_or_masked(dist, valid):
    return np.where(valid, _bucket_np(dist), -1).astype(np.int32)


LOG2E = math.log2(math.e)


def _bias_kernel(table_ref, bucket_ref, o_ref, *, unit):
    h = pl.program_id(0)
    bt = bucket_ref[...]
    far = table_ref[NUM_BUCKETS - 1, h]
    acc = jnp.zeros(bt.shape, F32)
    for b in range(NUM_BUCKETS - 1):
        acc = jnp.where(bt == b, (table_ref[b, h] - far) * unit, acc)
    o_ref[...] = jnp.where(bt < 0, NEG, acc)


def _bias_tiles(table, buckets, unit=1.0):
    shp = buckets.shape
    flat = jnp.asarray(buckets.reshape(-1, shp[-1]))
    rows = flat.shape[0]
    out = pl.pallas_call(
        functools.partial(_bias_kernel, unit=unit),
        grid=(N_HEADS,),
        in_specs=[pl.BlockSpec(memory_space=pltpu.SMEM),
                  pl.BlockSpec((rows, shp[-1]), lambda h: (0, 0))],
        out_specs=pl.BlockSpec((None, rows, shp[-1]), lambda h: (h, 0, 0)),
        out_shape=jax.ShapeDtypeStruct((N_HEADS, rows, shp[-1]), F32),
        compiler_params=_cparams(("arbitrary",)),
        name="bias_tiles",
    )(table, flat)
    return out.reshape((N_HEADS,) + shp)


def _rms_rows(x, g):
    y = x * lax.rsqrt(jnp.mean(x * x, axis=-1, keepdims=True) + EPS)
    return y * g


def _norm_matmul_kernel(x_ref, g_ref, w_ref, *rest, taps, n_alias, emit):
    o_ref = rest[n_alias]
    tap_refs = rest[n_alias + 1:n_alias + 1 + len(taps)]
    h_scr = rest[-1]
    j = pl.program_id(1)
    tm, tn = o_ref.shape

    @pl.when(j == 0)
    def _():
        h_scr[...] = _rms_rows(x_ref[...], g_ref[...]).astype(BF16)

    wb = w_ref[...].astype(BF16)
    if emit:
        rest[n_alias + 1 + len(taps)][...] = wb
    acc = jnp.dot(h_scr[...], wb, preferred_element_type=F32)
    o_ref[...] = acc

    def whole_blocks(t_ref, col0, width):
        @pl.when((j >= col0 // tn) & (j < (col0 + width) // tn))
        def _():
            t_ref[...] = acc

    def token_rows(t_ref, col0, per_tok):
        loc = col0 % tn

        @pl.when(j == col0 // tn)
        def _():
            for c in range(per_tok):
                t_ref[pl.ds(c, tm, stride=per_tok), :] = acc[:, loc + c * LANES:loc + (c + 1) * LANES]

    for (col0, width, per_tok), t_ref in zip(taps, tap_refs):
        if per_tok == 1:
            whole_blocks(t_ref, col0, width)
        else:
            token_rows(t_ref, col0, per_tok)


def _norm_matmul(x, g, w, tm, tn, taps=(), stack=1, slot=0, prev=(), w_layer=None):
    m, d = x.shape
    n = w.shape[-1]
    emit = w_layer is not None
    assert m % tm == 0 and n % tn == 0 and (not prev or len(prev) == len(taps)) and (not emit or m == tm)
    out_specs = [pl.BlockSpec((tm, tn), lambda i, j: (i, j))]
    out_shape = [jax.ShapeDtypeStruct((m, n), F32)]
    for col0, width, per_tok in taps:
        if per_tok == 1:
            assert col0 % tn == 0 and width % tn == 0
            b0, nb = col0 // tn, width // tn
            out_specs.append(pl.BlockSpec((None, tm, tn), lambda i, j, b0=b0, nb=nb: (slot, i, jnp.clip(j - b0, 0, nb - 1))))
            out_shape.append(jax.ShapeDtypeStruct((stack, m, width), F32))
        else:
            assert width == per_tok * LANES and col0 // tn == (col0 + width - 1) // tn
            out_specs.append(pl.BlockSpec((None, tm * per_tok, LANES), lambda i, j: (slot, i, 0)))
            out_shape.append(jax.ShapeDtypeStruct((stack, m * per_tok, LANES), F32))
    n_alias = len(prev)
    if emit:
        w_spec = pl.BlockSpec((None, d, tn), lambda i, j: (w_layer, 0, j))
        out_specs.append(pl.BlockSpec((d, tn), lambda i, j: (0, j)))
        out_shape.append(jax.ShapeDtypeStruct((d, n), BF16))
    else:
        w_spec = pl.BlockSpec((d, tn), lambda i, j: (0, j))
    outs = pl.pallas_call(
        functools.partial(_norm_matmul_kernel, taps=tuple(taps), n_alias=n_alias, emit=emit),
        grid=(m // tm, n // tn),
        in_specs=[pl.BlockSpec((tm, d), lambda i, j: (i, 0)),
                  pl.BlockSpec((1, d), lambda i, j: (0, 0)),
                  w_spec]
        + [pl.BlockSpec(memory_space=pl.ANY)] * n_alias,
        out_specs=out_specs,
        out_shape=out_shape,
        input_output_aliases={3 + k: 1 + k for k in range(n_alias)},
        scratch_shapes=[pltpu.VMEM((tm, d), BF16)],
        compiler_params=_cparams(("parallel", "arbitrary")),
        name="norm_matmul",
    )(x, g.reshape(1, d), w, *prev)
    return outs[0], tuple(outs[1:])


def _out_proj_kernel(x_ref, a_ref, b_ref, wa_ref, wb_ref, o_ref, *rest, emit):
    wa = wa_ref[...].astype(BF16)
    wb = wb_ref[...].astype(BF16)
    if emit:
        ka = wa.shape[0]
        rest[0][0:ka, :] = wa
        rest[0][ka:2 * ka, :] = wb
    acc = jnp.dot(a_ref[...], wa, preferred_element_type=F32)
    acc = acc + jnp.dot(b_ref[...], wb, preferred_element_type=F32)
    o_ref[...] = x_ref[...] + acc


def _out_proj(x, a, b, w, tm, tn, w_layer=None):
    m, d = x.shape
    ka = a.shape[1]
    emit = w_layer is not None
    assert m % tm == 0 and d % tn == 0 and (not emit or m == tm)
    if emit:
        w_specs = [pl.BlockSpec((None, ka, tn), lambda i, j: (w_layer, 0, j)),
                   pl.BlockSpec((None, ka, tn), lambda i, j: (w_layer, 1, j))]
    else:
        w_specs = [pl.BlockSpec((ka, tn), lambda i, j: (0, j)), pl.BlockSpec((ka, tn), lambda i, j: (1, j))]
    out_specs = [pl.BlockSpec((tm, tn), lambda i, j: (i, j))]
    out_shape = [jax.ShapeDtypeStruct((m, d), F32)]
    if emit:
        out_specs.append(pl.BlockSpec((2 * ka, tn), lambda i, j: (0, j)))
        out_shape.append(jax.ShapeDtypeStruct((2 * ka, d), BF16))
    outs = pl.pallas_call(
        functools.partial(_out_proj_kernel, emit=emit),
        grid=(m // tm, d // tn),
        in_specs=[pl.BlockSpec((tm, tn), lambda i, j: (i, j)),
                  pl.BlockSpec((tm, ka), lambda i, j: (i, 0)),
                  pl.BlockSpec((tm, ka), lambda i, j: (i, 0))] + w_specs,
        out_specs=out_specs,
        out_shape=out_shape,
        compiler_params=_cparams(("parallel", "arbitrary")),
        name="out_proj",
    )(x, a, b, w, w)
    return outs if emit else outs[0]


def _mlp_kernel(x_ref, g_ref, w1_ref, w2_ref, gf_ref, o_ref, *rest, final_norm, emit):
    h_scr = rest[-1]
    f = pl.program_id(1)

    @pl.when(f == 0)
    def _():
        x = x_ref[...]
        h_scr[...] = _rms_rows(x, g_ref[...]).astype(BF16)
        o_ref[...] = x

    w1b = w1_ref[...].astype(BF16)
    w2b = w2_ref[...].astype(BF16)
    if emit:
        rest[0][...] = w1b
        rest[1][...] = w2b
    a = jnp.maximum(jnp.dot(h_scr[...], w1b, preferred_element_type=F32), 0.0)
    a = (a * a).astype(BF16)
    o_ref[...] += jnp.dot(a, w2b, preferred_element_type=F32)

    if final_norm:
        @pl.when(f == pl.num_programs(1) - 1)
        def _():
            o_ref[...] = _rms_rows(o_ref[...], gf_ref[...])


def _mlp(x, g, w1, w2, gf, tm, tf, final_norm, layer=None):
    m, d = x.shape
    ff = w1.shape[-1]
    emit = layer is not None
    assert m % tm == 0 and ff % tf == 0 and (not emit or m == tm)
    if emit:
        w_specs = [pl.BlockSpec((None, d, tf), lambda i, f: (layer, 0, f)),
                   pl.BlockSpec((None, tf, d), lambda i, f: (layer, f, 0))]
    else:
        w_specs = [pl.BlockSpec((d, tf), lambda i, f: (0, f)), pl.BlockSpec((tf, d), lambda i, f: (f, 0))]
    out_specs = [pl.BlockSpec((tm, d), lambda i, f: (i, 0))]
    out_shape = [jax.ShapeDtypeStruct((m, d), F32)]
    if emit:
        out_specs += [pl.BlockSpec((d, tf), lambda i, f: (0, f)), pl.BlockSpec((tf, d), lambda i, f: (f, 0))]
        out_shape += [jax.ShapeDtypeStruct((d, ff), BF16), jax.ShapeDtypeStruct((ff, d), BF16)]
    outs = pl.pallas_call(
        functools.partial(_mlp_kernel, final_norm=final_norm, emit=emit),
        grid=(m // tm, ff // tf),
        in_specs=[pl.BlockSpec((tm, d), lambda i, f: (i, 0)),
                  pl.BlockSpec((1, d), lambda i, f: (0, 0))] + w_specs
        + [pl.BlockSpec((1, d), lambda i, f: (0, 0))],
        out_specs=out_specs,
        out_shape=out_shape,
        scratch_shapes=[pltpu.VMEM((tm, d), BF16)],
        compiler_params=_cparams(("parallel", "arbitrary")),
        name="mlp",
    )(x, g.reshape(1, d), w1, w2, gf.reshape(1, d))
    return outs if emit else outs[0]


def _flash_init(m_scr, l_scr, acc_scr):
    m_scr[...] = jnp.full(m_scr.shape, NEG, F32)
    l_scr[...] = jnp.zeros(l_scr.shape, F32)
    acc_scr[...] = jnp.zeros(acc_scr.shape, F32)


def _flash_tile(qs, kt, vt, m_scr, l_scr, acc_scr, *, heads, bias=None, mask_add=None):
    s = lax.dot_general(qs, kt, (((1,), (1,)), ((), ())), preferred_element_type=F32)
    rows, tk = s.shape
    if bias is not None or mask_add is not None:
        s3 = s.reshape(heads, rows // heads, tk)
        if bias is not None:
            s3 = s3 + bias
        if mask_add is not None:
            s3 = s3 + mask_add[None]
        s = s3.reshape(rows, tk)
    m_prev = m_scr[...]
    m_next = jnp.maximum(m_prev, jnp.max(s, axis=1, keepdims=True))
    p = jnp.exp2(s - jnp.tile(m_next, (1, tk // LANES)))
    alpha = jnp.exp2(m_prev - m_next)
    l_scr[...] = alpha * l_scr[...] + jnp.sum(p, axis=1, keepdims=True)
    acc_scr[...] = alpha * acc_scr[...] + jnp.dot(p.astype(BF16), vt, preferred_element_type=F32)
    m_scr[...] = m_next


def _kv_tile(k_ref, v_ref, j, tk):
    off = pl.multiple_of(j * tk, tk)
    return k_ref[pl.ds(off, tk), :].astype(BF16), v_ref[pl.ds(off, tk), :].astype(BF16)


def _diff_lambda_in_kernel(lv_ref, lam_init):
    lv = lv_ref[...]
    a = jnp.sum(lv[0:1] * lv[1:2], axis=1, keepdims=True)
    b = jnp.sum(lv[2:3] * lv[3:4], axis=1, keepdims=True)
    return jnp.exp(a) - jnp.exp(b) + lam_init


DIFF_HEADS_PER_STEP = 4


def _diff_attn_kernel(lv_ref, q_ref, k_ref, v_ref, bias_ref, g_ref, o_ref, m_scr, l_scr, acc_scr, *, lam_init):
    qi = pl.program_id(2)
    tq = q_ref.shape[0]
    tk = bias_ref.shape[-1]
    hp = q_ref.shape[1] // DA_V
    lane = lax.broadcasted_iota(jnp.int32, (tq, DA_V), 1)
    qs = []
    for h in range(hp):
        q = q_ref[:, h * DA_V:(h + 1) * DA_V] * (DA_QK ** -0.5 * LOG2E)
        qs.append(jnp.concatenate([jnp.where(lane < DA_QK, q, 0.0), jnp.where(lane >= DA_QK, q, 0.0)],
                                  axis=0).astype(BF16))
    _flash_init(m_scr, l_scr, acc_scr)

    def tiles(j, which):
        off = pl.multiple_of(j * tk, tk)
        for h in range(hp):
            cols = slice(h * DA_V, (h + 1) * DA_V)
            kt = k_ref[pl.ds(off, tk), cols].astype(BF16)
            vt = v_ref[pl.ds(off, tk), cols].astype(BF16)
            _flash_tile(qs[h], kt, vt, m_scr.at[h], l_scr.at[h], acc_scr.at[h], heads=2,
                        bias=None if which is None else bias_ref[h, which][None])

    def far(j, c):
        tiles(j, None)
        return c

    lax.fori_loop(0, jnp.maximum(qi - 1, 0), far, 0)

    @pl.when(qi == 0)
    def _():
        tiles(qi, 0)

    @pl.when(qi >= 1)
    def _():
        tiles(qi - 1, 1)
        tiles(qi, 0)

    lam = _diff_lambda_in_kernel(lv_ref, lam_init)
    for h in range(hp):
        o = acc_scr[h] / l_scr[h]
        a = o[:tq] - lam * o[tq:]
        o_ref[:, h * DA_V:(h + 1) * DA_V] = (_rms_rows(a, g_ref[...]) * (1.0 - lam_init)).astype(BF16)


def _diff_attn_prompt(z, lam_vec, subln_g, bias, lam_init, batch, seq):
    nq = seq // TQ
    hp = DIFF_HEADS_PER_STEP
    groups = N_HEADS // hp
    wide = hp * DA_V
    return pl.pallas_call(
        functools.partial(_diff_attn_kernel, lam_init=lam_init),
        grid=(batch, groups, nq),
        in_specs=[pl.BlockSpec((4, DA_QK), lambda b, h, i: (0, 0)),
                  pl.BlockSpec((TQ, wide), lambda b, h, i: (b * nq + i, h)),
                  pl.BlockSpec((seq, wide), lambda b, h, i: (b, groups + h)),
                  pl.BlockSpec((seq, wide), lambda b, h, i: (b, 2 * groups + h)),
                  pl.BlockSpec((hp, bias.shape[1], TQ, TK), lambda b, h, i: (h, 0, 0, 0)),
                  pl.BlockSpec((1, DA_V), lambda b, h, i: (0, 0))],
        out_specs=pl.BlockSpec((TQ, wide), lambda b, h, i: (b * nq + i, h)),
        out_shape=jax.ShapeDtypeStruct((batch * seq, N_HEADS * DA_V), BF16),
        scratch_shapes=[pltpu.VMEM((hp, 2 * TQ, LANES), F32)] * 3,
        compiler_params=_cparams(("parallel", "parallel", "arbitrary")),
        name="diff_attn_prompt",
    )(lam_vec, z, z, z, bias, subln_g.reshape(1, DA_V))


def _toeplitz_buckets(tq, tk):
    i = np.arange(tq)[:, None]
    j = np.arange(tk)[None, :]
    diag = _bucket_or_masked(i - j, i >= j)
    sub = _bucket_or_masked(tk + i - j, np.ones((tq, tk), bool))
    return np.stack([diag, sub])


HALO = 16


def _pool_kernel(u_ref, halo_ref, prev_ref, w_ref, scale_ref, o_ref, ext_scr, *, pos0):
    i = pl.program_id(1)
    tm = u_ref.shape[0]
    ext_scr[0:HALO, :] = jnp.where(i == 0, prev_ref[...], halo_ref[...])
    ext_scr[HALO:HALO + tm, :] = u_ref[...]
    pos = pos0 + i * tm + lax.broadcasted_iota(jnp.int32, (tm, 1), 0)
    outs = []
    for g, w in enumerate(POOL_WINDOWS):
        c0, c1 = g * POOL_GROUP, (g + 1) * POOL_GROUP
        x0 = ext_scr[HALO:HALO + tm, c0:c1]
        win = x0
        for k in range(1, w):
            win = win + ext_scr[HALO - k:HALO - k + tm, c0:c1]
        cnt = jnp.minimum(pos + 1, w).astype(F32)
        d = win / cnt - x0
        outs.append(jnp.dot(d.astype(BF16), w_ref[g], preferred_element_type=F32))
    o_ref[...] = (jnp.concatenate(outs, axis=-1) * scale_ref[...]).astype(BF16)


def _pool_mix(z, prev, w_pool, scale, pos0, batch, seq, tm):
    nb = seq // tm
    ucol = 3
    return pl.pallas_call(
        functools.partial(_pool_kernel, pos0=pos0),
        grid=(batch, nb),
        in_specs=[pl.BlockSpec((tm, MIX_WIDTH), lambda b, i: (b * nb + i, ucol)),
                  pl.BlockSpec((HALO, MIX_WIDTH),
                               lambda b, i: (jnp.maximum((b * nb + i) * (tm // HALO) - 1, 0), ucol)),
                  pl.BlockSpec((None, HALO, MIX_WIDTH), lambda b, i: (b, 0, 0)),
                  pl.BlockSpec((4, POOL_GROUP, POOL_GROUP), lambda b, i: (0, 0, 0)),
                  pl.BlockSpec((1, MIX_WIDTH), lambda b, i: (0, 0))],
        out_specs=pl.BlockSpec((tm, MIX_WIDTH), lambda b, i: (b * nb + i, 0)),
        out_shape=jax.ShapeDtypeStruct((batch * seq, MIX_WIDTH), BF16),
        scratch_shapes=[pltpu.VMEM((HALO + tm, MIX_WIDTH), F32)],
        compiler_params=_cparams(("parallel", "arbitrary")),
        name="pool_mix",
    )(z, z, prev, w_pool, scale.reshape(1, MIX_WIDTH))


COL_CB, COL_CC, COL_CH, COL_KC, COL_KS, COL_KW, COL_GATE = 8, 16, 24, 32, 36, 40, 44


def _odd_weight_layout(w_in):
    offs = np.cumsum((0,) + ODD_SIZES)
    q, kc, ks_, kw, gates, cb, cc, ch = [w_in[:, offs[i]:offs[i + 1]] for i in range(8)]
    pad = jnp.zeros((w_in.shape[0], LANES - ODD_SIZES[4]), w_in.dtype)
    return jnp.concatenate([q, cb, cc, ch, kc, ks_, kw, gates, pad], axis=1)


def _select_blocks(score_t, q0, nb):
    nbp, tq = score_t.shape
    blk = lax.broadcasted_iota(jnp.int32, (nbp, tq), 0)
    cur = (q0 + lax.broadcasted_iota(jnp.int32, (nbp, tq), 1)) // SLC_BLOCK
    forced = (blk == 0) | (blk == cur) | (blk == cur - 1)
    future = (blk > cur) | (blk >= nb)
    s = jnp.where(forced, jnp.inf, jnp.where(future, -jnp.inf, score_t))
    cnt = jnp.zeros((nbp, tq), jnp.int32)
    for n in range(nb):
        row = s[n:n + 1, :]
        beats = (row > s) | ((row == s) & (blk > n))
        cnt = cnt + beats.astype(jnp.int32)
    return (cnt < min(SLC_TOPK, nb)).astype(F32)


def _nsa_prompt_kernel(q_ref, kc_ref, vc_ref, ks_ref, vs_ref, kw_ref, vw_ref, gate_ref, tb_ref, cb_ref, e_ref,
                       o_ref, kcm_scr, vcm_scr, m_scr, l_scr, acc_scr):
    g = pl.program_id(1)
    qi = pl.program_id(2)
    tq = q_ref.shape[0]
    tk = tb_ref.shape[-1]
    seq = kc_ref.shape[0]
    nb = seq // CMP_BLOCK
    nbp = -(-nb // 8) * 8
    scale = NSA_DH ** -0.5
    hp = NSA_HPG

    @pl.when(qi == 0)
    def _():
        kcm_scr[...] = jnp.zeros(kcm_scr.shape, BF16)
        vcm_scr[...] = jnp.zeros(vcm_scr.shape, BF16)
        kcm_scr[0:nb, :] = (jnp.sum(kc_ref[...].reshape(nb, CMP_BLOCK, NSA_DH), axis=1) / CMP_BLOCK).astype(BF16)
        vcm_scr[0:nb, :] = (jnp.sum(vc_ref[...].reshape(nb, CMP_BLOCK, NSA_DH), axis=1) / CMP_BLOCK).astype(BF16)

    qh = [q_ref[:, h * NSA_DH:(h + 1) * NSA_DH].astype(BF16) for h in range(hp)]
    qs = jnp.concatenate([(q_ref[:, h * NSA_DH:(h + 1) * NSA_DH] * (scale * LOG2E)).astype(BF16)
                          for h in range(hp)], axis=0)

    kcm = kcm_scr[...]
    vcm = vcm_scr[...]
    o_cmp = []
    p_grp_t = jnp.zeros((LANES, tq), F32)
    for h in range(hp):
        lt = lax.dot_general(kcm, qh[h], (((1,), (1,)), ((), ())), preferred_element_type=F32) * scale
        bt = cb_ref[h]
        lt = lt + bt
        p = jnp.exp(lt - jnp.max(lt, axis=0, keepdims=True))
        p = p / jnp.sum(p, axis=0, keepdims=True)
        p = jnp.where(bt > 0.5 * NEG, p, 0.0)
        p_grp_t = p_grp_t + p
        o_cmp.append(jnp.dot(p.T.astype(BF16), vcm, preferred_element_type=F32))

    sel_t = _select_blocks(p_grp_t[0:nbp, :], qi * tq, nb)
    if nbp < LANES:
        sel_t = jnp.concatenate([sel_t, jnp.zeros((LANES - nbp, tq), F32)], axis=0)
    sel_neg = ((sel_t.T - 1.0) * MASK_BIG).astype(BF16)

    def sel_mask(j):
        return jnp.dot(sel_neg, e_ref[j], preferred_element_type=F32)

    def slc_tile(j, which):
        kt, vt = _kv_tile(ks_ref, vs_ref, j, tk)
        _flash_tile(qs, kt, vt, m_scr.at[0], l_scr.at[0], acc_scr.at[0], heads=hp,
                    bias=None if which is None else tb_ref[:, which], mask_add=sel_mask(j))

    def win_tile(j, which):
        kt, vt = _kv_tile(kw_ref, vw_ref, j, tk)
        _flash_tile(qs, kt, vt, m_scr.at[1], l_scr.at[1], acc_scr.at[1], heads=hp, bias=tb_ref[:, which])

    _flash_init(m_scr, l_scr, acc_scr)
    n_far = jnp.maximum(qi - 1, 0)

    def far_pair(jj, c):
        slc_tile(2 * jj, None)
        slc_tile(2 * jj + 1, None)
        return c

    lax.fori_loop(0, n_far // 2, far_pair, 0)

    @pl.when(n_far % 2 == 1)
    def _():
        slc_tile(n_far - 1, None)

    def near_tiles(before):
        if before >= 2:
            win_tile(qi - 2, 2)
        if before >= 1:
            slc_tile(qi - 1, 1)
            win_tile(qi - 1, 1)
        slc_tile(qi, 0)
        win_tile(qi, 0)

    @pl.when(qi == 0)
    def _():
        near_tiles(0)

    @pl.when(qi == 1)
    def _():
        near_tiles(1)

    @pl.when(qi >= 2)
    def _():
        near_tiles(2)

    o_slc = acc_scr[0] / l_scr[0]
    o_win = acc_scr[1] / l_scr[1]

    sig = jax.nn.sigmoid(gate_ref[...])
    lane = lax.broadcasted_iota(jnp.int32, sig.shape, 1)

    def gate(branch, h):
        col = branch * N_HEADS + g * hp + h
        return jnp.sum(jnp.where(lane == col, sig, 0.0), axis=1, keepdims=True)

    for h in range(hp):
        rows = slice(h * tq, (h + 1) * tq)
        o = gate(0, h) * o_cmp[h] + gate(1, h) * o_slc[rows] + gate(2, h) * o_win[rows]
        o_ref[:, h * NSA_DH:(h + 1) * NSA_DH] = o.astype(BF16)


def _nsa_buckets(seq, tq, tk):
    i = np.arange(tq)[:, None]
    j = np.arange(tk)[None, :]
    toe = _toeplitz_buckets(tq, tk)
    assert WINDOW == 2 * tk and tq == tk
    win2 = np.where(i < j, NUM_BUCKETS - 1, -1).astype(np.int32)
    tiles = np.concatenate([toe, win2[None]])
    nb = seq // CMP_BLOCK
    blk_end = (np.arange(LANES)[:, None] + 1) * CMP_BLOCK - 1
    dist = np.arange(seq)[None, :] - blk_end
    cmp_t = _bucket_or_masked(dist, (dist >= 0) & (np.arange(LANES)[:, None] < nb))
    nk = seq // tk
    key_blk = (np.arange(nk)[:, None, None] * tk + np.arange(tk)[None, None, :]) // SLC_BLOCK
    expand = (key_blk == np.arange(LANES)[None, :, None]).astype(np.float32)
    return tiles, cmp_t, expand


def _nsa_prompt(z, tile_bias, cmp_bias_t, expand, batch, seq):
    nq = seq // TQ
    nk = seq // TK
    kv = lambda col: pl.BlockSpec((seq, NSA_DH), lambda b, g, i: (b, col + g))
    return pl.pallas_call(
        _nsa_prompt_kernel,
        grid=(batch, NSA_KV, nq),
        in_specs=[pl.BlockSpec((TQ, NSA_HPG * NSA_DH), lambda b, g, i: (b * nq + i, g)),
                  kv(COL_KC), kv(COL_KC + 2), kv(COL_KS), kv(COL_KS + 2), kv(COL_KW), kv(COL_KW + 2),
                  pl.BlockSpec((TQ, LANES), lambda b, g, i: (b * nq + i, COL_GATE)),
                  pl.BlockSpec((NSA_HPG, 3, TQ, TK), lambda b, g, i: (g, 0, 0, 0)),
                  pl.BlockSpec((NSA_HPG, LANES, TQ), lambda b, g, i: (g, 0, i)),
                  pl.BlockSpec((nk, LANES, TK), lambda b, g, i: (0, 0, 0))],
        out_specs=pl.BlockSpec((TQ, NSA_HPG * NSA_DH), lambda b, g, i: (b * nq + i, g)),
        out_shape=jax.ShapeDtypeStruct((batch * seq, MIX_WIDTH), BF16),
        scratch_shapes=[pltpu.VMEM((LANES, NSA_DH), BF16), pltpu.VMEM((LANES, NSA_DH), BF16)]
        + [pltpu.VMEM((2, NSA_HPG * TQ, LANES), F32)] * 3,
        compiler_params=_cparams(("parallel", "parallel", "arbitrary")),
        name="nsa_prompt",
    )(z, z, z, z, z, z, z, z, tile_bias, cmp_bias_t, expand)


def _conv_kernel(cb_ref, cc_ref, ch_ref, hc_ref, hh_ref, prev_ref, w_ref, o_ref, tail_ref, ext_scr):
    i = pl.program_id(1)
    tm = cb_ref.shape[0]
    e = cc_ref[...] * ch_ref[...]
    ext_scr[0:8, :] = jnp.where(i == 0, prev_ref[...], hc_ref[...] * hh_ref[...])
    ext_scr[8:8 + tm, :] = e
    w = w_ref[...]
    y = w[0:1] * ext_scr[6:6 + tm, :]
    y = y + w[1:2] * ext_scr[7:7 + tm, :]
    y = y + w[2:3] * e
    o_ref[...] = (cb_ref[...] * y).astype(BF16)

    @pl.when(i == pl.num_programs(1) - 1)
    def _():
        tail_ref[...] = e[tm - 8:tm, :]


def _short_conv(z, prev, w_conv, batch, seq, tm):
    nb = seq // tm
    cw = MIX_WIDTH // LANES
    blk = lambda c: pl.BlockSpec((tm, MIX_WIDTH), lambda b, i: (b * nb + i, c // cw))
    halo = lambda c: pl.BlockSpec((8, MIX_WIDTH), lambda b, i: (jnp.maximum((b * nb + i) * (tm // 8) - 1, 0), c // cw))
    return pl.pallas_call(
        _conv_kernel,
        grid=(batch, nb),
        in_specs=[blk(COL_CB), blk(COL_CC), blk(COL_CH), halo(COL_CC), halo(COL_CH),
                  pl.BlockSpec((None, 8, MIX_WIDTH), lambda b, i: (b, 0, 0)),
                  pl.BlockSpec((8, MIX_WIDTH), lambda b, i: (0, 0))],
        out_specs=[pl.BlockSpec((tm, MIX_WIDTH), lambda b, i: (b * nb + i, 0)),
                   pl.BlockSpec((None, 8, MIX_WIDTH), lambda b, i: (b, 0, 0))],
        out_shape=[jax.ShapeDtypeStruct((batch * seq, MIX_WIDTH), BF16),
                   jax.ShapeDtypeStruct((batch, 8, MIX_WIDTH), F32)],
        scratch_shapes=[pltpu.VMEM((8 + tm, MIX_WIDTH), F32)],
        compiler_params=_cparams(("parallel", "arbitrary")),
        name="short_conv",
    )(z, z, z, z, z, prev, jnp.pad(w_conv, ((0, 8 - CONV_WIDTH), (0, 0))))


T8 = 8
NEW_ROWS = 16
DIFF_PAGES_PER_STEP = 8
NSA_PAGES_PER_STEP = 16
NSA_ROW = 2 * NSA_KV
V_SHIFT = NSA_KV
NB_DEC = 288


def _nt(a, b):
    return lax.dot_general(a, b, (((1,), (1,)), ((), ())), preferred_element_type=F32)


def _softmax_update(st, v_tiles, width, m_ref, l_ref, acc_ref):
    m_prev = m_ref[...]
    m_next = jnp.maximum(m_prev, jnp.max(st, axis=1, keepdims=True))
    p = jnp.exp(st - jnp.tile(m_next, (1, st.shape[1] // LANES)))
    alpha = jnp.exp(m_prev - m_next)
    l_ref[...] = alpha * l_ref[...] + jnp.sum(p, axis=1, keepdims=True)
    pv = jnp.zeros(acc_ref.shape, F32)
    for k, vt in enumerate(v_tiles):
        pv = pv + jnp.dot(p[:, k * width:(k + 1) * width].astype(BF16), vt, preferred_element_type=F32)
    acc_ref[...] = alpha * acc_ref[...] + pv
    m_ref[...] = m_next


def _diff_decode_kernel(pt_ref, q_ref, new_ref, mask_ref, newmask_ref, lv_ref, g_ref, *rest, lam_init):
    pages = rest[:DIFF_PAGES_PER_STEP]
    o_ref, m_scr, l_scr, acc_scr = rest[DIFF_PAGES_PER_STEP:]
    _diff_decode_step(pl.program_id(1), pl.num_programs(1) - 1, q_ref, new_ref, mask_ref, newmask_ref, lv_ref,
                      g_ref, pages, o_ref, m_scr, l_scr, acc_scr, lam_init)


def _diff_decode_step(s, last, q_ref, new_ref, mask_ref, newmask_ref, lv_ref, g_ref, pages, o_ref,
                      m_scr, l_scr, acc_scr, lam_init, overlap=None):
    width = PAGE_SIZE * N_HEADS

    @pl.when(s == 0)
    def _():
        _flash_init(m_scr, l_scr, acc_scr)

    if overlap is not None:
        overlap()

    q = q_ref[...] * (DA_QK ** -0.5)
    lane = lax.broadcasted_iota(jnp.int32, q.shape, 1)
    qs = jnp.concatenate([jnp.where(lane < DA_QK, q, 0.0), jnp.where(lane >= DA_QK, q, 0.0)], axis=0).astype(BF16)
    upd = functools.partial(_softmax_update, m_ref=m_scr, l_ref=l_scr, acc_ref=acc_scr)

    is_last = s == last
    tiles, values = [], []
    for k, pg in enumerate(pages):
        st = _nt(qs, pg[:, 0].reshape(width, DA_V).astype(BF16))
        if k == len(pages) - 1:
            st = st + jnp.where(is_last, mask_ref[1], mask_ref[0])
        else:
            st = st + mask_ref[0]
        tiles.append(st)
        values.append(pg[:, 1].reshape(width, DA_V).astype(BF16))
    upd(jnp.concatenate(tiles, axis=1), values, width)

    @pl.when(is_last)
    def _():
        nw = NEW_ROWS * N_HEADS
        st = _nt(qs, new_ref[:, 0].reshape(nw, DA_V).astype(BF16)) + newmask_ref[...]
        upd(st, [new_ref[:, 1].reshape(nw, DA_V).astype(BF16)], nw)
        o = acc_scr[...] / l_scr[...]
        half = o.shape[0] // 2
        a = o[:half] - _diff_lambda_in_kernel(lv_ref, lam_init) * o[half:]
        o_ref[...] = _rms_rows(a, g_ref[...]) * (1.0 - lam_init)


def _diff_decode(page_table, q_ht, kv_new, masks, new_mask, lam_vec, subln_g, cache, layer, lam_init):
    batch, n_pages = page_table.shape
    steps = n_pages // DIFF_PAGES_PER_STEP
    rows = q_ht.shape[1]

    def page_spec(k):
        return pl.BlockSpec((None, None, PAGE_SIZE, 2, N_HEADS, DA_V),
                            lambda b, s, pt: (layer, pt[b, s * DIFF_PAGES_PER_STEP + k], 0, 0, 0, 0))

    grid_spec = pltpu.PrefetchScalarGridSpec(
        num_scalar_prefetch=1,
        grid=(batch, steps),
        in_specs=[pl.BlockSpec((None, rows, DA_V), lambda b, s, pt: (b, 0, 0)),
                  pl.BlockSpec((None, NEW_ROWS, 2, N_HEADS, DA_V), lambda b, s, pt: (b, 0, 0, 0, 0)),
                  pl.BlockSpec(masks.shape, lambda b, s, pt: (0, 0, 0)),
                  pl.BlockSpec(new_mask.shape, lambda b, s, pt: (0, 0)),
                  pl.BlockSpec((4, DA_QK), lambda b, s, pt: (0, 0)),
                  pl.BlockSpec((1, DA_V), lambda b, s, pt: (0, 0))]
        + [page_spec(k) for k in range(DIFF_PAGES_PER_STEP)],
        out_specs=pl.BlockSpec((None, rows, DA_V), lambda b, s, pt: (b, 0, 0)),
        scratch_shapes=[pltpu.VMEM((2 * rows, LANES), F32)] * 3,
    )
    return pl.pallas_call(
        functools.partial(_diff_decode_kernel, lam_init=lam_init),
        grid_spec=grid_spec,
        out_shape=jax.ShapeDtypeStruct((batch, rows, DA_V), F32),
        compiler_params=_cparams(("parallel", "arbitrary")),
        name="diff_decode",
    )(page_table, q_ht, kv_new, masks, new_mask, lam_vec, subln_g.reshape(1, DA_V),
      *([cache] * DIFF_PAGES_PER_STEP))


def _mlp_decode_kernel(pt_ref, x_ref, g_ref, w1_ref, w2_ref, gf_ref, q_ref, new_ref, mask_ref, newmask_ref,
                       lv_ref, sg_ref, *rest, final_norm, lam_init, pages_per_step, steps_per_seq):
    pages = rest[:pages_per_step]
    o_ref, odec_ref, h_scr, m_scr, l_scr, acc_scr = rest[pages_per_step:]
    t = pl.program_id(0) * pl.num_programs(1) + pl.program_id(1)
    mlp = functools.partial(_mlp_kernel, x_ref, g_ref, w1_ref, w2_ref, gf_ref, o_ref, h_scr, final_norm=final_norm,
                            emit=False)
    _diff_decode_step(t % steps_per_seq, steps_per_seq - 1, q_ref, new_ref, mask_ref, newmask_ref, lv_ref, sg_ref,
                      pages, odec_ref, m_scr, l_scr, acc_scr, lam_init, overlap=mlp)


def _decode_steps_per_seq(m, tm, ff, tf, page_table):
    batch, n_pages = page_table.shape
    total = (m // tm) * (ff // tf)
    if m % tm or ff % tf or total % batch or n_pages % (total // batch):
        return 0
    return total // batch


def _mlp_with_decode(x, g, w1, w2, gf, tm, tf, final_norm,
                     page_table, q_ht, kv_new, masks, new_mask, lam_vec, subln_g, cache, layer, lam_init):
    m, d = x.shape
    ff = w1.shape[1]
    n_f = ff // tf
    batch, n_pages = page_table.shape
    steps_per_seq = _decode_steps_per_seq(m, tm, ff, tf, page_table)
    assert steps_per_seq > 0
    pps = n_pages // steps_per_seq
    rows = q_ht.shape[1]
    seq_of = lambda i, f: (i * n_f + f) // steps_per_seq
    step_of = lambda i, f: (i * n_f + f) % steps_per_seq

    def page_spec(k):
        return pl.BlockSpec((None, None, PAGE_SIZE, 2, N_HEADS, DA_V),
                            lambda i, f, pt: (layer, pt[seq_of(i, f), step_of(i, f) * pps + k], 0, 0, 0, 0))

    grid_spec = pltpu.PrefetchScalarGridSpec(
        num_scalar_prefetch=1,
        grid=(m // tm, n_f),
        in_specs=[pl.BlockSpec((tm, d), lambda i, f, pt: (i, 0)),
                  pl.BlockSpec((1, d), lambda i, f, pt: (0, 0)),
                  pl.BlockSpec((d, tf), lambda i, f, pt: (0, f)),
                  pl.BlockSpec((tf, d), lambda i, f, pt: (f, 0)),
                  pl.BlockSpec((1, d), lambda i, f, pt: (0, 0)),
                  pl.BlockSpec((None, rows, DA_V), lambda i, f, pt: (seq_of(i, f), 0, 0)),
                  pl.BlockSpec((None, NEW_ROWS, 2, N_HEADS, DA_V), lambda i, f, pt: (seq_of(i, f), 0, 0, 0, 0)),
                  pl.BlockSpec(masks.shape, lambda i, f, pt: (0, 0, 0)),
                  pl.BlockSpec(new_mask.shape, lambda i, f, pt: (0, 0)),
                  pl.BlockSpec((4, DA_QK), lambda i, f, pt: (0, 0)),
                  pl.BlockSpec((1, DA_V), lambda i, f, pt: (0, 0))]
        + [page_spec(k) for k in range(pps)],
        out_specs=[pl.BlockSpec((tm, d), lambda i, f, pt: (i, 0)),
                   pl.BlockSpec((None, rows, DA_V), lambda i, f, pt: (seq_of(i, f), 0, 0))],
        scratch_shapes=[pltpu.VMEM((tm, d), BF16)] + [pltpu.VMEM((2 * rows, LANES), F32)] * 3,
    )
    return pl.pallas_call(
        functools.partial(_mlp_decode_kernel, final_norm=final_norm, lam_init=lam_init, pages_per_step=pps,
                          steps_per_seq=steps_per_seq),
        grid_spec=grid_spec,
        out_shape=[jax.ShapeDtypeStruct((m, d), F32), jax.ShapeDtypeStruct((batch, rows, DA_V), F32)],
        compiler_params=_cparams(("arbitrary", "arbitrary")),
        name="mlp_with_decode",
    )(page_table, x, g.reshape(1, d), w1, w2, gf.reshape(1, d), q_ht, kv_new, masks, new_mask, lam_vec,
      subln_g.reshape(1, DA_V), *([cache] * pps))


def _nsa_page_specs(layer):
    def spec(k):
        return pl.BlockSpec((None, None, PAGE_SIZE * NSA_ROW, NSA_DH),
                            lambda b, s, pt: (layer, pt[b, s * NSA_PAGES_PER_STEP + k], 0, 0))
    return [spec(k) for k in range(NSA_PAGES_PER_STEP)]


def _cmp_means_kernel(pt_ref, *rest):
    pages = rest[:NSA_PAGES_PER_STEP]
    o_ref = rest[NSA_PAGES_PER_STEP]
    per_blk = CMP_BLOCK * NSA_ROW // 8
    low = lax.broadcasted_iota(jnp.int32, (8, NSA_DH), 0) < NSA_ROW
    out = []
    for pg in pages:
        x = pg[...].reshape(PAGE_SIZE * NSA_ROW // 8, 8, NSA_DH)
        t0 = jnp.sum(x[0:per_blk], axis=0)
        t1 = jnp.sum(x[per_blk:2 * per_blk], axis=0)
        t0 = t0 + pltpu.roll(t0, NSA_ROW, axis=0)
        t1 = t1 + pltpu.roll(t1, NSA_ROW, axis=0)
        out.append(jnp.where(low, t0, t1) / CMP_BLOCK)
    o_ref[...] = jnp.concatenate(out, axis=0)


def _cmp_means(page_table, cache, layer):
    batch, n_pages = page_table.shape
    steps = n_pages // NSA_PAGES_PER_STEP
    per_step = NSA_PAGES_PER_STEP * 8
    grid_spec = pltpu.PrefetchScalarGridSpec(
        num_scalar_prefetch=1,
        grid=(batch, steps),
        in_specs=_nsa_page_specs(layer),
        out_specs=pl.BlockSpec((None, per_step, NSA_DH), lambda b, s, pt: (b, s, 0)),
    )
    return pl.pallas_call(
        _cmp_means_kernel,
        grid_spec=grid_spec,
        out_shape=jax.ShapeDtypeStruct((batch, steps * per_step, NSA_DH), F32),
        compiler_params=_cparams(("parallel", "arbitrary")),
        name="cmp_means",
    )(page_table, *([cache] * NSA_PAGES_PER_STEP))


def _cmp_select_decode_kernel(q_ref, kvc_ref, new_ref, tile_ref, ocmp_ref, sel_ref, k_scr, *, past_len, n_new):
    n_past = kvc_ref.shape[0]
    width = NB_DEC * NSA_ROW
    scale = NSA_DH ** -0.5
    k_scr[...] = jnp.zeros(k_scr.shape, F32)
    k_scr[0:n_past, :] = kvc_ref[...]
    real = lax.broadcasted_iota(jnp.int32, new_ref.shape, 0) < n_new
    tot = jnp.sum(jnp.where(real, new_ref[...], 0.0), axis=0, keepdims=True) / CMP_BLOCK
    new4 = jnp.concatenate([tot[:, c * NSA_DH:(c + 1) * NSA_DH] for c in range(NSA_ROW)]
                           + [jnp.zeros((8 - NSA_ROW, NSA_DH), F32)], axis=0)
    k_scr[n_past:n_past + 8, :] = new4

    q = q_ref[...].astype(BF16)
    tile = tile_ref[...]
    lg = _nt(q, k_scr[0:width, :].astype(BF16)) * scale + tile
    p = jnp.exp(lg - jnp.max(lg, axis=1, keepdims=True))
    p = p / jnp.sum(p, axis=1, keepdims=True)
    p = jnp.where(tile > 0.5 * NEG, p, 0.0)
    ocmp_ref[...] = jnp.dot(p.astype(BF16), k_scr[V_SHIFT:V_SHIFT + width, :].astype(BF16),
                            preferred_element_type=F32)

    lane = lax.broadcasted_iota(jnp.int32, (T8, width), 1)
    blk = lane // NSA_ROW
    cur = (past_len + lax.broadcasted_iota(jnp.int32, (T8, width), 0)) // SLC_BLOCK
    rows_g = NSA_HPG * T8
    for g in range(NSA_KV):
        mine = (lane % NSA_ROW) == g
        score = sum(p[g * rows_g + h * T8:g * rows_g + (h + 1) * T8] for h in range(NSA_HPG))
        forced = mine & ((blk == 0) | (blk == cur) | (blk == cur - 1))
        dead = (blk > cur) | jnp.logical_not(mine)
        sc = jnp.where(forced, jnp.inf, jnp.where(dead, -jnp.inf, score))
        taken = jnp.zeros((T8, width), jnp.bool_)
        for _ in range(SLC_TOPK):
            sm = jnp.where(taken, -jnp.inf, sc)
            cand = (sm == jnp.max(sm, axis=1, keepdims=True)) & jnp.logical_not(taken)
            idx = jnp.min(jnp.where(cand, lane, width), axis=1, keepdims=True)
            taken = taken | (lane == idx)
        sel_ref[g * T8:(g + 1) * T8, :] = taken.astype(F32)


def _cmp_select_decode(qg, kvc, kv_new, tile, past_len, n_new):
    batch, rows, _ = qg.shape
    width = NB_DEC * NSA_ROW
    n_past = kvc.shape[1]
    return pl.pallas_call(
        functools.partial(_cmp_select_decode_kernel, past_len=past_len, n_new=n_new),
        grid=(batch,),
        in_specs=[pl.BlockSpec((None, rows, NSA_DH), lambda b: (b, 0, 0)),
                  pl.BlockSpec((None, n_past, NSA_DH), lambda b: (b, 0, 0)),
                  pl.BlockSpec((None, T8, NSA_ROW * NSA_DH), lambda b: (b, 0, 0)),
                  pl.BlockSpec((rows, width), lambda b: (0, 0))],
        out_specs=[pl.BlockSpec((None, rows, NSA_DH), lambda b: (b, 0, 0)),
                   pl.BlockSpec((None, NSA_KV * T8, width), lambda b: (b, 0, 0))],
        out_shape=[jax.ShapeDtypeStruct((batch, rows, NSA_DH), F32),
                   jax.ShapeDtypeStruct((batch, NSA_KV * T8, width), F32)],
        scratch_shapes=[pltpu.VMEM((width + 8, NSA_DH), F32)],
        compiler_params=_cparams(("parallel",)),
        name="cmp_select_decode",
    )(qg, kvc, kv_new, tile)


def _masked_scores(st, sel, expand):
    rows, n = st.shape
    add = (jnp.dot(sel.astype(BF16), expand, preferred_element_type=F32) - 1.0) * (-NEG)
    st4 = st.reshape(NSA_KV, NSA_HPG, T8, n) + add.reshape(NSA_KV, 1, T8, n)
    return st4.reshape(rows, n)


def _slc_decode_kernel(pt_ref, q_ref, sel_ref, selnew_ref, e_ref, new_ref, tile_ref, newtile_ref, *rest):
    pages = rest[:NSA_PAGES_PER_STEP]
    o_ref, m_scr, l_scr, acc_scr = rest[NSA_PAGES_PER_STEP:]
    s = pl.program_id(1)
    last = pl.num_programs(1) - 1
    scale = NSA_DH ** -0.5
    width = PAGE_SIZE * NSA_ROW

    @pl.when(s == 0)
    def _():
        _flash_init(m_scr, l_scr, acc_scr)

    q = q_ref[...].astype(BF16)
    upd = functools.partial(_softmax_update, m_ref=m_scr, l_ref=l_scr, acc_ref=acc_scr)
    is_last = (s == last).astype(F32)
    tiles, values = [], []
    for k, pg in enumerate(pages):
        x = pg[...]
        st = _nt(q, x.astype(BF16)) * scale
        if k == NSA_PAGES_PER_STEP - 1:
            st = st + is_last * tile_ref[...]
        tiles.append(st)
        values.append(pltpu.roll(x, width - V_SHIFT, axis=0).astype(BF16))
    upd(_masked_scores(jnp.concatenate(tiles, axis=1), sel_ref[...], e_ref[...]), values, width)

    @pl.when(s == last)
    def _():
        x = new_ref[...]
        n = x.shape[0]
        st = _nt(q, x.astype(BF16)) * scale + newtile_ref[...]
        upd(_masked_scores(st, selnew_ref[...], e_ref[:, 0:n]), [pltpu.roll(x, n - V_SHIFT, axis=0).astype(BF16)], n)
        o_ref[...] = acc_scr[...] / l_scr[...]


def _slc_decode(page_table, qg, sel, expand, kv_new, tile, new_tile, cache, layer):
    batch, n_pages = page_table.shape
    steps = n_pages // NSA_PAGES_PER_STEP
    rows = qg.shape[1]
    keys = NSA_PAGES_PER_STEP * PAGE_SIZE * NSA_ROW
    grid_spec = pltpu.PrefetchScalarGridSpec(
        num_scalar_prefetch=1,
        grid=(batch, steps),
        in_specs=[pl.BlockSpec((None, rows, NSA_DH), lambda b, s, pt: (b, 0, 0)),
                  pl.BlockSpec((None, NSA_KV * T8, LANES), lambda b, s, pt: (b, 0, s)),
                  pl.BlockSpec((None, NSA_KV * T8, LANES), lambda b, s, pt: (b, 0, steps)),
                  pl.BlockSpec((LANES, keys), lambda b, s, pt: (0, 0)),
                  pl.BlockSpec((None,) + kv_new.shape[1:], lambda b, s, pt: (b, 0, 0)),
                  pl.BlockSpec(tile.shape, lambda b, s, pt: (0, 0)),
                  pl.BlockSpec(new_tile.shape, lambda b, s, pt: (0, 0))]
        + _nsa_page_specs(layer),
        out_specs=pl.BlockSpec((None, rows, NSA_DH), lambda b, s, pt: (b, 0, 0)),
        scratch_shapes=[pltpu.VMEM((rows, LANES), F32)] * 3,
    )
    return pl.pallas_call(
        _slc_decode_kernel,
        grid_spec=grid_spec,
        out_shape=jax.ShapeDtypeStruct((batch, rows, NSA_DH), F32),
        compiler_params=_cparams(("parallel", "arbitrary")),
        name="slc_decode",
    )(page_table, qg, sel, sel, expand, kv_new, tile, new_tile, *([cache] * NSA_PAGES_PER_STEP))


def _win_combine_decode_kernel(q_ref, state_ref, new_ref, tile_ref, gate_ref, ocmp_ref, oslc_ref, o_ref):
    scale = NSA_DH ** -0.5
    q = q_ref[...].astype(BF16)
    xs, xn = state_ref[...], new_ref[...]
    ns, nn = xs.shape[0], xn.shape[0]
    st = jnp.concatenate([_nt(q, xs.astype(BF16)), _nt(q, xn.astype(BF16))], axis=1) * scale + tile_ref[...]
    p = jnp.exp(st - jnp.max(st, axis=1, keepdims=True))
    p = (p / jnp.sum(p, axis=1, keepdims=True)).astype(BF16)
    o_win = (jnp.dot(p[:, 0:ns], pltpu.roll(xs, ns - V_SHIFT, axis=0).astype(BF16), preferred_element_type=F32)
             + jnp.dot(p[:, ns:], pltpu.roll(xn, nn - V_SHIFT, axis=0).astype(BF16), preferred_element_type=F32))
    sig = jax.nn.sigmoid(gate_ref[...])
    for g in range(NSA_KV):
        for h in range(NSA_HPG):
            head = g * NSA_HPG + h
            rows = slice(head * T8, (head + 1) * T8)
            gate = lambda branch: sig[:, branch * N_HEADS + head:branch * N_HEADS + head + 1]
            o = gate(0) * ocmp_ref[rows, :] + gate(1) * oslc_ref[rows, :] + gate(2) * o_win[rows]
            o_ref[:, head * NSA_DH:(head + 1) * NSA_DH] = o.astype(BF16)


def _win_combine_decode(qg, state_win, kv_new, tile, gates, o_cmp, o_slc):
    batch, rows, _ = qg.shape
    branch = pl.BlockSpec((None, rows, NSA_DH), lambda b: (b, 0, 0))
    return pl.pallas_call(
        _win_combine_decode_kernel,
        grid=(batch,),
        in_specs=[branch,
                  pl.BlockSpec((None,) + state_win.shape[1:], lambda b: (b, 0, 0)),
                  pl.BlockSpec((None,) + kv_new.shape[1:], lambda b: (b, 0, 0)),
                  pl.BlockSpec(tile.shape, lambda b: (0, 0)),
                  pl.BlockSpec((T8, LANES), lambda b: (b, COL_GATE)),
                  branch, branch],
        out_specs=pl.BlockSpec((T8, MIX_WIDTH), lambda b: (b, 0)),
        out_shape=jax.ShapeDtypeStruct((batch * T8, MIX_WIDTH), BF16),
        compiler_params=_cparams(("parallel",)),
        name="win_combine_decode",
    )(qg, state_win, kv_new, tile, gates, o_cmp, o_slc)


def _decode_buckets(past_len, wbuf):
    t = np.arange(T8)[:, None]
    dpos = np.repeat(np.arange(PAGE_SIZE), N_HEADS)[None, :]
    dnew = np.repeat(np.arange(NEW_ROWS), N_HEADS)[None, :]
    npos = np.repeat(np.arange(PAGE_SIZE), NSA_ROW)[None, :]
    nnew = np.repeat(np.arange(PAGE_SIZE // NSA_ROW), NSA_ROW)[None, :]
    nblk = np.repeat(np.arange(NB_DEC), NSA_ROW)[None, :]
    nwin = np.repeat(np.arange(wbuf), NSA_ROW)[None, :]
    d_cmp = past_len + t - ((nblk + 1) * CMP_BLOCK - 1)
    d_win = t + wbuf - nwin
    parts = [
        _bucket_or_masked(PAGE_SIZE + t - dpos, np.ones((T8, dpos.shape[1]), bool)),
        _bucket_or_masked(t - dnew, dnew <= t),
        _bucket_or_masked(PAGE_SIZE + t - npos, np.ones((T8, npos.shape[1]), bool)),
        _bucket_or_masked(t - nnew, nnew <= t),
        _bucket_or_masked(d_cmp, d_cmp >= 0),
        _bucket_or_masked(d_win, (d_win >= 0) & (d_win < WINDOW) & (past_len - wbuf + nwin >= 0)),
    ]
    return np.concatenate(parts, axis=1), np.cumsum([0] + [p.shape[1] for p in parts])


def _pad_rows(x, rows):
    return jnp.pad(x, ((0, 0), (0, rows - x.shape[1])) + ((0, 0),) * (x.ndim - 2))


def _tile_cfg():
    return dict(tm=1024, tn_even=512, tm_odd=512, tn_odd=1152, tn_out=1024, tm_mlp=512, tf=1024, tf_cast=512,
                tf_fused=512, tm_pool=256)


def kernel(x_prompt, x_sample, cache_diff_kv, cache_cmp_kv, cache_slc_kv, state_win_kv, state_pool, state_conv,
           page_table, rel_bias, norm_mix, norm_mlp, norm_final, even_w_in, even_w_out, diff_lambda, diff_subln,
           pool_w, pool_scale, odd_w_in, odd_w_out, conv_w, mlp_w1, mlp_w2):
    bp, seq, d = x_prompt.shape
    bs, ts, _ = x_sample.shape
    depth = norm_mix.shape[0]
    n_pages = page_table.shape[1]
    past_len = n_pages * PAGE_SIZE
    n_phys = cache_diff_kv.shape[1]
    wbuf = state_win_kv.shape[2]
    n_past_blk = past_len // SLC_BLOCK
    assert seq % TQ == 0 and ts <= T8 and n_pages % NSA_PAGES_PER_STEP == 0 and wbuf == WINDOW
    assert (past_len + ts - 1) // SLC_BLOCK == n_past_blk < NB_DEC
    cfg = _tile_cfg()

    w_in_o = [_odd_weight_layout(odd_w_in[o].astype(BF16)) for o in range(odd_w_in.shape[0])]
    pool_wb = pool_w.astype(BF16)

    tiles, cmp_t, expand = _nsa_buckets(seq, TQ, TK)
    tile_bias = _bias_tiles(rel_bias, tiles, LOG2E)
    cmp_bias_t = _bias_tiles(rel_bias, cmp_t)
    expand = jnp.asarray(expand, BF16)
    dec_buckets, off = _decode_buckets(past_len, wbuf)
    dec = _bias_tiles(rel_bias, dec_buckets)
    part = lambda i: dec[:, :, off[i]:off[i + 1]]

    def diff_rows(x):
        n = x.shape[-1]
        own = (np.arange(n) % N_HEADS)[None, None, :] == np.arange(N_HEADS)[:, None, None]
        x = jnp.where(own, x, NEG)
        return jnp.broadcast_to(x[None], (2,) + x.shape).reshape(2 * N_HEADS * T8, n)

    def nsa_rows(x):
        n = x.shape[-1]
        own = (np.arange(n) % NSA_ROW)[None, None, :] == (np.arange(N_HEADS) // NSA_HPG)[:, None, None]
        return jnp.where(own, x, NEG).reshape(N_HEADS * T8, n)

    diff_masks = jnp.stack([diff_rows(jnp.zeros_like(part(0))), diff_rows(part(0))])
    diff_new_mask = diff_rows(part(1))
    slc_tile = nsa_rows(part(2))
    nsa_new_tile = nsa_rows(part(3))
    cmp_tile = nsa_rows(part(4))
    win_tile = jnp.concatenate([nsa_rows(part(5)), nsa_new_tile], axis=1)
    key = np.arange(NSA_PAGES_PER_STEP * PAGE_SIZE * NSA_ROW)
    sel_lane = (key // (NSA_ROW * SLC_BLOCK)) * NSA_ROW + key % NSA_ROW
    is_key_row = (key % NSA_ROW) < NSA_KV
    expand_dec = jnp.asarray((np.arange(LANES)[:, None] == sel_lane[None, :]) & is_key_row[None, :], BF16)

    cache_cmp = cache_cmp_kv.reshape(cache_cmp_kv.shape[0], n_phys, PAGE_SIZE * NSA_ROW, NSA_DH)
    cache_slc = cache_slc_kv.reshape(cache_slc_kv.shape[0], n_phys, PAGE_SIZE * NSA_ROW, NSA_DH)

    mp, ms = bp * seq, bs * T8
    xp = x_prompt.reshape(mp, d)
    xs = _pad_rows(x_sample, T8).reshape(ms, d)
    tm_p = min(cfg["tm"], mp)
    outs = {k: [] for k in ("diff_s", "pool_p", "pool_s", "cmp_s", "slc_s", "win_s", "conv_p", "conv_s")}
    kvw = NSA_ROW * NSA_DH
    n_even, n_odd = (depth + 1) // 2, depth // 2
    even_taps = ((N_HEADS * DA_V, 2 * N_HEADS * DA_V, 1),)
    odd_taps = tuple((c * LANES, kvw, NSA_ROW) for c in (COL_KC, COL_KS, COL_KW))
    diff_kv_p, nsa_kv_p = (), ()

    lam_init_of = lambda lyr: 0.8 - 0.6 * math.exp(-0.3 * lyr)

    def sample_even_projection(lyr):
        zs_, (w_b,) = _norm_matmul(xs, norm_mix[lyr], even_w_in, ms, cfg["tn_even"], w_layer=lyr // 2)
        z3 = zs_.reshape(bs, T8, -1)
        q_ = z3[:, :, 0:1024].reshape(bs, T8, N_HEADS, DA_V).transpose(0, 2, 1, 3).reshape(bs, N_HEADS * T8, DA_V)
        kv_ = _pad_rows(z3[:, :, 1024:3072].reshape(bs, T8, 2, N_HEADS, DA_V), NEW_ROWS)
        return zs_, w_b, q_, kv_

    ahead = None
    for layer in range(depth):
        if layer % 2 == 0:
            e = layer // 2
            lam_init = lam_init_of(layer)
            if ahead is None:
                zs, w_in_b, q_ht, kv_new = sample_even_projection(layer)
                o_dec = _diff_decode(page_table, q_ht, kv_new, diff_masks, diff_new_mask, diff_lambda[e],
                                     diff_subln[e], cache_diff_kv, e, lam_init)
            else:
                zs, w_in_b, o_dec = ahead
            zp, diff_kv_p = _norm_matmul(xp, norm_mix[layer], w_in_b, tm_p, cfg["tn_even"], even_taps, n_even, e,
                                         diff_kv_p)
            zp3 = zp.reshape(bp, seq, -1)
            zs3 = zs.reshape(bs, T8, -1)
            o_attn = _diff_attn_prompt(zp, diff_lambda[e], diff_subln[e], tile_bias, lam_init, bp, seq)
            o_pool = _pool_mix(zp, jnp.zeros((bp, HALO, MIX_WIDTH), F32), pool_wb[e], pool_scale[e], 0, bp, seq,
                               cfg["tm_pool"])
            o_attn_s = o_dec.reshape(bs, N_HEADS, T8, DA_V).transpose(0, 2, 1, 3).reshape(ms, N_HEADS * DA_V)
            prev = jnp.pad(state_pool[e], ((0, 0), (HALO - POOL_STATE, 0), (0, 0)))
            o_pool_s = _pool_mix(zs, prev, pool_wb[e], pool_scale[e], past_len, bs, T8, T8)
            xs, w_out_b = _out_proj(xs, o_attn_s.astype(BF16), o_pool_s, even_w_out, ms, cfg["tn_out"], w_layer=e)
            xp = _out_proj(xp, o_attn, o_pool, w_out_b, tm_p, cfg["tn_out"])
            outs["diff_s"].append(zs3[:, :ts, 1024:3072].reshape(bs, ts, 2, N_HEADS, DA_V))
            outs["pool_p"].append(zp3[:, seq - POOL_STATE:, 3072:])
            outs["pool_s"].append(jnp.concatenate([state_pool[e], zs3[:, :ts, 3072:]], axis=1)[:, -POOL_STATE:])
        else:
            o = layer // 2
            zp, nsa_kv_p = _norm_matmul(xp, norm_mix[layer], w_in_o[o], min(cfg["tm_odd"], mp), cfg["tn_odd"],
                                        odd_taps, n_odd, o, nsa_kv_p)
            zs, _ = _norm_matmul(xs, norm_mix[layer], w_in_o[o], ms, cfg["tn_odd"])
            zp3 = zp.reshape(bp, seq, -1)
            zs3 = zs.reshape(bs, T8, -1)
            c_kc, c_ks, c_kw = COL_KC * LANES, COL_KS * LANES, COL_KW * LANES
            o_nsa = _nsa_prompt(zp, tile_bias, cmp_bias_t, expand, bp, seq)
            o_conv, tail = _short_conv(zp, jnp.zeros((bp, 8, MIX_WIDTH), F32), conv_w[o], bp, seq, cfg["tm_pool"])
            qg = zs3[:, :, 0:1024].reshape(bs, T8, N_HEADS, NSA_DH).transpose(0, 2, 1, 3).reshape(
                bs, N_HEADS * T8, NSA_DH)
            new_rows = lambda c0: _pad_rows(zs3[:, :, c0:c0 + kvw].reshape(bs, T8 * NSA_ROW, NSA_DH), PAGE_SIZE)
            kvc = _cmp_means(page_table, cache_cmp, o)
            o_cmp, sel = _cmp_select_decode(qg, kvc, zs3[:, :, c_kc:c_kc + kvw], cmp_tile, past_len, ts)
            o_slc = _slc_decode(page_table, qg, sel, expand_dec, new_rows(c_ks), slc_tile, nsa_new_tile, cache_slc, o)
            o_nsa_s = _win_combine_decode(qg, state_win_kv[o].reshape(bs, wbuf * NSA_ROW, NSA_DH), new_rows(c_kw),
                                          win_tile, zs, o_cmp, o_slc)
            prev = jnp.pad(state_conv[o], ((0, 0), (8 - (CONV_WIDTH - 1), 0), (0, 0)))
            o_conv_s, tail_s = _short_conv(zs, prev, conv_w[o], bs, T8, T8)
            xs, w_out_b = _out_proj(xs, o_nsa_s, o_conv_s, odd_w_out, ms, cfg["tn_out"], w_layer=o)
            xp = _out_proj(xp, o_nsa, o_conv, w_out_b, tm_p, cfg["tn_out"])
            kv5 = lambda a, n: a.reshape(a.shape[0], n, 2, NSA_KV, NSA_DH)
            outs["cmp_s"].append(kv5(zs3[:, :ts, c_kc:c_kc + kvw], ts))
            outs["slc_s"].append(kv5(zs3[:, :ts, c_ks:c_ks + kvw], ts))
            n_win = min(WINDOW, past_len + ts)
            outs["win_s"].append(jnp.concatenate([state_win_kv[o], kv5(zs3[:, :ts, c_kw:c_kw + kvw], ts)],
                                                 axis=1)[:, -n_win:])
            outs["conv_p"].append(tail[:, 8 - (CONV_WIDTH - 1):])
            outs["conv_s"].append(tail_s[:, ts - (CONV_WIDTH - 1):ts])
        last = layer == depth - 1
        xs, w1b, w2b = _mlp(xs, norm_mlp[layer], mlp_w1, mlp_w2, norm_final, ms, cfg["tf_cast"], last, layer=layer)
        tm_mlp = min(cfg["tm_mlp"], mp)
        nxt = layer + 1
        if nxt < depth and nxt % 2 == 0 and _decode_steps_per_seq(mp, tm_mlp, w1b.shape[1], cfg["tf_fused"], page_table):
            zs_n, w_in_n, q_ht, kv_new = sample_even_projection(nxt)
            xp, o_dec_n = _mlp_with_decode(xp, norm_mlp[layer], w1b, w2b, norm_final, tm_mlp, cfg["tf_fused"], last,
                                           page_table, q_ht, kv_new, diff_masks, diff_new_mask, diff_lambda[nxt // 2],
                                           diff_subln[nxt // 2], cache_diff_kv, nxt // 2, lam_init_of(nxt))
            ahead = (zs_n, w_in_n, o_dec_n)
        else:
            xp = _mlp(xp, norm_mlp[layer], w1b, w2b, norm_final, tm_mlp, cfg["tf"], last)
            ahead = None

    st = {k: jnp.stack(v) for k, v in outs.items()}
    st["diff_p"] = diff_kv_p[0].reshape(n_even, bp, seq, 2, N_HEADS, DA_V)
    nsa6 = lambda a: a.reshape(n_odd, bp, seq, 2, NSA_KV, NSA_DH)
    st["cmp_p"], st["slc_p"] = nsa6(nsa_kv_p[0]), nsa6(nsa_kv_p[1])
    st["win_p"] = nsa6(nsa_kv_p[2])[:, :, seq - min(WINDOW, seq):]
    return (xp.reshape(bp, seq, d), xs.reshape(bs, T8, d)[:, :ts],
            st["diff_p"], st["diff_s"], st["pool_p"], st["pool_s"], st["cmp_p"], st["cmp_s"],
            st["slc_p"], st["slc_s"], st["win_p"], st["win_s"], st["conv_p"], st["conv_s"])
```

```python
import functools
import math

import jax
import jax.numpy as jnp
import numpy as np
from jax import lax
from jax.experimental import pallas as pl
from jax.experimental.pallas import tpu as pltpu

F32 = jnp.float32
BF16 = jnp.bfloat16

D_MODEL = 2048
N_HEADS = 8
MIX_WIDTH = D_MODEL // 2
DA_QK = 64
DA_V = 128
POOL_WINDOWS = (2, 4, 8, 16)
POOL_GROUP = MIX_WIDTH // 4
POOL_STATE = 15
NSA_DH = 128
NSA_KV = 2
NSA_HPG = 4
CMP_BLOCK = 64
SLC_BLOCK = 64
SLC_TOPK = 16
WINDOW = 512
CONV_WIDTH = 3
NUM_BUCKETS = 32
MAX_DISTANCE = 128
PAGE_SIZE = 128
EPS = 1e-6
NEG = -1e30
MASK_BIG = 2.0 ** 100
ODD_SIZES = (1024, 512, 512, 512, 24, 1024, 1024, 1024)

LANES = 128
VMEM_LIMIT = 56 * 1024 * 1024
TQ = 256
TK = 256


def _cparams(sem):
    return pltpu.CompilerParams(dimension_semantics=sem, vmem_limit_bytes=VMEM_LIMIT)


def _bucket_np(dist):
    n = np.maximum(dist, 0)
    max_exact = NUM_BUCKETS // 2
    nf = np.maximum(n, 1).astype(np.float32)
    large = max_exact + (np.log(nf / np.float32(max_exact)) / np.float32(math.log(MAX_DISTANCE / max_exact))
                         * np.float32(NUM_BUCKETS - max_exact)).astype(np.int32)
    large = np.minimum(large, NUM_BUCKETS - 1)
    return np.where(n < max_exact, n, large).astype(np.int32)


def _bucket_or_masked(dist, valid):
    return np.where(valid, _bucket_np(dist), -1).astype(np.int32)


LOG2E = math.log2(math.e)


def _bias_kernel(table_ref, bucket_ref, o_ref, *, unit):
    h = pl.program_id(0)
    bt = bucket_ref[...]
    far = table_ref[NUM_BUCKETS - 1, h]
    acc = jnp.zeros(bt.shape, F32)
    for b in range(NUM_BUCKETS - 1):
        acc = jnp.where(bt == b, (table_ref[b, h] - far) * unit, acc)
    o_ref[...] = jnp.where(bt < 0, NEG, acc)


def _bias_tiles(table, buckets, unit=1.0):
    shp = buckets.shape
    flat = jnp.asarray(buckets.reshape(-1, shp[-1]))
    rows = flat.shape[0]
    out = pl.pallas_call(
        functools.partial(_bias_kernel, unit=unit),
        grid=(N_HEADS,),
        in_specs=[pl.BlockSpec(memory_space=pltpu.SMEM),
                  pl.BlockSpec((rows, shp[-1]), lambda h: (0, 0))],
        out_specs=pl.BlockSpec((None, rows, shp[-1]), lambda h: (h, 0, 0)),
        out_shape=jax.ShapeDtypeStruct((N_HEADS, rows, shp[-1]), F32),
        compiler_params=_cparams(("arbitrary",)),
        name="bias_tiles",
    )(table, flat)
    return out.reshape((N_HEADS,) + shp)


def _rms_rows(x, g):
    y = x * lax.rsqrt(jnp.mean(x * x, axis=-1, keepdims=True) + EPS)
    return y * g


def _norm_matmul_kernel(x_ref, g_ref, w_ref, *rest, taps, n_alias, emit):
    o_ref = rest[n_alias]
    tap_refs = rest[n_alias + 1:n_alias + 1 + len(taps)]
    h_scr = rest[-1]
    j = pl.program_id(1)
    tm, tn = o_ref.shape

    @pl.when(j == 0)
    def _():
        h_scr[...] = _rms_rows(x_ref[...], g_ref[...]).astype(BF16)

    wb = w_ref[...].astype(BF16)
    if emit:
        rest[n_alias + 1 + len(taps)][...] = wb
    acc = jnp.dot(h_scr[...], wb, preferred_element_type=F32)
    o_ref[...] = acc

    def whole_blocks(t_ref, col0, width):
        @pl.when((j >= col0 // tn) & (j < (col0 + width) // tn))
        def _():
            t_ref[...] = acc

    def token_rows(t_ref, col0, per_tok):
        loc = col0 % tn

        @pl.when(j == col0 // tn)
        def _():
            for c in range(per_tok):
                t_ref[pl.ds(c, tm, stride=per_tok), :] = acc[:, loc + c * LANES:loc + (c + 1) * LANES]

    for (col0, width, per_tok), t_ref in zip(taps, tap_refs):
        if per_tok == 1:
            whole_blocks(t_ref, col0, width)
        else:
            token_rows(t_ref, col0, per_tok)


def _norm_matmul(x, g, w, tm, tn, taps=(), stack=1, slot=0, prev=(), w_layer=None):
    m, d = x.shape
    n = w.shape[-1]
    emit = w_layer is not None
    assert m % tm == 0 and n % tn == 0 and (not prev or len(prev) == len(taps)) and (not emit or m == tm)
    out_specs = [pl.BlockSpec((tm, tn), lambda i, j: (i, j))]
    out_shape = [jax.ShapeDtypeStruct((m, n), F32)]
    for col0, width, per_tok in taps:
        if per_tok == 1:
            assert col0 % tn == 0 and width % tn == 0
            b0, nb = col0 // tn, width // tn
            out_specs.append(pl.BlockSpec((None, tm, tn), lambda i, j, b0=b0, nb=nb: (slot, i, jnp.clip(j - b0, 0, nb - 1))))
            out_shape.append(jax.ShapeDtypeStruct((stack, m, width), F32))
        else:
            assert width == per_tok * LANES and col0 // tn == (col0 + width - 1) // tn
            out_specs.append(pl.BlockSpec((None, tm * per_tok, LANES), lambda i, j: (slot, i, 0)))
            out_shape.append(jax.ShapeDtypeStruct((stack, m * per_tok, LANES), F32))
    n_alias = len(prev)
    if emit:
        w_spec = pl.BlockSpec((None, d, tn), lambda i, j: (w_layer, 0, j))
        out_specs.append(pl.BlockSpec((d, tn), lambda i, j: (0, j)))
        out_shape.append(jax.ShapeDtypeStruct((d, n), BF16))
    else:
        w_spec = pl.BlockSpec((d, tn), lambda i, j: (0, j))
    outs = pl.pallas_call(
        functools.partial(_norm_matmul_kernel, taps=tuple(taps), n_alias=n_alias, emit=emit),
        grid=(m // tm, n // tn),
        in_specs=[pl.BlockSpec((tm, d), lambda i, j: (i, 0)),
                  pl.BlockSpec((1, d), lambda i, j: (0, 0)),
                  w_spec]
        + [pl.BlockSpec(memory_space=pl.ANY)] * n_alias,
        out_specs=out_specs,
        out_shape=out_shape,
        input_output_aliases={3 + k: 1 + k for k in range(n_alias)},
        scratch_shapes=[pltpu.VMEM((tm, d), BF16)],
        compiler_params=_cparams(("parallel", "arbitrary")),
        name="norm_matmul",
    )(x, g.reshape(1, d), w, *prev)
    return outs[0], tuple(outs[1:])


def _out_proj_kernel(x_ref, a_ref, b_ref, wa_ref, wb_ref, o_ref, *rest, emit):
    wa = wa_ref[...].astype(BF16)
    wb = wb_ref[...].astype(BF16)
    if emit:
        ka = wa.shape[0]
        rest[0][0:ka, :] = wa
        rest[0][ka:2 * ka, :] = wb
    acc = jnp.dot(a_ref[...], wa, preferred_element_type=F32)
    acc = acc + jnp.dot(b_ref[...], wb, preferred_element_type=F32)
    o_ref[...] = x_ref[...] + acc


def _out_proj(x, a, b, w, tm, tn, w_layer=None):
    m, d = x.shape
    ka = a.shape[1]
    emit = w_layer is not None
    assert m % tm == 0 and d % tn == 0 and (not emit or m == tm)
    if emit:
        w_specs = [pl.BlockSpec((None, ka, tn), lambda i, j: (w_layer, 0, j)),
                   pl.BlockSpec((None, ka, tn), lambda i, j: (w_layer, 1, j))]
    else:
        w_specs = [pl.BlockSpec((ka, tn), lambda i, j: (0, j)), pl.BlockSpec((ka, tn), lambda i, j: (1, j))]
    out_specs = [pl.BlockSpec((tm, tn), lambda i, j: (i, j))]
    out_shape = [jax.ShapeDtypeStruct((m, d), F32)]
    if emit:
        out_specs.append(pl.BlockSpec((2 * ka, tn), lambda i, j: (0, j)))
        out_shape.append(jax.ShapeDtypeStruct((2 * ka, d), BF16))
    outs = pl.pallas_call(
        functools.partial(_out_proj_kernel, emit=emit),
        grid=(m // tm, d // tn),
        in_specs=[pl.BlockSpec((tm, tn), lambda i, j: (i, j)),
                  pl.BlockSpec((tm, ka), lambda i, j: (i, 0)),
                  pl.BlockSpec((tm, ka), lambda i, j: (i, 0))] + w_specs,
        out_specs=out_specs,
        out_shape=out_shape,
        compiler_params=_cparams(("parallel", "arbitrary")),
        name="out_proj",
    )(x, a, b, w, w)
    return outs if emit else outs[0]


def _mlp_kernel(x_ref, g_ref, w1_ref, w2_ref, gf_ref, o_ref, *rest, final_norm, emit):
    h_scr = rest[-1]
    f = pl.program_id(1)

    @pl.when(f == 0)
    def _():
        x = x_ref[...]
        h_scr[...] = _rms_rows(x, g_ref[...]).astype(BF16)
        o_ref[...] = x

    w1b = w1_ref[...].astype(BF16)
    w2b = w2_ref[...].astype(BF16)
    if emit:
        rest[0][...] = w1b
        rest[1][...] = w2b
    a = jnp.maximum(jnp.dot(h_scr[...], w1b, preferred_element_type=F32), 0.0)
    a = (a * a).astype(BF16)
    o_ref[...] += jnp.dot(a, w2b, preferred_element_type=F32)

    if final_norm:
        @pl.when(f == pl.num_programs(1) - 1)
        def _():
            o_ref[...] = _rms_rows(o_ref[...], gf_ref[...])


def _mlp(x, g, w1, w2, gf, tm, tf, final_norm, layer=None):
    m, d = x.shape
    ff = w1.shape[-1]
    emit = layer is not None
    assert m % tm == 0 and ff % tf == 0 and (not emit or m == tm)
    if emit:
        w_specs = [pl.BlockSpec((None, d, tf), lambda i, f: (layer, 0, f)),
                   pl.BlockSpec((None, tf, d), lambda i, f: (layer, f, 0))]
    else:
        w_specs = [pl.BlockSpec((d, tf), lambda i, f: (0, f)), pl.BlockSpec((tf, d), lambda i, f: (f, 0))]
    out_specs = [pl.BlockSpec((tm, d), lambda i, f: (i, 0))]
    out_shape = [jax.ShapeDtypeStruct((m, d), F32)]
    if emit:
        out_specs += [pl.BlockSpec((d, tf), lambda i, f: (0, f)), pl.BlockSpec((tf, d), lambda i, f: (f, 0))]
        out_shape += [jax.ShapeDtypeStruct((d, ff), BF16), jax.ShapeDtypeStruct((ff, d), BF16)]
    outs = pl.pallas_call(
        functools.partial(_mlp_kernel, final_norm=final_norm, emit=emit),
        grid=(m // tm, ff // tf),
        in_specs=[pl.BlockSpec((tm, d), lambda i, f: (i, 0)),
                  pl.BlockSpec((1, d), lambda i, f: (0, 0))] + w_specs
        + [pl.BlockSpec((1, d), lambda i, f: (0, 0))],
        out_specs=out_specs,
        out_shape=out_shape,
        scratch_shapes=[pltpu.VMEM((tm, d), BF16)],
        compiler_params=_cparams(("parallel", "arbitrary")),
        name="mlp",
    )(x, g.reshape(1, d), w1, w2, gf.reshape(1, d))
    return outs if emit else outs[0]


def _flash_init(m_scr, l_scr, acc_scr):
    m_scr[...] = jnp.full(m_scr.shape, NEG, F32)
    l_scr[...] = jnp.zeros(l_scr.shape, F32)
    acc_scr[...] = jnp.zeros(acc_scr.shape, F32)


def _flash_tile(qs, kt, vt, m_scr, l_scr, acc_scr, *, heads, bias=None, mask_add=None):
    s = lax.dot_general(qs, kt, (((1,), (1,)), ((), ())), preferred_element_type=F32)
    rows, tk = s.shape
    if bias is not None or mask_add is not None:
        s3 = s.reshape(heads, rows // heads, tk)
        if bias is not None:
            s3 = s3 + bias
        if mask_add is not None:
            s3 = s3 + mask_add[None]
        s = s3.reshape(rows, tk)
    m_prev = m_scr[...]
    m_next = jnp.maximum(m_prev, jnp.max(s, axis=1, keepdims=True))
    p = jnp.exp2(s - jnp.tile(m_next, (1, tk // LANES)))
    alpha = jnp.exp2(m_prev - m_next)
    v_ones = jnp.concatenate([vt, jnp.ones((tk, LANES), BF16)], axis=1)
    pv = jnp.dot(p.astype(BF16), v_ones, preferred_element_type=F32)
    l_scr[...] = alpha * l_scr[...] + pv[:, LANES:]
    acc_scr[...] = alpha * acc_scr[...] + pv[:, :LANES]
    m_scr[...] = m_next


def _kv_tile(k_ref, v_ref, j, tk):
    off = pl.multiple_of(j * tk, tk)
    return k_ref[pl.ds(off, tk), :].astype(BF16), v_ref[pl.ds(off, tk), :].astype(BF16)


def _diff_lambda_in_kernel(lv_ref, lam_init):
    lv = lv_ref[...]
    a = jnp.sum(lv[0:1] * lv[1:2], axis=1, keepdims=True)
    b = jnp.sum(lv[2:3] * lv[3:4], axis=1, keepdims=True)
    return jnp.exp(a) - jnp.exp(b) + lam_init


DIFF_HEADS_PER_STEP = 4


def _diff_attn_kernel(lv_ref, q_ref, k_ref, v_ref, bias_ref, g_ref, o_ref, m_scr, l_scr, acc_scr, *, lam_init):
    qi = pl.program_id(2)
    tq = q_ref.shape[0]
    tk = bias_ref.shape[-1]
    hp = q_ref.shape[1] // DA_V
    lane = lax.broadcasted_iota(jnp.int32, (tq, DA_V), 1)
    qs = []
    for h in range(hp):
        q = q_ref[:, h * DA_V:(h + 1) * DA_V] * (DA_QK ** -0.5 * LOG2E)
        qs.append(jnp.concatenate([jnp.where(lane < DA_QK, q, 0.0), jnp.where(lane >= DA_QK, q, 0.0)],
                                  axis=0).astype(BF16))
    _flash_init(m_scr, l_scr, acc_scr)

    def tiles(j, which):
        off = pl.multiple_of(j * tk, tk)
        for h in range(hp):
            cols = slice(h * DA_V, (h + 1) * DA_V)
            kt = k_ref[pl.ds(off, tk), cols].astype(BF16)
            vt = v_ref[pl.ds(off, tk), cols].astype(BF16)
            _flash_tile(qs[h], kt, vt, m_scr.at[h], l_scr.at[h], acc_scr.at[h], heads=2,
                        bias=None if which is None else bias_ref[h, which][None])

    def far(j, c):
        tiles(j, None)
        return c

    lax.fori_loop(0, jnp.maximum(qi - 1, 0), far, 0)

    @pl.when(qi == 0)
    def _():
        tiles(qi, 0)

    @pl.when(qi >= 1)
    def _():
        tiles(qi - 1, 1)
        tiles(qi, 0)

    lam = _diff_lambda_in_kernel(lv_ref, lam_init)
    for h in range(hp):
        o = acc_scr[h] / l_scr[h]
        a = o[:tq] - lam * o[tq:]
        o_ref[:, h * DA_V:(h + 1) * DA_V] = (_rms_rows(a, g_ref[...]) * (1.0 - lam_init)).astype(BF16)


def _diff_attn_prompt(z, lam_vec, subln_g, bias, lam_init, batch, seq):
    nq = seq // TQ
    hp = DIFF_HEADS_PER_STEP
    groups = N_HEADS // hp
    wide = hp * DA_V
    return pl.pallas_call(
        functools.partial(_diff_attn_kernel, lam_init=lam_init),
        grid=(batch, groups, nq),
        in_specs=[pl.BlockSpec((4, DA_QK), lambda b, h, i: (0, 0)),
                  pl.BlockSpec((TQ, wide), lambda b, h, i: (b * nq + i, h)),
                  pl.BlockSpec((seq, wide), lambda b, h, i: (b, groups + h)),
                  pl.BlockSpec((seq, wide), lambda b, h, i: (b, 2 * groups + h)),
                  pl.BlockSpec((hp, bias.shape[1], TQ, TK), lambda b, h, i: (h, 0, 0, 0)),
                  pl.BlockSpec((1, DA_V), lambda b, h, i: (0, 0))],
        out_specs=pl.BlockSpec((TQ, wide), lambda b, h, i: (b * nq + i, h)),
        out_shape=jax.ShapeDtypeStruct((batch * seq, N_HEADS * DA_V), BF16),
        scratch_shapes=[pltpu.VMEM((hp, 2 * TQ, LANES), F32)] * 3,
        compiler_params=_cparams(("parallel", "parallel", "arbitrary")),
        name="diff_attn_prompt",
    )(lam_vec, z, z, z, bias, subln_g.reshape(1, DA_V))


def _toeplitz_buckets(tq, tk):
    i = np.arange(tq)[:, None]
    j = np.arange(tk)[None, :]
    diag = _bucket_or_masked(i - j, i >= j)
    sub = _bucket_or_masked(tk + i - j, np.ones((tq, tk), bool))
    return np.stack([diag, sub])


HALO = 16


def _pool_kernel(u_ref, halo_ref, prev_ref, w_ref, scale_ref, o_ref, ext_scr, *, pos0):
    i = pl.program_id(1)
    tm = u_ref.shape[0]
    ext_scr[0:HALO, :] = jnp.where(i == 0, prev_ref[...], halo_ref[...])
    ext_scr[HALO:HALO + tm, :] = u_ref[...]
    pos = pos0 + i * tm + lax.broadcasted_iota(jnp.int32, (tm, 1), 0)
    outs = []
    for g, w in enumerate(POOL_WINDOWS):
        c0, c1 = g * POOL_GROUP, (g + 1) * POOL_GROUP
        x0 = ext_scr[HALO:HALO + tm, c0:c1]
        win = x0
        for k in range(1, w):
            win = win + ext_scr[HALO - k:HALO - k + tm, c0:c1]
        cnt = jnp.minimum(pos + 1, w).astype(F32)
        d = win / cnt - x0
        outs.append(jnp.dot(d.astype(BF16), w_ref[g], preferred_element_type=F32))
    o_ref[...] = (jnp.concatenate(outs, axis=-1) * scale_ref[...]).astype(BF16)


def _pool_mix(z, prev, w_pool, scale, pos0, batch, seq, tm):
    nb = seq // tm
    ucol = 3
    return pl.pallas_call(
        functools.partial(_pool_kernel, pos0=pos0),
        grid=(batch, nb),
        in_specs=[pl.BlockSpec((tm, MIX_WIDTH), lambda b, i: (b * nb + i, ucol)),
                  pl.BlockSpec((HALO, MIX_WIDTH),
                               lambda b, i: (jnp.maximum((b * nb + i) * (tm // HALO) - 1, 0), ucol)),
                  pl.BlockSpec((None, HALO, MIX_WIDTH), lambda b, i: (b, 0, 0)),
                  pl.BlockSpec((4, POOL_GROUP, POOL_GROUP), lambda b, i: (0, 0, 0)),
                  pl.BlockSpec((1, MIX_WIDTH), lambda b, i: (0, 0))],
        out_specs=pl.BlockSpec((tm, MIX_WIDTH), lambda b, i: (b * nb + i, 0)),
        out_shape=jax.ShapeDtypeStruct((batch * seq, MIX_WIDTH), BF16),
        scratch_shapes=[pltpu.VMEM((HALO + tm, MIX_WIDTH), F32)],
        compiler_params=_cparams(("parallel", "arbitrary")),
        name="pool_mix",
    )(z, z, prev, w_pool, scale.reshape(1, MIX_WIDTH))


COL_CB, COL_CC, COL_CH, COL_KC, COL_KS, COL_KW, COL_GATE = 8, 16, 24, 32, 36, 40, 44


def _odd_weight_layout(w_in):
    offs = np.cumsum((0,) + ODD_SIZES)
    q, kc, ks_, kw, gates, cb, cc, ch = [w_in[:, offs[i]:offs[i + 1]] for i in range(8)]
    pad = jnp.zeros((w_in.shape[0], LANES - ODD_SIZES[4]), w_in.dtype)
    return jnp.concatenate([q, cb, cc, ch, kc, ks_, kw, gates, pad], axis=1)


def _select_blocks(score_t, q0, nb):
    nbp, tq = score_t.shape
    blk = lax.broadcasted_iota(jnp.int32, (nbp, tq), 0)
    cur = (q0 + lax.broadcasted_iota(jnp.int32, (nbp, tq), 1)) // SLC_BLOCK
    forced = (blk == 0) | (blk == cur) | (blk == cur - 1)
    future = (blk > cur) | (blk >= nb)
    s = jnp.where(forced, jnp.inf, jnp.where(future, -jnp.inf, score_t))
    cnt = jnp.zeros((nbp, tq), jnp.int32)
    for n in range(nb):
        row = s[n:n + 1, :]
        beats = (row > s) | ((row == s) & (blk > n))
        cnt = cnt + beats.astype(jnp.int32)
    return (cnt < min(SLC_TOPK, nb)).astype(F32)


def _nsa_prompt_kernel(q_ref, kc_ref, vc_ref, ks_ref, vs_ref, kw_ref, vw_ref, gate_ref, tb_ref, cb_ref, e_ref,
                       o_ref, kcm_scr, vcm_scr, m_scr, l_scr, acc_scr):
    g = pl.program_id(1)
    qi = pl.program_id(2)
    tq = q_ref.shape[0]
    tk = tb_ref.shape[-1]
    seq = kc_ref.shape[0]
    nb = seq // CMP_BLOCK
    nbp = -(-nb // 8) * 8
    scale = NSA_DH ** -0.5
    hp = NSA_HPG

    @pl.when(qi == 0)
    def _():
        kcm_scr[...] = jnp.zeros(kcm_scr.shape, BF16)
        vcm_scr[...] = jnp.zeros(vcm_scr.shape, BF16)
        kcm_scr[0:nb, :] = (jnp.sum(kc_ref[...].reshape(nb, CMP_BLOCK, NSA_DH), axis=1) / CMP_BLOCK).astype(BF16)
        vcm_scr[0:nb, :] = (jnp.sum(vc_ref[...].reshape(nb, CMP_BLOCK, NSA_DH), axis=1) / CMP_BLOCK).astype(BF16)

    qh = [q_ref[:, h * NSA_DH:(h + 1) * NSA_DH].astype(BF16) for h in range(hp)]
    qs = jnp.concatenate([(q_ref[:, h * NSA_DH:(h + 1) * NSA_DH] * (scale * LOG2E)).astype(BF16)
                          for h in range(hp)], axis=0)

    kcm = kcm_scr[...]
    vcm = vcm_scr[...]
    o_cmp = []
    p_grp_t = jnp.zeros((LANES, tq), F32)
    for h in range(hp):
        lt = lax.dot_general(kcm, qh[h], (((1,), (1,)), ((), ())), preferred_element_type=F32) * scale
        bt = cb_ref[h]
        lt = lt + bt
        p = jnp.exp(lt - jnp.max(lt, axis=0, keepdims=True))
        p = p / jnp.sum(p, axis=0, keepdims=True)
        p = jnp.where(bt > 0.5 * NEG, p, 0.0)
        p_grp_t = p_grp_t + p
        o_cmp.append(jnp.dot(p.T.astype(BF16), vcm, preferred_element_type=F32))

    sel_t = _select_blocks(p_grp_t[0:nbp, :], qi * tq, nb)
    if nbp < LANES:
        sel_t = jnp.concatenate([sel_t, jnp.zeros((LANES - nbp, tq), F32)], axis=0)
    sel_neg = ((sel_t.T - 1.0) * MASK_BIG).astype(BF16)

    def sel_mask(j):
        return jnp.dot(sel_neg, e_ref[j], preferred_element_type=F32)

    def slc_tile(j, which):
        kt, vt = _kv_tile(ks_ref, vs_ref, j, tk)
        _flash_tile(qs, kt, vt, m_scr.at[0], l_scr.at[0], acc_scr.at[0], heads=hp,
                    bias=None if which is None else tb_ref[:, which], mask_add=sel_mask(j))

    def win_tile(j, which):
        kt, vt = _kv_tile(kw_ref, vw_ref, j, tk)
        _flash_tile(qs, kt, vt, m_scr.at[1], l_scr.at[1], acc_scr.at[1], heads=hp, bias=tb_ref[:, which])

    _flash_init(m_scr, l_scr, acc_scr)
    n_far = jnp.maximum(qi - 1, 0)

    def far_pair(jj, c):
        slc_tile(2 * jj, None)
        slc_tile(2 * jj + 1, None)
        return c

    lax.fori_loop(0, n_far // 2, far_pair, 0)

    @pl.when(n_far % 2 == 1)
    def _():
        slc_tile(n_far - 1, None)

    def near_tiles(before):
        if before >= 2:
            win_tile(qi - 2, 2)
        if before >= 1:
            slc_tile(qi - 1, 1)
            win_tile(qi - 1, 1)
        slc_tile(qi, 0)
        win_tile(qi, 0)

    @pl.when(qi == 0)
    def _():
        near_tiles(0)

    @pl.when(qi == 1)
    def _():
        near_tiles(1)

    @pl.when(qi >= 2)
    def _():
        near_tiles(2)

    o_slc = acc_scr[0] / l_scr[0]
    o_win = acc_scr[1] / l_scr[1]

    sig = jax.nn.sigmoid(gate_ref[...])
    lane = lax.broadcasted_iota(jnp.int32, sig.shape, 1)

    def gate(branch, h):
        col = branch * N_HEADS + g * hp + h
        return jnp.sum(jnp.where(lane == col, sig, 0.0), axis=1, keepdims=True)

    for h in range(hp):
        rows = slice(h * tq, (h + 1) * tq)
        o = gate(0, h) * o_cmp[h] + gate(1, h) * o_slc[rows] + gate(2, h) * o_win[rows]
        o_ref[:, h * NSA_DH:(h + 1) * NSA_DH] = o.astype(BF16)


def _nsa_buckets(seq, tq, tk):
    i = np.arange(tq)[:, None]
    j = np.arange(tk)[None, :]
    toe = _toeplitz_buckets(tq, tk)
    assert WINDOW == 2 * tk and tq == tk
    win2 = np.where(i < j, NUM_BUCKETS - 1, -1).astype(np.int32)
    tiles = np.concatenate([toe, win2[None]])
    nb = seq // CMP_BLOCK
    blk_end = (np.arange(LANES)[:, None] + 1) * CMP_BLOCK - 1
    dist = np.arange(seq)[None, :] - blk_end
    cmp_t = _bucket_or_masked(dist, (dist >= 0) & (np.arange(LANES)[:, None] < nb))
    nk = seq // tk
    key_blk = (np.arange(nk)[:, None, None] * tk + np.arange(tk)[None, None, :]) // SLC_BLOCK
    expand = (key_blk == np.arange(LANES)[None, :, None]).astype(np.float32)
    return tiles, cmp_t, expand


def _nsa_prompt(z, tile_bias, cmp_bias_t, expand, batch, seq):
    nq = seq // TQ
    nk = seq // TK
    kv = lambda col: pl.BlockSpec((seq, NSA_DH), lambda b, g, i: (b, col + g))
    return pl.pallas_call(
        _nsa_prompt_kernel,
        grid=(batch, NSA_KV, nq),
        in_specs=[pl.BlockSpec((TQ, NSA_HPG * NSA_DH), lambda b, g, i: (b * nq + i, g)),
                  kv(COL_KC), kv(COL_KC + 2), kv(COL_KS), kv(COL_KS + 2), kv(COL_KW), kv(COL_KW + 2),
                  pl.BlockSpec((TQ, LANES), lambda b, g, i: (b * nq + i, COL_GATE)),
                  pl.BlockSpec((NSA_HPG, 3, TQ, TK), lambda b, g, i: (g, 0, 0, 0)),
                  pl.BlockSpec((NSA_HPG, LANES, TQ), lambda b, g, i: (g, 0, i)),
                  pl.BlockSpec((nk, LANES, TK), lambda b, g, i: (0, 0, 0))],
        out_specs=pl.BlockSpec((TQ, NSA_HPG * NSA_DH), lambda b, g, i: (b * nq + i, g)),
        out_shape=jax.ShapeDtypeStruct((batch * seq, MIX_WIDTH), BF16),
        scratch_shapes=[pltpu.VMEM((LANES, NSA_DH), BF16), pltpu.VMEM((LANES, NSA_DH), BF16)]
        + [pltpu.VMEM((2, NSA_HPG * TQ, LANES), F32)] * 3,
        compiler_params=_cparams(("parallel", "parallel", "arbitrary")),
        name="nsa_prompt",
    )(z, z, z, z, z, z, z, z, tile_bias, cmp_bias_t, expand)


def _conv_kernel(cb_ref, cc_ref, ch_ref, hc_ref, hh_ref, prev_ref, w_ref, o_ref, tail_ref, ext_scr):
    i = pl.program_id(1)
    tm = cb_ref.shape[0]
    e = cc_ref[...] * ch_ref[...]
    ext_scr[0:8, :] = jnp.where(i == 0, prev_ref[...], hc_ref[...] * hh_ref[...])
    ext_scr[8:8 + tm, :] = e
    w = w_ref[...]
    y = w[0:1] * ext_scr[6:6 + tm, :]
    y = y + w[1:2] * ext_scr[7:7 + tm, :]
    y = y + w[2:3] * e
    o_ref[...] = (cb_ref[...] * y).astype(BF16)

    @pl.when(i == pl.num_programs(1) - 1)
    def _():
        tail_ref[...] = e[tm - 8:tm, :]


def _short_conv(z, prev, w_conv, batch, seq, tm):
    nb = seq // tm
    cw = MIX_WIDTH // LANES
    blk = lambda c: pl.BlockSpec((tm, MIX_WIDTH), lambda b, i: (b * nb + i, c // cw))
    halo = lambda c: pl.BlockSpec((8, MIX_WIDTH), lambda b, i: (jnp.maximum((b * nb + i) * (tm // 8) - 1, 0), c // cw))
    return pl.pallas_call(
        _conv_kernel,
        grid=(batch, nb),
        in_specs=[blk(COL_CB), blk(COL_CC), blk(COL_CH), halo(COL_CC), halo(COL_CH),
                  pl.BlockSpec((None, 8, MIX_WIDTH), lambda b, i: (b, 0, 0)),
                  pl.BlockSpec((8, MIX_WIDTH), lambda b, i: (0, 0))],
        out_specs=[pl.BlockSpec((tm, MIX_WIDTH), lambda b, i: (b * nb + i, 0)),
                   pl.BlockSpec((None, 8, MIX_WIDTH), lambda b, i: (b, 0, 0))],
        out_shape=[jax.ShapeDtypeStruct((batch * seq, MIX_WIDTH), BF16),
                   jax.ShapeDtypeStruct((batch, 8, MIX_WIDTH), F32)],
        scratch_shapes=[pltpu.VMEM((8 + tm, MIX_WIDTH), F32)],
        compiler_params=_cparams(("parallel", "arbitrary")),
        name="short_conv",
    )(z, z, z, z, z, prev, jnp.pad(w_conv, ((0, 8 - CONV_WIDTH), (0, 0))))


T8 = 8
NEW_ROWS = 16
DIFF_PAGES_PER_STEP = 8
NSA_PAGES_PER_STEP = 16
NSA_ROW = 2 * NSA_KV
V_SHIFT = NSA_KV
NB_DEC = 288


def _nt(a, b):
    return lax.dot_general(a, b, (((1,), (1,)), ((), ())), preferred_element_type=F32)


def _softmax_update(st, v_tiles, width, m_ref, l_ref, acc_ref):
    m_prev = m_ref[...]
    m_next = jnp.maximum(m_prev, jnp.max(st, axis=1, keepdims=True))
    p = jnp.exp(st - jnp.tile(m_next, (1, st.shape[1] // LANES)))
    alpha = jnp.exp(m_prev - m_next)
    l_ref[...] = alpha * l_ref[...] + jnp.sum(p, axis=1, keepdims=True)
    pv = jnp.zeros(acc_ref.shape, F32)
    for k, vt in enumerate(v_tiles):
        pv = pv + jnp.dot(p[:, k * width:(k + 1) * width].astype(BF16), vt, preferred_element_type=F32)
    acc_ref[...] = alpha * acc_ref[...] + pv
    m_ref[...] = m_next


def _diff_decode_kernel(pt_ref, q_ref, new_ref, mask_ref, newmask_ref, lv_ref, g_ref, *rest, lam_init):
    pages = rest[:DIFF_PAGES_PER_STEP]
    o_ref, m_scr, l_scr, acc_scr = rest[DIFF_PAGES_PER_STEP:]
    s = pl.program_id(1)
    last = pl.num_programs(1) - 1
    width = PAGE_SIZE * N_HEADS

    @pl.when(s == 0)
    def _():
        _flash_init(m_scr, l_scr, acc_scr)

    q = q_ref[...] * (DA_QK ** -0.5)
    lane = lax.broadcasted_iota(jnp.int32, q.shape, 1)
    qs = jnp.concatenate([jnp.where(lane < DA_QK, q, 0.0), jnp.where(lane >= DA_QK, q, 0.0)], axis=0).astype(BF16)
    upd = functools.partial(_softmax_update, m_ref=m_scr, l_ref=l_scr, acc_ref=acc_scr)

    is_last = s == last
    tiles, values = [], []
    for k, pg in enumerate(pages):
        st = _nt(qs, pg[:, 0].reshape(width, DA_V).astype(BF16))
        if k == DIFF_PAGES_PER_STEP - 1:
            st = st + jnp.where(is_last, mask_ref[1], mask_ref[0])
        else:
            st = st + mask_ref[0]
        tiles.append(st)
        values.append(pg[:, 1].reshape(width, DA_V).astype(BF16))
    upd(jnp.concatenate(tiles, axis=1), values, width)

    @pl.when(is_last)
    def _():
        nw = NEW_ROWS * N_HEADS
        st = _nt(qs, new_ref[:, 0].reshape(nw, DA_V).astype(BF16)) + newmask_ref[...]
        upd(st, [new_ref[:, 1].reshape(nw, DA_V).astype(BF16)], nw)
        o = acc_scr[...] / l_scr[...]
        half = o.shape[0] // 2
        a = o[:half] - _diff_lambda_in_kernel(lv_ref, lam_init) * o[half:]
        o_ref[...] = _rms_rows(a, g_ref[...]) * (1.0 - lam_init)


def _diff_decode(page_table, q_ht, kv_new, masks, new_mask, lam_vec, subln_g, cache, layer, lam_init):
    batch, n_pages = page_table.shape
    steps = n_pages // DIFF_PAGES_PER_STEP
    rows = q_ht.shape[1]

    def page_spec(k):
        return pl.BlockSpec((None, None, PAGE_SIZE, 2, N_HEADS, DA_V),
                            lambda b, s, pt: (layer, pt[b, s * DIFF_PAGES_PER_STEP + k], 0, 0, 0, 0))

    grid_spec = pltpu.PrefetchScalarGridSpec(
        num_scalar_prefetch=1,
        grid=(batch, steps),
        in_specs=[pl.BlockSpec((None, rows, DA_V), lambda b, s, pt: (b, 0, 0)),
                  pl.BlockSpec((None, NEW_ROWS, 2, N_HEADS, DA_V), lambda b, s, pt: (b, 0, 0, 0, 0)),
                  pl.BlockSpec(masks.shape, lambda b, s, pt: (0, 0, 0)),
                  pl.BlockSpec(new_mask.shape, lambda b, s, pt: (0, 0)),
                  pl.BlockSpec((4, DA_QK), lambda b, s, pt: (0, 0)),
                  pl.BlockSpec((1, DA_V), lambda b, s, pt: (0, 0))]
        + [page_spec(k) for k in range(DIFF_PAGES_PER_STEP)],
        out_specs=pl.BlockSpec((None, rows, DA_V), lambda b, s, pt: (b, 0, 0)),
        scratch_shapes=[pltpu.VMEM((2 * rows, LANES), F32)] * 3,
    )
    return pl.pallas_call(
        functools.partial(_diff_decode_kernel, lam_init=lam_init),
        grid_spec=grid_spec,
        out_shape=jax.ShapeDtypeStruct((batch, rows, DA_V), F32),
        compiler_params=_cparams(("parallel", "arbitrary")),
        name="diff_decode",
    )(page_table, q_ht, kv_new, masks, new_mask, lam_vec, subln_g.reshape(1, DA_V),
      *([cache] * DIFF_PAGES_PER_STEP))


def _nsa_page_specs(layer):
    def spec(k):
        return pl.BlockSpec((None, None, PAGE_SIZE * NSA_ROW, NSA_DH),
                            lambda b, s, pt: (layer, pt[b, s * NSA_PAGES_PER_STEP + k], 0, 0))
    return [spec(k) for k in range(NSA_PAGES_PER_STEP)]


def _cmp_means_kernel(pt_ref, *rest):
    pages = rest[:NSA_PAGES_PER_STEP]
    o_ref = rest[NSA_PAGES_PER_STEP]
    per_blk = CMP_BLOCK * NSA_ROW // 8
    low = lax.broadcasted_iota(jnp.int32, (8, NSA_DH), 0) < NSA_ROW
    out = []
    for pg in pages:
        x = pg[...].reshape(PAGE_SIZE * NSA_ROW // 8, 8, NSA_DH)
        t0 = jnp.sum(x[0:per_blk], axis=0)
        t1 = jnp.sum(x[per_blk:2 * per_blk], axis=0)
        t0 = t0 + pltpu.roll(t0, NSA_ROW, axis=0)
        t1 = t1 + pltpu.roll(t1, NSA_ROW, axis=0)
        out.append(jnp.where(low, t0, t1) / CMP_BLOCK)
    o_ref[...] = jnp.concatenate(out, axis=0)


def _cmp_means(page_table, cache, layer):
    batch, n_pages = page_table.shape
    steps = n_pages // NSA_PAGES_PER_STEP
    per_step = NSA_PAGES_PER_STEP * 8
    grid_spec = pltpu.PrefetchScalarGridSpec(
        num_scalar_prefetch=1,
        grid=(batch, steps),
        in_specs=_nsa_page_specs(layer),
        out_specs=pl.BlockSpec((None, per_step, NSA_DH), lambda b, s, pt: (b, s, 0)),
    )
    return pl.pallas_call(
        _cmp_means_kernel,
        grid_spec=grid_spec,
        out_shape=jax.ShapeDtypeStruct((batch, steps * per_step, NSA_DH), F32),
        compiler_params=_cparams(("parallel", "arbitrary")),
        name="cmp_means",
    )(page_table, *([cache] * NSA_PAGES_PER_STEP))


def _cmp_select_decode_kernel(q_ref, kvc_ref, new_ref, tile_ref, ocmp_ref, sel_ref, k_scr, *, past_len, n_new):
    n_past = kvc_ref.shape[0]
    width = NB_DEC * NSA_ROW
    scale = NSA_DH ** -0.5
    k_scr[...] = jnp.zeros(k_scr.shape, F32)
    k_scr[0:n_past, :] = kvc_ref[...]
    real = lax.broadcasted_iota(jnp.int32, new_ref.shape, 0) < n_new
    tot = jnp.sum(jnp.where(real, new_ref[...], 0.0), axis=0, keepdims=True) / CMP_BLOCK
    new4 = jnp.concatenate([tot[:, c * NSA_DH:(c + 1) * NSA_DH] for c in range(NSA_ROW)]
                           + [jnp.zeros((8 - NSA_ROW, NSA_DH), F32)], axis=0)
    k_scr[n_past:n_past + 8, :] = new4

    q = q_ref[...].astype(BF16)
    tile = tile_ref[...]
    lg = _nt(q, k_scr[0:width, :].astype(BF16)) * scale + tile
    p = jnp.exp(lg - jnp.max(lg, axis=1, keepdims=True))
    p = p / jnp.sum(p, axis=1, keepdims=True)
    p = jnp.where(tile > 0.5 * NEG, p, 0.0)
    ocmp_ref[...] = jnp.dot(p.astype(BF16), k_scr[V_SHIFT:V_SHIFT + width, :].astype(BF16),
                            preferred_element_type=F32)

    lane = lax.broadcasted_iota(jnp.int32, (T8, width), 1)
    blk = lane // NSA_ROW
    cur = (past_len + lax.broadcasted_iota(jnp.int32, (T8, width), 0)) // SLC_BLOCK
    rows_g = NSA_HPG * T8
    for g in range(NSA_KV):
        mine = (lane % NSA_ROW) == g
        score = sum(p[g * rows_g + h * T8:g * rows_g + (h + 1) * T8] for h in range(NSA_HPG))
        forced = mine & ((blk == 0) | (blk == cur) | (blk == cur - 1))
        dead = (blk > cur) | jnp.logical_not(mine)
        sc = jnp.where(forced, jnp.inf, jnp.where(dead, -jnp.inf, score))
        taken = jnp.zeros((T8, width), jnp.bool_)
        for _ in range(SLC_TOPK):
            sm = jnp.where(taken, -jnp.inf, sc)
            cand = (sm == jnp.max(sm, axis=1, keepdims=True)) & jnp.logical_not(taken)
            idx = jnp.min(jnp.where(cand, lane, width), axis=1, keepdims=True)
            taken = taken | (lane == idx)
        sel_ref[g * T8:(g + 1) * T8, :] = taken.astype(F32)


def _cmp_select_decode(qg, kvc, kv_new, tile, past_len, n_new):
    batch, rows, _ = qg.shape
    width = NB_DEC * NSA_ROW
    n_past = kvc.shape[1]
    return pl.pallas_call(
        functools.partial(_cmp_select_decode_kernel, past_len=past_len, n_new=n_new),
        grid=(batch,),
        in_specs=[pl.BlockSpec((None, rows, NSA_DH), lambda b: (b, 0, 0)),
                  pl.BlockSpec((None, n_past, NSA_DH), lambda b: (b, 0, 0)),
                  pl.BlockSpec((None, T8, NSA_ROW * NSA_DH), lambda b: (b, 0, 0)),
                  pl.BlockSpec((rows, width), lambda b: (0, 0))],
        out_specs=[pl.BlockSpec((None, rows, NSA_DH), lambda b: (b, 0, 0)),
                   pl.BlockSpec((None, NSA_KV * T8, width), lambda b: (b, 0, 0))],
        out_shape=[jax.ShapeDtypeStruct((batch, rows, NSA_DH), F32),
                   jax.ShapeDtypeStruct((batch, NSA_KV * T8, width), F32)],
        scratch_shapes=[pltpu.VMEM((width + 8, NSA_DH), F32)],
        compiler_params=_cparams(("parallel",)),
        name="cmp_select_decode",
    )(qg, kvc, kv_new, tile)


def _masked_scores(st, sel, expand):
    rows, n = st.shape
    add = (jnp.dot(sel.astype(BF16), expand, preferred_element_type=F32) - 1.0) * (-NEG)
    st4 = st.reshape(NSA_KV, NSA_HPG, T8, n) + add.reshape(NSA_KV, 1, T8, n)
    return st4.reshape(rows, n)


def _slc_decode_kernel(pt_ref, q_ref, sel_ref, selnew_ref, e_ref, new_ref, tile_ref, newtile_ref, *rest):
    pages = rest[:NSA_PAGES_PER_STEP]
    o_ref, m_scr, l_scr, acc_scr = rest[NSA_PAGES_PER_STEP:]
    s = pl.program_id(1)
    last = pl.num_programs(1) - 1
    scale = NSA_DH ** -0.5
    width = PAGE_SIZE * NSA_ROW

    @pl.when(s == 0)
    def _():
        _flash_init(m_scr, l_scr, acc_scr)

    q = q_ref[...].astype(BF16)
    upd = functools.partial(_softmax_update, m_ref=m_scr, l_ref=l_scr, acc_ref=acc_scr)
    is_last = (s == last).astype(F32)
    tiles, values = [], []
    for k, pg in enumerate(pages):
        x = pg[...]
        st = _nt(q, x.astype(BF16)) * scale
        if k == NSA_PAGES_PER_STEP - 1:
            st = st + is_last * tile_ref[...]
        tiles.append(st)
        values.append(pltpu.roll(x, width - V_SHIFT, axis=0).astype(BF16))
    upd(_masked_scores(jnp.concatenate(tiles, axis=1), sel_ref[...], e_ref[...]), values, width)

    @pl.when(s == last)
    def _():
        x = new_ref[...]
        n = x.shape[0]
        st = _nt(q, x.astype(BF16)) * scale + newtile_ref[...]
        upd(_masked_scores(st, selnew_ref[...], e_ref[:, 0:n]), [pltpu.roll(x, n - V_SHIFT, axis=0).astype(BF16)], n)
        o_ref[...] = acc_scr[...] / l_scr[...]


def _slc_decode(page_table, qg, sel, expand, kv_new, tile, new_tile, cache, layer):
    batch, n_pages = page_table.shape
    steps = n_pages // NSA_PAGES_PER_STEP
    rows = qg.shape[1]
    keys = NSA_PAGES_PER_STEP * PAGE_SIZE * NSA_ROW
    grid_spec = pltpu.PrefetchScalarGridSpec(
        num_scalar_prefetch=1,
        grid=(batch, steps),
        in_specs=[pl.BlockSpec((None, rows, NSA_DH), lambda b, s, pt: (b, 0, 0)),
                  pl.BlockSpec((None, NSA_KV * T8, LANES), lambda b, s, pt: (b, 0, s)),
                  pl.BlockSpec((None, NSA_KV * T8, LANES), lambda b, s, pt: (b, 0, steps)),
                  pl.BlockSpec((LANES, keys), lambda b, s, pt: (0, 0)),
                  pl.BlockSpec((None,) + kv_new.shape[1:], lambda b, s, pt: (b, 0, 0)),
                  pl.BlockSpec(tile.shape, lambda b, s, pt: (0, 0)),
                  pl.BlockSpec(new_tile.shape, lambda b, s, pt: (0, 0))]
        + _nsa_page_specs(layer),
        out_specs=pl.BlockSpec((None, rows, NSA_DH), lambda b, s, pt: (b, 0, 0)),
        scratch_shapes=[pltpu.VMEM((rows, LANES), F32)] * 3,
    )
    return pl.pallas_call(
        _slc_decode_kernel,
        grid_spec=grid_spec,
        out_shape=jax.ShapeDtypeStruct((batch, rows, NSA_DH), F32),
        compiler_params=_cparams(("parallel", "arbitrary")),
        name="slc_decode",
    )(page_table, qg, sel, sel, expand, kv_new, tile, new_tile, *([cache] * NSA_PAGES_PER_STEP))


def _win_combine_decode_kernel(q_ref, state_ref, new_ref, tile_ref, gate_ref, ocmp_ref, oslc_ref, o_ref):
    scale = NSA_DH ** -0.5
    q = q_ref[...].astype(BF16)
    xs, xn = state_ref[...], new_ref[...]
    ns, nn = xs.shape[0], xn.shape[0]
    st = jnp.concatenate([_nt(q, xs.astype(BF16)), _nt(q, xn.astype(BF16))], axis=1) * scale + tile_ref[...]
    p = jnp.exp(st - jnp.max(st, axis=1, keepdims=True))
    p = (p / jnp.sum(p, axis=1, keepdims=True)).astype(BF16)
    o_win = (jnp.dot(p[:, 0:ns], pltpu.roll(xs, ns - V_SHIFT, axis=0).astype(BF16), preferred_element_type=F32)
             + jnp.dot(p[:, ns:], pltpu.roll(xn, nn - V_SHIFT, axis=0).astype(BF16), preferred_element_type=F32))
    sig = jax.nn.sigmoid(gate_ref[...])
    for g in range(NSA_KV):
        for h in range(NSA_HPG):
            head = g * NSA_HPG + h
            rows = slice(head * T8, (head + 1) * T8)
            gate = lambda branch: sig[:, branch * N_HEADS + head:branch * N_HEADS + head + 1]
            o = gate(0) * ocmp_ref[rows, :] + gate(1) * oslc_ref[rows, :] + gate(2) * o_win[rows]
            o_ref[:, head * NSA_DH:(head + 1) * NSA_DH] = o.astype(BF16)


def _win_combine_decode(qg, state_win, kv_new, tile, gates, o_cmp, o_slc):
    batch, rows, _ = qg.shape
    branch = pl.BlockSpec((None, rows, NSA_DH), lambda b: (b, 0, 0))
    return pl.pallas_call(
        _win_combine_decode_kernel,
        grid=(batch,),
        in_specs=[branch,
                  pl.BlockSpec((None,) + state_win.shape[1:], lambda b: (b, 0, 0)),
                  pl.BlockSpec((None,) + kv_new.shape[1:], lambda b: (b, 0, 0)),
                  pl.BlockSpec(tile.shape, lambda b: (0, 0)),
                  pl.BlockSpec((T8, LANES), lambda b: (b, COL_GATE)),
                  branch, branch],
        out_specs=pl.BlockSpec((T8, MIX_WIDTH), lambda b: (b, 0)),
        out_shape=jax.ShapeDtypeStruct((batch * T8, MIX_WIDTH), BF16),
        compiler_params=_cparams(("parallel",)),
        name="win_combine_decode",
    )(qg, state_win, kv_new, tile, gates, o_cmp, o_slc)


def _decode_buckets(past_len, wbuf):
    t = np.arange(T8)[:, None]
    dpos = np.repeat(np.arange(PAGE_SIZE), N_HEADS)[None, :]
    dnew = np.repeat(np.arange(NEW_ROWS), N_HEADS)[None, :]
    npos = np.repeat(np.arange(PAGE_SIZE), NSA_ROW)[None, :]
    nnew = np.repeat(np.arange(PAGE_SIZE // NSA_ROW), NSA_ROW)[None, :]
    nblk = np.repeat(np.arange(NB_DEC), NSA_ROW)[None, :]
    nwin = np.repeat(np.arange(wbuf), NSA_ROW)[None, :]
    d_cmp = past_len + t - ((nblk + 1) * CMP_BLOCK - 1)
    d_win = t + wbuf - nwin
    parts = [
        _bucket_or_masked(PAGE_SIZE + t - dpos, np.ones((T8, dpos.shape[1]), bool)),
        _bucket_or_masked(t - dnew, dnew <= t),
        _bucket_or_masked(PAGE_SIZE + t - npos, np.ones((T8, npos.shape[1]), bool)),
        _bucket_or_masked(t - nnew, nnew <= t),
        _bucket_or_masked(d_cmp, d_cmp >= 0),
        _bucket_or_masked(d_win, (d_win >= 0) & (d_win < WINDOW) & (past_len - wbuf + nwin >= 0)),
    ]
    return np.concatenate(parts, axis=1), np.cumsum([0] + [p.shape[1] for p in parts])


def _pad_rows(x, rows):
    return jnp.pad(x, ((0, 0), (0, rows - x.shape[1])) + ((0, 0),) * (x.ndim - 2))


def _tile_cfg():
    return dict(tm=1024, tn_even=512, tm_odd=512, tn_odd=1152, tn_out=1024, tm_mlp=512, tf=1024, tf_cast=512,
                tm_pool=512)


def kernel(x_prompt, x_sample, cache_diff_kv, cache_cmp_kv, cache_slc_kv, state_win_kv, state_pool, state_conv,
           page_table, rel_bias, norm_mix, norm_mlp, norm_final, even_w_in, even_w_out, diff_lambda, diff_subln,
           pool_w, pool_scale, odd_w_in, odd_w_out, conv_w, mlp_w1, mlp_w2):
    bp, seq, d = x_prompt.shape
    bs, ts, _ = x_sample.shape
    depth = norm_mix.shape[0]
    n_pages = page_table.shape[1]
    past_len = n_pages * PAGE_SIZE
    n_phys = cache_diff_kv.shape[1]
    wbuf = state_win_kv.shape[2]
    n_past_blk = past_len // SLC_BLOCK
    assert seq % TQ == 0 and ts <= T8 and n_pages % NSA_PAGES_PER_STEP == 0 and wbuf == WINDOW
    assert (past_len + ts - 1) // SLC_BLOCK == n_past_blk < NB_DEC
    cfg = _tile_cfg()

    w_in_o = [_odd_weight_layout(odd_w_in[o].astype(BF16)) for o in range(odd_w_in.shape[0])]
    pool_wb = pool_w.astype(BF16)

    tiles, cmp_t, expand = _nsa_buckets(seq, TQ, TK)
    tile_bias = _bias_tiles(rel_bias, tiles, LOG2E)
    cmp_bias_t = _bias_tiles(rel_bias, cmp_t)
    expand = jnp.asarray(expand, BF16)
    dec_buckets, off = _decode_buckets(past_len, wbuf)
    dec = _bias_tiles(rel_bias, dec_buckets)
    part = lambda i: dec[:, :, off[i]:off[i + 1]]

    def diff_rows(x):
        n = x.shape[-1]
        own = (np.arange(n) % N_HEADS)[None, None, :] == np.arange(N_HEADS)[:, None, None]
        x = jnp.where(own, x, NEG)
        return jnp.broadcast_to(x[None], (2,) + x.shape).reshape(2 * N_HEADS * T8, n)

    def nsa_rows(x):
        n = x.shape[-1]
        own = (np.arange(n) % NSA_ROW)[None, None, :] == (np.arange(N_HEADS) // NSA_HPG)[:, None, None]
        return jnp.where(own, x, NEG).reshape(N_HEADS * T8, n)

    diff_masks = jnp.stack([diff_rows(jnp.zeros_like(part(0))), diff_rows(part(0))])
    diff_new_mask = diff_rows(part(1))
    slc_tile = nsa_rows(part(2))
    nsa_new_tile = nsa_rows(part(3))
    cmp_tile = nsa_rows(part(4))
    win_tile = jnp.concatenate([nsa_rows(part(5)), nsa_new_tile], axis=1)
    key = np.arange(NSA_PAGES_PER_STEP * PAGE_SIZE * NSA_ROW)
    sel_lane = (key // (NSA_ROW * SLC_BLOCK)) * NSA_ROW + key % NSA_ROW
    is_key_row = (key % NSA_ROW) < NSA_KV
    expand_dec = jnp.asarray((np.arange(LANES)[:, None] == sel_lane[None, :]) & is_key_row[None, :], BF16)

    cache_cmp = cache_cmp_kv.reshape(cache_cmp_kv.shape[0], n_phys, PAGE_SIZE * NSA_ROW, NSA_DH)
    cache_slc = cache_slc_kv.reshape(cache_slc_kv.shape[0], n_phys, PAGE_SIZE * NSA_ROW, NSA_DH)

    mp, ms = bp * seq, bs * T8
    xp = x_prompt.reshape(mp, d)
    xs = _pad_rows(x_sample, T8).reshape(ms, d)
    tm_p = min(cfg["tm"], mp)
    outs = {k: [] for k in ("diff_s", "pool_p", "pool_s", "cmp_s", "slc_s", "win_s", "conv_p", "conv_s")}
    kvw = NSA_ROW * NSA_DH
    n_even, n_odd = (depth + 1) // 2, depth // 2
    even_taps = ((N_HEADS * DA_V, 2 * N_HEADS * DA_V, 1),)
    odd_taps = tuple((c * LANES, kvw, NSA_ROW) for c in (COL_KC, COL_KS, COL_KW))
    diff_kv_p, nsa_kv_p = (), ()

    for layer in range(depth):
        if layer % 2 == 0:
            e = layer // 2
            lam_init = 0.8 - 0.6 * math.exp(-0.3 * layer)
            zs, (w_in_b,) = _norm_matmul(xs, norm_mix[layer], even_w_in, ms, cfg["tn_even"], w_layer=e)
            zp, diff_kv_p = _norm_matmul(xp, norm_mix[layer], w_in_b, tm_p, cfg["tn_even"], even_taps, n_even, e,
                                         diff_kv_p)
            zp3 = zp.reshape(bp, seq, -1)
            zs3 = zs.reshape(bs, T8, -1)
            o_attn = _diff_attn_prompt(zp, diff_lambda[e], diff_subln[e], tile_bias, lam_init, bp, seq)
            o_pool = _pool_mix(zp, jnp.zeros((bp, HALO, MIX_WIDTH), F32), pool_wb[e], pool_scale[e], 0, bp, seq,
                               cfg["tm_pool"])
            q_ht = zs3[:, :, 0:1024].reshape(bs, T8, N_HEADS, DA_V).transpose(0, 2, 1, 3).reshape(
                bs, N_HEADS * T8, DA_V)
            kv_new = _pad_rows(zs3[:, :, 1024:3072].reshape(bs, T8, 2, N_HEADS, DA_V), NEW_ROWS)
            o_dec = _diff_decode(page_table, q_ht, kv_new, diff_masks, diff_new_mask, diff_lambda[e], diff_subln[e],
                                 cache_diff_kv, e, lam_init)
            o_attn_s = o_dec.reshape(bs, N_HEADS, T8, DA_V).transpose(0, 2, 1, 3).reshape(ms, N_HEADS * DA_V)
            prev = jnp.pad(state_pool[e], ((0, 0), (HALO - POOL_STATE, 0), (0, 0)))
            o_pool_s = _pool_mix(zs, prev, pool_wb[e], pool_scale[e], past_len, bs, T8, T8)
            xs, w_out_b = _out_proj(xs, o_attn_s.astype(BF16), o_pool_s, even_w_out, ms, cfg["tn_out"], w_layer=e)
            xp = _out_proj(xp, o_attn, o_pool, w_out_b, tm_p, cfg["tn_out"])
            outs["diff_s"].append(zs3[:, :ts, 1024:3072].reshape(bs, ts, 2, N_HEADS, DA_V))
            outs["pool_p"].append(zp3[:, seq - POOL_STATE:, 3072:])
            outs["pool_s"].append(jnp.concatenate([state_pool[e], zs3[:, :ts, 3072:]], axis=1)[:, -POOL_STATE:])
        else:
            o = layer // 2
            zp, nsa_kv_p = _norm_matmul(xp, norm_mix[layer], w_in_o[o], min(cfg["tm_odd"], mp), cfg["tn_odd"],
                                        odd_taps, n_odd, o, nsa_kv_p)
            zs, _ = _norm_matmul(xs, norm_mix[layer], w_in_o[o], ms, cfg["tn_odd"])
            zp3 = zp.reshape(bp, seq, -1)
            zs3 = zs.reshape(bs, T8, -1)
            c_kc, c_ks, c_kw = COL_KC * LANES, COL_KS * LANES, COL_KW * LANES
            o_nsa = _nsa_prompt(zp, tile_bias, cmp_bias_t, expand, bp, seq)
            o_conv, tail = _short_conv(zp, jnp.zeros((bp, 8, MIX_WIDTH), F32), conv_w[o], bp, seq, cfg["tm_pool"])
            qg = zs3[:, :, 0:1024].reshape(bs, T8, N_HEADS, NSA_DH).transpose(0, 2, 1, 3).reshape(
                bs, N_HEADS * T8, NSA_DH)
            new_rows = lambda c0: _pad_rows(zs3[:, :, c0:c0 + kvw].reshape(bs, T8 * NSA_ROW, NSA_DH), PAGE_SIZE)
            kvc = _cmp_means(page_table, cache_cmp, o)
            o_cmp, sel = _cmp_select_decode(qg, kvc, zs3[:, :, c_kc:c_kc + kvw], cmp_tile, past_len, ts)
            o_slc = _slc_decode(page_table, qg, sel, expand_dec, new_rows(c_ks), slc_tile, nsa_new_tile, cache_slc, o)
            o_nsa_s = _win_combine_decode(qg, state_win_kv[o].reshape(bs, wbuf * NSA_ROW, NSA_DH), new_rows(c_kw),
                                          win_tile, zs, o_cmp, o_slc)
            prev = jnp.pad(state_conv[o], ((0, 0), (8 - (CONV_WIDTH - 1), 0), (0, 0)))
            o_conv_s, tail_s = _short_conv(zs, prev, conv_w[o], bs, T8, T8)
            xs, w_out_b = _out_proj(xs, o_nsa_s, o_conv_s, odd_w_out, ms, cfg["tn_out"], w_layer=o)
            xp = _out_proj(xp, o_nsa, o_conv, w_out_b, tm_p, cfg["tn_out"])
            kv5 = lambda a, n: a.reshape(a.shape[0], n, 2, NSA_KV, NSA_DH)
            outs["cmp_s"].append(kv5(zs3[:, :ts, c_kc:c_kc + kvw], ts))
            outs["slc_s"].append(kv5(zs3[:, :ts, c_ks:c_ks + kvw], ts))
            n_win = min(WINDOW, past_len + ts)
            outs["win_s"].append(jnp.concatenate([state_win_kv[o], kv5(zs3[:, :ts, c_kw:c_kw + kvw], ts)],
                                                 axis=1)[:, -n_win:])
            outs["conv_p"].append(tail[:, 8 - (CONV_WIDTH - 1):])
            outs["conv_s"].append(tail_s[:, ts - (CONV_WIDTH - 1):ts])
        last = layer == depth - 1
        xs, w1b, w2b = _mlp(xs, norm_mlp[layer], mlp_w1, mlp_w2, norm_final, ms, cfg["tf_cast"], last, layer=layer)
        xp = _mlp(xp, norm_mlp[layer], w1b, w2b, norm_final, min(cfg["tm_mlp"], mp), cfg["tf"], last)

    st = {k: jnp.stack(v) for k, v in outs.items()}
    st["diff_p"] = diff_kv_p[0].reshape(n_even, bp, seq, 2, N_HEADS, DA_V)
    nsa6 = lambda a: a.reshape(n_odd, bp, seq, 2, NSA_KV, NSA_DH)
    st["cmp_p"], st["slc_p"] = nsa6(nsa_kv_p[0]), nsa6(nsa_kv_p[1])
    st["win_p"] = nsa6(nsa_kv_p[2])[:, :, seq - min(WINDOW, seq):]
    return (xp.reshape(bp, seq, d), xs.reshape(bs, T8, d)[:, :ts],
            st["diff_p"], st["diff_s"], st["pool_p"], st["pool_s"], st["cmp_p"], st["cmp_s"],
            st["slc_p"], st["slc_s"], st["win_p"], st["win_s"], st["conv_p"], st["conv_s"])
```

```python
import functools
import math

import jax
import jax.numpy as jnp
import numpy as np
from jax import lax
from jax.experimental import pallas as pl
from jax.experimental.pallas import tpu as pltpu

F32 = jnp.float32
BF16 = jnp.bfloat16

D_MODEL = 2048
N_HEADS = 8
MIX_WIDTH = D_MODEL // 2
DA_QK = 64
DA_V = 128
POOL_WINDOWS = (2, 4, 8, 16)
POOL_GROUP = MIX_WIDTH // 4
POOL_STATE = 15
NSA_DH = 128
NSA_KV = 2
NSA_HPG = 4
CMP_BLOCK = 64
SLC_BLOCK = 64
SLC_TOPK = 16
WINDOW = 512
CONV_WIDTH = 3
NUM_BUCKETS = 32
MAX_DISTANCE = 128
PAGE_SIZE = 128
EPS = 1e-6
NEG = -1e30
MASK_BIG = 2.0 ** 100
ODD_SIZES = (1024, 512, 512, 512, 24, 1024, 1024, 1024)

LANES = 128
VMEM_LIMIT = 56 * 1024 * 1024
TQ = 256
TK = 256


def _cparams(sem):
    return pltpu.CompilerParams(dimension_semantics=sem, vmem_limit_bytes=VMEM_LIMIT)


def _bucket_np(dist):
    n = np.maximum(dist, 0)
    max_exact = NUM_BUCKETS // 2
    nf = np.maximum(n, 1).astype(np.float32)
    large = max_exact + (np.log(nf / np.float32(max_exact)) / np.float32(math.log(MAX_DISTANCE / max_exact))
                         * np.float32(NUM_BUCKETS - max_exact)).astype(np.int32)
    large = np.minimum(large, NUM_BUCKETS - 1)
    return np.where(n < max_exact, n, large).astype(np.int32)


def _bucket_or_masked(dist, valid):
    return np.where(valid, _bucket_np(dist), -1).astype(np.int32)


LOG2E = math.log2(math.e)


def _bias_kernel(table_ref, bucket_ref, o_ref, *, unit):
    h = pl.program_id(0)
    bt = bucket_ref[...]
    far = table_ref[NUM_BUCKETS - 1, h]
    acc = jnp.zeros(bt.shape, F32)
    for b in range(NUM_BUCKETS - 1):
        acc = jnp.where(bt == b, (table_ref[b, h] - far) * unit, acc)
    o_ref[...] = jnp.where(bt < 0, NEG, acc)


def _bias_tiles(table, buckets, unit=1.0):
    shp = buckets.shape
    flat = jnp.asarray(buckets.reshape(-1, shp[-1]))
    rows = flat.shape[0]
    out = pl.pallas_call(
        functools.partial(_bias_kernel, unit=unit),
        grid=(N_HEADS,),
        in_specs=[pl.BlockSpec(memory_space=pltpu.SMEM),
                  pl.BlockSpec((rows, shp[-1]), lambda h: (0, 0))],
        out_specs=pl.BlockSpec((None, rows, shp[-1]), lambda h: (h, 0, 0)),
        out_shape=jax.ShapeDtypeStruct((N_HEADS, rows, shp[-1]), F32),
        compiler_params=_cparams(("arbitrary",)),
        name="bias_tiles",
    )(table, flat)
    return out.reshape((N_HEADS,) + shp)


def _rms_rows(x, g):
    y = x * lax.rsqrt(jnp.mean(x * x, axis=-1, keepdims=True) + EPS)
    return y * g


def _norm_matmul_kernel(x_ref, g_ref, w_ref, *rest, taps, n_alias, emit):
    o_ref = rest[n_alias]
    tap_refs = rest[n_alias + 1:n_alias + 1 + len(taps)]
    h_scr = rest[-1]
    j = pl.program_id(1)
    tm, tn = o_ref.shape

    @pl.when(j == 0)
    def _():
        h_scr[...] = _rms_rows(x_ref[...], g_ref[...]).astype(BF16)

    wb = w_ref[...].astype(BF16)
    if emit:
        rest[n_alias + 1 + len(taps)][...] = wb
    acc = jnp.dot(h_scr[...], wb, preferred_element_type=F32)
    o_ref[...] = acc

    def whole_blocks(t_ref, col0, width):
        @pl.when((j >= col0 // tn) & (j < (col0 + width) // tn))
        def _():
            t_ref[...] = acc

    def token_rows(t_ref, col0, per_tok):
        loc = col0 % tn

        @pl.when(j == col0 // tn)
        def _():
            for c in range(per_tok):
                t_ref[pl.ds(c, tm, stride=per_tok), :] = acc[:, loc + c * LANES:loc + (c + 1) * LANES]

    for (col0, width, per_tok), t_ref in zip(taps, tap_refs):
        if per_tok == 1:
            whole_blocks(t_ref, col0, width)
        else:
            token_rows(t_ref, col0, per_tok)


def _norm_matmul(x, g, w, tm, tn, taps=(), stack=1, slot=0, prev=(), w_layer=None):
    m, d = x.shape
    n = w.shape[-1]
    emit = w_layer is not None
    assert m % tm == 0 and n % tn == 0 and (not prev or len(prev) == len(taps)) and (not emit or m == tm)
    out_specs = [pl.BlockSpec((tm, tn), lambda i, j: (i, j))]
    out_shape = [jax.ShapeDtypeStruct((m, n), F32)]
    for col0, width, per_tok in taps:
        if per_tok == 1:
            assert col0 % tn == 0 and width % tn == 0
            b0, nb = col0 // tn, width // tn
            out_specs.append(pl.BlockSpec((None, tm, tn), lambda i, j, b0=b0, nb=nb: (slot, i, jnp.clip(j - b0, 0, nb - 1))))
            out_shape.append(jax.ShapeDtypeStruct((stack, m, width), F32))
        else:
            assert width == per_tok * LANES and col0 // tn == (col0 + width - 1) // tn
            out_specs.append(pl.BlockSpec((None, tm * per_tok, LANES), lambda i, j: (slot, i, 0)))
            out_shape.append(jax.ShapeDtypeStruct((stack, m * per_tok, LANES), F32))
    n_alias = len(prev)
    if emit:
        w_spec = pl.BlockSpec((None, d, tn), lambda i, j: (w_layer, 0, j))
        out_specs.append(pl.BlockSpec((d, tn), lambda i, j: (0, j)))
        out_shape.append(jax.ShapeDtypeStruct((d, n), BF16))
    else:
        w_spec = pl.BlockSpec((d, tn), lambda i, j: (0, j))
    outs = pl.pallas_call(
        functools.partial(_norm_matmul_kernel, taps=tuple(taps), n_alias=n_alias, emit=emit),
        grid=(m // tm, n // tn),
        in_specs=[pl.BlockSpec((tm, d), lambda i, j: (i, 0)),
                  pl.BlockSpec((1, d), lambda i, j: (0, 0)),
                  w_spec]
        + [pl.BlockSpec(memory_space=pl.ANY)] * n_alias,
        out_specs=out_specs,
        out_shape=out_shape,
        input_output_aliases={3 + k: 1 + k for k in range(n_alias)},
        scratch_shapes=[pltpu.VMEM((tm, d), BF16)],
        compiler_params=_cparams(("parallel", "arbitrary")),
        name="norm_matmul",
    )(x, g.reshape(1, d), w, *prev)
    return outs[0], tuple(outs[1:])


def _out_proj_kernel(x_ref, a_ref, b_ref, wa_ref, wb_ref, o_ref, *rest, emit):
    wa = wa_ref[...].astype(BF16)
    wb = wb_ref[...].astype(BF16)
    if emit:
        ka = wa.shape[0]
        rest[0][0:ka, :] = wa
        rest[0][ka:2 * ka, :] = wb
    acc = jnp.dot(a_ref[...], wa, preferred_element_type=F32)
    acc = acc + jnp.dot(b_ref[...], wb, preferred_element_type=F32)
    o_ref[...] = x_ref[...] + acc


def _out_proj(x, a, b, w, tm, tn, w_layer=None):
    m, d = x.shape
    ka = a.shape[1]
    emit = w_layer is not None
    assert m % tm == 0 and d % tn == 0 and (not emit or m == tm)
    if emit:
        w_specs = [pl.BlockSpec((None, ka, tn), lambda i, j: (w_layer, 0, j)),
                   pl.BlockSpec((None, ka, tn), lambda i, j: (w_layer, 1, j))]
    else:
        w_specs = [pl.BlockSpec((ka, tn), lambda i, j: (0, j)), pl.BlockSpec((ka, tn), lambda i, j: (1, j))]
    out_specs = [pl.BlockSpec((tm, tn), lambda i, j: (i, j))]
    out_shape = [jax.ShapeDtypeStruct((m, d), F32)]
    if emit:
        out_specs.append(pl.BlockSpec((2 * ka, tn), lambda i, j: (0, j)))
        out_shape.append(jax.ShapeDtypeStruct((2 * ka, d), BF16))
    outs = pl.pallas_call(
        functools.partial(_out_proj_kernel, emit=emit),
        grid=(m // tm, d // tn),
        in_specs=[pl.BlockSpec((tm, tn), lambda i, j: (i, j)),
                  pl.BlockSpec((tm, ka), lambda i, j: (i, 0)),
                  pl.BlockSpec((tm, ka), lambda i, j: (i, 0))] + w_specs,
        out_specs=out_specs,
        out_shape=out_shape,
        compiler_params=_cparams(("parallel", "arbitrary")),
        name="out_proj",
    )(x, a, b, w, w)
    return outs if emit else outs[0]


def _mlp_kernel(x_ref, g_ref, w1_ref, w2_ref, gf_ref, o_ref, *rest, final_norm, emit):
    h_scr = rest[-1]
    f = pl.program_id(1)

    @pl.when(f == 0)
    def _():
        x = x_ref[...]
        h_scr[...] = _rms_rows(x, g_ref[...]).astype(BF16)
        o_ref[...] = x

    w1b = w1_ref[...].astype(BF16)
    w2b = w2_ref[...].astype(BF16)
    if emit:
        rest[0][...] = w1b
        rest[1][...] = w2b
    a = jnp.maximum(jnp.dot(h_scr[...], w1b, preferred_element_type=F32), 0.0)
    a = (a * a).astype(BF16)
    o_ref[...] += jnp.dot(a, w2b, preferred_element_type=F32)

    if final_norm:
        @pl.when(f == pl.num_programs(1) - 1)
        def _():
            o_ref[...] = _rms_rows(o_ref[...], gf_ref[...])


def _mlp(x, g, w1, w2, gf, tm, tf, final_norm, layer=None):
    m, d = x.shape
    ff = w1.shape[-1]
    emit = layer is not None
    assert m % tm == 0 and ff % tf == 0 and (not emit or m == tm)
    if emit:
        w_specs = [pl.BlockSpec((None, d, tf), lambda i, f: (layer, 0, f)),
                   pl.BlockSpec((None, tf, d), lambda i, f: (layer, f, 0))]
    else:
        w_specs = [pl.BlockSpec((d, tf), lambda i, f: (0, f)), pl.BlockSpec((tf, d), lambda i, f: (f, 0))]
    out_specs = [pl.BlockSpec((tm, d), lambda i, f: (i, 0))]
    out_shape = [jax.ShapeDtypeStruct((m, d), F32)]
    if emit:
        out_specs += [pl.BlockSpec((d, tf), lambda i, f: (0, f)), pl.BlockSpec((tf, d), lambda i, f: (f, 0))]
        out_shape += [jax.ShapeDtypeStruct((d, ff), BF16), jax.ShapeDtypeStruct((ff, d), BF16)]
    outs = pl.pallas_call(
        functools.partial(_mlp_kernel, final_norm=final_norm, emit=emit),
        grid=(m // tm, ff // tf),
        in_specs=[pl.BlockSpec((tm, d), lambda i, f: (i, 0)),
                  pl.BlockSpec((1, d), lambda i, f: (0, 0))] + w_specs
        + [pl.BlockSpec((1, d), lambda i, f: (0, 0))],
        out_specs=out_specs,
        out_shape=out_shape,
        scratch_shapes=[pltpu.VMEM((tm, d), BF16)],
        compiler_params=_cparams(("parallel", "arbitrary")),
        name="mlp",
    )(x, g.reshape(1, d), w1, w2, gf.reshape(1, d))
    return outs if emit else outs[0]


def _flash_init(m_scr, l_scr, acc_scr):
    m_scr[...] = jnp.full(m_scr.shape, NEG, F32)
    l_scr[...] = jnp.zeros(l_scr.shape, F32)
    acc_scr[...] = jnp.zeros(acc_scr.shape, F32)


def _flash_tile(qs, kt, vt, m_scr, l_scr, acc_scr, *, heads, bias=None, mask_add=None):
    s = lax.dot_general(qs, kt, (((1,), (1,)), ((), ())), preferred_element_type=F32)
    rows, tk = s.shape
    if bias is not None or mask_add is not None:
        s3 = s.reshape(heads, rows // heads, tk)
        if bias is not None:
            s3 = s3 + bias
        if mask_add is not None:
            s3 = s3 + mask_add[None]
        s = s3.reshape(rows, tk)
    m_prev = m_scr[...]
    m_next = jnp.maximum(m_prev, jnp.max(s, axis=1, keepdims=True))
    p = jnp.exp2(s - jnp.tile(m_next, (1, tk // LANES)))
    alpha = jnp.exp2(m_prev - m_next)
    v_ones = jnp.concatenate([vt, jnp.ones((tk, LANES), BF16)], axis=1)
    pv = jnp.dot(p.astype(BF16), v_ones, preferred_element_type=F32)
    l_scr[...] = alpha * l_scr[...] + pv[:, LANES:]
    acc_scr[...] = alpha * acc_scr[...] + pv[:, :LANES]
    m_scr[...] = m_next


def _kv_tile(k_ref, v_ref, j, tk):
    off = pl.multiple_of(j * tk, tk)
    return k_ref[pl.ds(off, tk), :].astype(BF16), v_ref[pl.ds(off, tk), :].astype(BF16)


def _diff_lambda_in_kernel(lv_ref, lam_init):
    lv = lv_ref[...]
    a = jnp.sum(lv[0:1] * lv[1:2], axis=1, keepdims=True)
    b = jnp.sum(lv[2:3] * lv[3:4], axis=1, keepdims=True)
    return jnp.exp(a) - jnp.exp(b) + lam_init


DIFF_HEADS_PER_STEP = 4


def _diff_attn_kernel(lv_ref, q_ref, k_ref, v_ref, bias_ref, g_ref, o_ref, m_scr, l_scr, acc_scr, *, lam_init):
    qi = pl.program_id(2)
    tq = q_ref.shape[0]
    tk = bias_ref.shape[-1]
    hp = q_ref.shape[1] // DA_V
    lane = lax.broadcasted_iota(jnp.int32, (tq, DA_V), 1)
    qs = []
    for h in range(hp):
        q = q_ref[:, h * DA_V:(h + 1) * DA_V] * (DA_QK ** -0.5 * LOG2E)
        qs.append(jnp.concatenate([jnp.where(lane < DA_QK, q, 0.0), jnp.where(lane >= DA_QK, q, 0.0)],
                                  axis=0).astype(BF16))
    _flash_init(m_scr, l_scr, acc_scr)

    def tiles(j, which):
        off = pl.multiple_of(j * tk, tk)
        for h in range(hp):
            cols = slice(h * DA_V, (h + 1) * DA_V)
            kt = k_ref[pl.ds(off, tk), cols].astype(BF16)
            vt = v_ref[pl.ds(off, tk), cols].astype(BF16)
            _flash_tile(qs[h], kt, vt, m_scr.at[h], l_scr.at[h], acc_scr.at[h], heads=2,
                        bias=None if which is None else bias_ref[h, which][None])

    def far(j, c):
        tiles(j, None)
        return c

    lax.fori_loop(0, jnp.maximum(qi - 1, 0), far, 0)

    @pl.when(qi == 0)
    def _():
        tiles(qi, 0)

    @pl.when(qi >= 1)
    def _():
        tiles(qi - 1, 1)
        tiles(qi, 0)

    lam = _diff_lambda_in_kernel(lv_ref, lam_init)
    for h in range(hp):
        o = acc_scr[h] / l_scr[h]
        a = o[:tq] - lam * o[tq:]
        o_ref[:, h * DA_V:(h + 1) * DA_V] = (_rms_rows(a, g_ref[...]) * (1.0 - lam_init)).astype(BF16)


def _diff_attn_prompt(z, lam_vec, subln_g, bias, lam_init, batch, seq):
    nq = seq // TQ
    hp = DIFF_HEADS_PER_STEP
    groups = N_HEADS // hp
    wide = hp * DA_V
    return pl.pallas_call(
        functools.partial(_diff_attn_kernel, lam_init=lam_init),
        grid=(batch, groups, nq),
        in_specs=[pl.BlockSpec((4, DA_QK), lambda b, h, i: (0, 0)),
                  pl.BlockSpec((TQ, wide), lambda b, h, i: (b * nq + i, h)),
                  pl.BlockSpec((seq, wide), lambda b, h, i: (b, groups + h)),
                  pl.BlockSpec((seq, wide), lambda b, h, i: (b, 2 * groups + h)),
                  pl.BlockSpec((hp, bias.shape[1], TQ, TK), lambda b, h, i: (h, 0, 0, 0)),
                  pl.BlockSpec((1, DA_V), lambda b, h, i: (0, 0))],
        out_specs=pl.BlockSpec((TQ, wide), lambda b, h, i: (b * nq + i, h)),
        out_shape=jax.ShapeDtypeStruct((batch * seq, N_HEADS * DA_V), BF16),
        scratch_shapes=[pltpu.VMEM((hp, 2 * TQ, LANES), F32)] * 3,
        compiler_params=_cparams(("parallel", "parallel", "arbitrary")),
        name="diff_attn_prompt",
    )(lam_vec, z, z, z, bias, subln_g.reshape(1, DA_V))


def _toeplitz_buckets(tq, tk):
    i = np.arange(tq)[:, None]
    j = np.arange(tk)[None, :]
    diag = _bucket_or_masked(i - j, i >= j)
    sub = _bucket_or_masked(tk + i - j, np.ones((tq, tk), bool))
    return np.stack([diag, sub])


HALO = 16


def _pool_kernel(u_ref, halo_ref, prev_ref, w_ref, scale_ref, o_ref, ext_scr, *, pos0):
    i = pl.program_id(1)
    tm = u_ref.shape[0]
    ext_scr[0:HALO, :] = jnp.where(i == 0, prev_ref[...], halo_ref[...])
    ext_scr[HALO:HALO + tm, :] = u_ref[...]
    pos = pos0 + i * tm + lax.broadcasted_iota(jnp.int32, (tm, 1), 0)
    outs = []
    for g, w in enumerate(POOL_WINDOWS):
        c0, c1 = g * POOL_GROUP, (g + 1) * POOL_GROUP
        x0 = ext_scr[HALO:HALO + tm, c0:c1]
        win = x0
        for k in range(1, w):
            win = win + ext_scr[HALO - k:HALO - k + tm, c0:c1]
        cnt = jnp.minimum(pos + 1, w).astype(F32)
        d = win / cnt - x0
        outs.append(jnp.dot(d.astype(BF16), w_ref[g], preferred_element_type=F32))
    o_ref[...] = (jnp.concatenate(outs, axis=-1) * scale_ref[...]).astype(BF16)


def _pool_mix(z, prev, w_pool, scale, pos0, batch, seq, tm):
    nb = seq // tm
    ucol = 3
    return pl.pallas_call(
        functools.partial(_pool_kernel, pos0=pos0),
        grid=(batch, nb),
        in_specs=[pl.BlockSpec((tm, MIX_WIDTH), lambda b, i: (b * nb + i, ucol)),
                  pl.BlockSpec((HALO, MIX_WIDTH),
                               lambda b, i: (jnp.maximum((b * nb + i) * (tm // HALO) - 1, 0), ucol)),
                  pl.BlockSpec((None, HALO, MIX_WIDTH), lambda b, i: (b, 0, 0)),
                  pl.BlockSpec((4, POOL_GROUP, POOL_GROUP), lambda b, i: (0, 0, 0)),
                  pl.BlockSpec((1, MIX_WIDTH), lambda b, i: (0, 0))],
        out_specs=pl.BlockSpec((tm, MIX_WIDTH), lambda b, i: (b * nb + i, 0)),
        out_shape=jax.ShapeDtypeStruct((batch * seq, MIX_WIDTH), BF16),
        scratch_shapes=[pltpu.VMEM((HALO + tm, MIX_WIDTH), F32)],
        compiler_params=_cparams(("parallel", "arbitrary")),
        name="pool_mix",
    )(z, z, prev, w_pool, scale.reshape(1, MIX_WIDTH))


COL_CB, COL_CC, COL_CH, COL_KC, COL_KS, COL_KW, COL_GATE = 8, 16, 24, 32, 36, 40, 44


def _odd_weight_layout(w_in):
    offs = np.cumsum((0,) + ODD_SIZES)
    q, kc, ks_, kw, gates, cb, cc, ch = [w_in[:, offs[i]:offs[i + 1]] for i in range(8)]
    pad = jnp.zeros((w_in.shape[0], LANES - ODD_SIZES[4]), w_in.dtype)
    return jnp.concatenate([q, cb, cc, ch, kc, ks_, kw, gates, pad], axis=1)


def _select_blocks(score_t, q0, nb):
    nbp, tq = score_t.shape
    blk = lax.broadcasted_iota(jnp.int32, (nbp, tq), 0)
    cur = (q0 + lax.broadcasted_iota(jnp.int32, (nbp, tq), 1)) // SLC_BLOCK
    forced = (blk == 0) | (blk == cur) | (blk == cur - 1)
    future = (blk > cur) | (blk >= nb)
    s = jnp.where(forced, jnp.inf, jnp.where(future, -jnp.inf, score_t))
    cnt = jnp.zeros((nbp, tq), jnp.int32)
    for n in range(nb):
        row = s[n:n + 1, :]
        beats = (row > s) | ((row == s) & (blk > n))
        cnt = cnt + beats.astype(jnp.int32)
    return (cnt < min(SLC_TOPK, nb)).astype(F32)


def _nsa_prompt_kernel(q_ref, kc_ref, vc_ref, ks_ref, vs_ref, kw_ref, vw_ref, gate_ref, tb_ref, cb_ref, e_ref,
                       o_ref, kcm_scr, vcm_scr, m_scr, l_scr, acc_scr):
    g = pl.program_id(1)
    qi = pl.program_id(2)
    tq = q_ref.shape[0]
    tk = tb_ref.shape[-1]
    seq = kc_ref.shape[0]
    nb = seq // CMP_BLOCK
    nbp = -(-nb // 8) * 8
    scale = NSA_DH ** -0.5
    hp = NSA_HPG

    @pl.when(qi == 0)
    def _():
        kcm_scr[...] = jnp.zeros(kcm_scr.shape, BF16)
        vcm_scr[...] = jnp.zeros(vcm_scr.shape, BF16)
        kcm_scr[0:nb, :] = (jnp.sum(kc_ref[...].reshape(nb, CMP_BLOCK, NSA_DH), axis=1) / CMP_BLOCK).astype(BF16)
        vcm_scr[0:nb, :] = (jnp.sum(vc_ref[...].reshape(nb, CMP_BLOCK, NSA_DH), axis=1) / CMP_BLOCK).astype(BF16)

    qh = [q_ref[:, h * NSA_DH:(h + 1) * NSA_DH].astype(BF16) for h in range(hp)]
    qs = jnp.concatenate([(q_ref[:, h * NSA_DH:(h + 1) * NSA_DH] * (scale * LOG2E)).astype(BF16)
                          for h in range(hp)], axis=0)

    kcm = kcm_scr[...]
    vcm = vcm_scr[...]
    o_cmp = []
    p_grp_t = jnp.zeros((LANES, tq), F32)
    for h in range(hp):
        lt = lax.dot_general(kcm, qh[h], (((1,), (1,)), ((), ())), preferred_element_type=F32) * scale
        bt = cb_ref[h]
        lt = lt + bt
        p = jnp.exp(lt - jnp.max(lt, axis=0, keepdims=True))
        p = p / jnp.sum(p, axis=0, keepdims=True)
        p = jnp.where(bt > 0.5 * NEG, p, 0.0)
        p_grp_t = p_grp_t + p
        o_cmp.append(jnp.dot(p.T.astype(BF16), vcm, preferred_element_type=F32))

    sel_t = _select_blocks(p_grp_t[0:nbp, :], qi * tq, nb)
    if nbp < LANES:
        sel_t = jnp.concatenate([sel_t, jnp.zeros((LANES - nbp, tq), F32)], axis=0)
    sel_neg = ((sel_t.T - 1.0) * MASK_BIG).astype(BF16)

    def sel_mask(j):
        return jnp.dot(sel_neg, e_ref[j], preferred_element_type=F32)

    def slc_tile(j, which):
        kt, vt = _kv_tile(ks_ref, vs_ref, j, tk)
        _flash_tile(qs, kt, vt, m_scr.at[0], l_scr.at[0], acc_scr.at[0], heads=hp,
                    bias=None if which is None else tb_ref[:, which], mask_add=sel_mask(j))

    def win_tile(j, which):
        kt, vt = _kv_tile(kw_ref, vw_ref, j, tk)
        _flash_tile(qs, kt, vt, m_scr.at[1], l_scr.at[1], acc_scr.at[1], heads=hp, bias=tb_ref[:, which])

    _flash_init(m_scr, l_scr, acc_scr)
    n_far = jnp.maximum(qi - 1, 0)

    def far_pair(jj, c):
        slc_tile(2 * jj, None)
        slc_tile(2 * jj + 1, None)
        return c

    lax.fori_loop(0, n_far // 2, far_pair, 0)

    @pl.when(n_far % 2 == 1)
    def _():
        slc_tile(n_far - 1, None)

    def near_tiles(before):
        if before >= 2:
            win_tile(qi - 2, 2)
        if before >= 1:
            slc_tile(qi - 1, 1)
            win_tile(qi - 1, 1)
        slc_tile(qi, 0)
        win_tile(qi, 0)

    @pl.when(qi == 0)
    def _():
        near_tiles(0)

    @pl.when(qi == 1)
    def _():
        near_tiles(1)

    @pl.when(qi >= 2)
    def _():
        near_tiles(2)

    o_slc = acc_scr[0] / l_scr[0]
    o_win = acc_scr[1] / l_scr[1]

    sig = jax.nn.sigmoid(gate_ref[...])
    lane = lax.broadcasted_iota(jnp.int32, sig.shape, 1)

    def gate(branch, h):
        col = branch * N_HEADS + g * hp + h
        return jnp.sum(jnp.where(lane == col, sig, 0.0), axis=1, keepdims=True)

    for h in range(hp):
        rows = slice(h * tq, (h + 1) * tq)
        o = gate(0, h) * o_cmp[h] + gate(1, h) * o_slc[rows] + gate(2, h) * o_win[rows]
        o_ref[:, h * NSA_DH:(h + 1) * NSA_DH] = o.astype(BF16)


def _nsa_buckets(seq, tq, tk):
    i = np.arange(tq)[:, None]
    j = np.arange(tk)[None, :]
    toe = _toeplitz_buckets(tq, tk)
    assert WINDOW == 2 * tk and tq == tk
    win2 = np.where(i < j, NUM_BUCKETS - 1, -1).astype(np.int32)
    tiles = np.concatenate([toe, win2[None]])
    nb = seq // CMP_BLOCK
    blk_end = (np.arange(LANES)[:, None] + 1) * CMP_BLOCK - 1
    dist = np.arange(seq)[None, :] - blk_end
    cmp_t = _bucket_or_masked(dist, (dist >= 0) & (np.arange(LANES)[:, None] < nb))
    nk = seq // tk
    key_blk = (np.arange(nk)[:, None, None] * tk + np.arange(tk)[None, None, :]) // SLC_BLOCK
    expand = (key_blk == np.arange(LANES)[None, :, None]).astype(np.float32)
    return tiles, cmp_t, expand


def _nsa_prompt(z, tile_bias, cmp_bias_t, expand, batch, seq):
    nq = seq // TQ
    nk = seq // TK
    kv = lambda col: pl.BlockSpec((seq, NSA_DH), lambda b, g, i: (b, col + g))
    return pl.pallas_call(
        _nsa_prompt_kernel,
        grid=(batch, NSA_KV, nq),
        in_specs=[pl.BlockSpec((TQ, NSA_HPG * NSA_DH), lambda b, g, i: (b * nq + i, g)),
                  kv(COL_KC), kv(COL_KC + 2), kv(COL_KS), kv(COL_KS + 2), kv(COL_KW), kv(COL_KW + 2),
                  pl.BlockSpec((TQ, LANES), lambda b, g, i: (b * nq + i, COL_GATE)),
                  pl.BlockSpec((NSA_HPG, 3, TQ, TK), lambda b, g, i: (g, 0, 0, 0)),
                  pl.BlockSpec((NSA_HPG, LANES, TQ), lambda b, g, i: (g, 0, i)),
                  pl.BlockSpec((nk, LANES, TK), lambda b, g, i: (0, 0, 0))],
        out_specs=pl.BlockSpec((TQ, NSA_HPG * NSA_DH), lambda b, g, i: (b * nq + i, g)),
        out_shape=jax.ShapeDtypeStruct((batch * seq, MIX_WIDTH), BF16),
        scratch_shapes=[pltpu.VMEM((LANES, NSA_DH), BF16), pltpu.VMEM((LANES, NSA_DH), BF16)]
        + [pltpu.VMEM((2, NSA_HPG * TQ, LANES), F32)] * 3,
        compiler_params=_cparams(("parallel", "parallel", "arbitrary")),
        name="nsa_prompt",
    )(z, z, z, z, z, z, z, z, tile_bias, cmp_bias_t, expand)


def _conv_kernel(cb_ref, cc_ref, ch_ref, hc_ref, hh_ref, prev_ref, w_ref, o_ref, tail_ref, ext_scr):
    i = pl.program_id(1)
    tm = cb_ref.shape[0]
    e = cc_ref[...] * ch_ref[...]
    ext_scr[0:8, :] = jnp.where(i == 0, prev_ref[...], hc_ref[...] * hh_ref[...])
    ext_scr[8:8 + tm, :] = e
    w = w_ref[...]
    y = w[0:1] * ext_scr[6:6 + tm, :]
    y = y + w[1:2] * ext_scr[7:7 + tm, :]
    y = y + w[2:3] * e
    o_ref[...] = (cb_ref[...] * y).astype(BF16)

    @pl.when(i == pl.num_programs(1) - 1)
    def _():
        tail_ref[...] = e[tm - 8:tm, :]


def _short_conv(z, prev, w_conv, batch, seq, tm):
    nb = seq // tm
    cw = MIX_WIDTH // LANES
    blk = lambda c: pl.BlockSpec((tm, MIX_WIDTH), lambda b, i: (b * nb + i, c // cw))
    halo = lambda c: pl.BlockSpec((8, MIX_WIDTH), lambda b, i: (jnp.maximum((b * nb + i) * (tm // 8) - 1, 0), c // cw))
    return pl.pallas_call(
        _conv_kernel,
        grid=(batch, nb),
        in_specs=[blk(COL_CB), blk(COL_CC), blk(COL_CH), halo(COL_CC), halo(COL_CH),
                  pl.BlockSpec((None, 8, MIX_WIDTH), lambda b, i: (b, 0, 0)),
                  pl.BlockSpec((8, MIX_WIDTH), lambda b, i: (0, 0))],
        out_specs=[pl.BlockSpec((tm, MIX_WIDTH), lambda b, i: (b * nb + i, 0)),
                   pl.BlockSpec((None, 8, MIX_WIDTH), lambda b, i: (b, 0, 0))],
        out_shape=[jax.ShapeDtypeStruct((batch * seq, MIX_WIDTH), BF16),
                   jax.ShapeDtypeStruct((batch, 8, MIX_WIDTH), F32)],
        scratch_shapes=[pltpu.VMEM((8 + tm, MIX_WIDTH), F32)],
        compiler_params=_cparams(("parallel", "arbitrary")),
        name="short_conv",
    )(z, z, z, z, z, prev, jnp.pad(w_conv, ((0, 8 - CONV_WIDTH), (0, 0))))


T8 = 8
NEW_ROWS = 16
DIFF_PAGES_PER_STEP = 8
NSA_PAGES_PER_STEP = 16
NSA_ROW = 2 * NSA_KV
V_SHIFT = NSA_KV
NB_DEC = 288


def _nt(a, b):
    return lax.dot_general(a, b, (((1,), (1,)), ((), ())), preferred_element_type=F32)


def _softmax_update(st, v_tiles, width, m_ref, l_ref, acc_ref):
    m_prev = m_ref[...]
    m_next = jnp.maximum(m_prev, jnp.max(st, axis=1, keepdims=True))
    p = jnp.exp(st - jnp.tile(m_next, (1, st.shape[1] // LANES)))
    alpha = jnp.exp(m_prev - m_next)
    l_ref[...] = alpha * l_ref[...] + jnp.sum(p, axis=1, keepdims=True)
    pv = jnp.zeros(acc_ref.shape, F32)
    for k, vt in enumerate(v_tiles):
        pv = pv + jnp.dot(p[:, k * width:(k + 1) * width].astype(BF16), vt, preferred_element_type=F32)
    acc_ref[...] = alpha * acc_ref[...] + pv
    m_ref[...] = m_next


def _diff_decode_kernel(pt_ref, q_ref, new_ref, mask_ref, newmask_ref, lv_ref, g_ref, *rest, lam_init):
    pages = rest[:DIFF_PAGES_PER_STEP]
    o_ref, m_scr, l_scr, acc_scr = rest[DIFF_PAGES_PER_STEP:]
    s = pl.program_id(1)
    last = pl.num_programs(1) - 1
    width = PAGE_SIZE * N_HEADS

    @pl.when(s == 0)
    def _():
        _flash_init(m_scr, l_scr, acc_scr)

    q = q_ref[...] * (DA_QK ** -0.5)
    lane = lax.broadcasted_iota(jnp.int32, q.shape, 1)
    qs = jnp.concatenate([jnp.where(lane < DA_QK, q, 0.0), jnp.where(lane >= DA_QK, q, 0.0)], axis=0).astype(BF16)
    upd = functools.partial(_softmax_update, m_ref=m_scr, l_ref=l_scr, acc_ref=acc_scr)

    is_last = s == last
    tiles, values = [], []
    for k, pg in enumerate(pages):
        st = _nt(qs, pg[:, 0].reshape(width, DA_V).astype(BF16))
        if k == DIFF_PAGES_PER_STEP - 1:
            st = st + jnp.where(is_last, mask_ref[1], mask_ref[0])
        else:
            st = st + mask_ref[0]
        tiles.append(st)
        values.append(pg[:, 1].reshape(width, DA_V).astype(BF16))
    upd(jnp.concatenate(tiles, axis=1), values, width)

    @pl.when(is_last)
    def _():
        nw = NEW_ROWS * N_HEADS
        st = _nt(qs, new_ref[:, 0].reshape(nw, DA_V).astype(BF16)) + newmask_ref[...]
        upd(st, [new_ref[:, 1].reshape(nw, DA_V).astype(BF16)], nw)
        o = acc_scr[...] / l_scr[...]
        half = o.shape[0] // 2
        a = o[:half] - _diff_lambda_in_kernel(lv_ref, lam_init) * o[half:]
        o_ref[...] = _rms_rows(a, g_ref[...]) * (1.0 - lam_init)


def _diff_decode(page_table, q_ht, kv_new, masks, new_mask, lam_vec, subln_g, cache, layer, lam_init):
    batch, n_pages = page_table.shape
    steps = n_pages // DIFF_PAGES_PER_STEP
    rows = q_ht.shape[1]

    def page_spec(k):
        return pl.BlockSpec((None, None, PAGE_SIZE, 2, N_HEADS, DA_V),
                            lambda b, s, pt: (layer, pt[b, s * DIFF_PAGES_PER_STEP + k], 0, 0, 0, 0))

    grid_spec = pltpu.PrefetchScalarGridSpec(
        num_scalar_prefetch=1,
        grid=(batch, steps),
        in_specs=[pl.BlockSpec((None, rows, DA_V), lambda b, s, pt: (b, 0, 0)),
                  pl.BlockSpec((None, NEW_ROWS, 2, N_HEADS, DA_V), lambda b, s, pt: (b, 0, 0, 0, 0)),
                  pl.BlockSpec(masks.shape, lambda b, s, pt: (0, 0, 0)),
                  pl.BlockSpec(new_mask.shape, lambda b, s, pt: (0, 0)),
                  pl.BlockSpec((4, DA_QK), lambda b, s, pt: (0, 0)),
                  pl.BlockSpec((1, DA_V), lambda b, s, pt: (0, 0))]
        + [page_spec(k) for k in range(DIFF_PAGES_PER_STEP)],
        out_specs=pl.BlockSpec((None, rows, DA_V), lambda b, s, pt: (b, 0, 0)),
        scratch_shapes=[pltpu.VMEM((2 * rows, LANES), F32)] * 3,
    )
    return pl.pallas_call(
        functools.partial(_diff_decode_kernel, lam_init=lam_init),
        grid_spec=grid_spec,
        out_shape=jax.ShapeDtypeStruct((batch, rows, DA_V), F32),
        compiler_params=_cparams(("parallel", "arbitrary")),
        name="diff_decode",
    )(page_table, q_ht, kv_new, masks, new_mask, lam_vec, subln_g.reshape(1, DA_V),
      *([cache] * DIFF_PAGES_PER_STEP))


def _nsa_page_specs(layer):
    def spec(k):
        return pl.BlockSpec((None, None, PAGE_SIZE * NSA_ROW, NSA_DH),
                            lambda b, s, pt: (layer, pt[b, s * NSA_PAGES_PER_STEP + k], 0, 0))
    return [spec(k) for k in range(NSA_PAGES_PER_STEP)]


def _cmp_means_kernel(pt_ref, *rest):
    pages = rest[:NSA_PAGES_PER_STEP]
    o_ref = rest[NSA_PAGES_PER_STEP]
    per_blk = CMP_BLOCK * NSA_ROW // 8
    low = lax.broadcasted_iota(jnp.int32, (8, NSA_DH), 0) < NSA_ROW
    out = []
    for pg in pages:
        x = pg[...].reshape(PAGE_SIZE * NSA_ROW // 8, 8, NSA_DH)
        t0 = jnp.sum(x[0:per_blk], axis=0)
        t1 = jnp.sum(x[per_blk:2 * per_blk], axis=0)
        t0 = t0 + pltpu.roll(t0, NSA_ROW, axis=0)
        t1 = t1 + pltpu.roll(t1, NSA_ROW, axis=0)
        out.append(jnp.where(low, t0, t1) / CMP_BLOCK)
    o_ref[...] = jnp.concatenate(out, axis=0)


def _cmp_means(page_table, cache):
    batch, n_pages = page_table.shape
    layers = cache.shape[0]
    steps = n_pages // NSA_PAGES_PER_STEP
    per_step = NSA_PAGES_PER_STEP * 8

    def spec(k):
        return pl.BlockSpec((None, None, PAGE_SIZE * NSA_ROW, NSA_DH),
                            lambda l, b, s, pt: (l, pt[b, s * NSA_PAGES_PER_STEP + k], 0, 0))

    grid_spec = pltpu.PrefetchScalarGridSpec(
        num_scalar_prefetch=1,
        grid=(layers, batch, steps),
        in_specs=[spec(k) for k in range(NSA_PAGES_PER_STEP)],
        out_specs=pl.BlockSpec((None, None, per_step, NSA_DH), lambda l, b, s, pt: (l, b, s, 0)),
    )
    return pl.pallas_call(
        _cmp_means_kernel,
        grid_spec=grid_spec,
        out_shape=jax.ShapeDtypeStruct((layers, batch, steps * per_step, NSA_DH), F32),
        compiler_params=_cparams(("parallel", "parallel", "arbitrary")),
        name="cmp_means",
    )(page_table, *([cache] * NSA_PAGES_PER_STEP))


def _cmp_select_decode_kernel(q_ref, kvc_ref, new_ref, tile_ref, ocmp_ref, sel_ref, k_scr, *, past_len, n_new):
    n_past = kvc_ref.shape[0]
    width = NB_DEC * NSA_ROW
    scale = NSA_DH ** -0.5
    k_scr[...] = jnp.zeros(k_scr.shape, F32)
    k_scr[0:n_past, :] = kvc_ref[...]
    real = lax.broadcasted_iota(jnp.int32, new_ref.shape, 0) < n_new
    tot = jnp.sum(jnp.where(real, new_ref[...], 0.0), axis=0, keepdims=True) / CMP_BLOCK
    new4 = jnp.concatenate([tot[:, c * NSA_DH:(c + 1) * NSA_DH] for c in range(NSA_ROW)]
                           + [jnp.zeros((8 - NSA_ROW, NSA_DH), F32)], axis=0)
    k_scr[n_past:n_past + 8, :] = new4

    q = q_ref[...].astype(BF16)
    tile = tile_ref[...]
    lg = _nt(q, k_scr[0:width, :].astype(BF16)) * scale + tile
    p = jnp.exp(lg - jnp.max(lg, axis=1, keepdims=True))
    p = p / jnp.sum(p, axis=1, keepdims=True)
    p = jnp.where(tile > 0.5 * NEG, p, 0.0)
    ocmp_ref[...] = jnp.dot(p.astype(BF16), k_scr[V_SHIFT:V_SHIFT + width, :].astype(BF16),
                            preferred_element_type=F32)

    lane = lax.broadcasted_iota(jnp.int32, (T8, width), 1)
    blk = lane // NSA_ROW
    cur = (past_len + lax.broadcasted_iota(jnp.int32, (T8, width), 0)) // SLC_BLOCK
    rows_g = NSA_HPG * T8
    for g in range(NSA_KV):
        mine = (lane % NSA_ROW) == g
        score = sum(p[g * rows_g + h * T8:g * rows_g + (h + 1) * T8] for h in range(NSA_HPG))
        forced = mine & ((blk == 0) | (blk == cur) | (blk == cur - 1))
        dead = (blk > cur) | jnp.logical_not(mine)
        sc = jnp.where(forced, jnp.inf, jnp.where(dead, -jnp.inf, score))
        taken = jnp.zeros((T8, width), jnp.bool_)
        for _ in range(SLC_TOPK):
            sm = jnp.where(taken, -jnp.inf, sc)
            cand = (sm == jnp.max(sm, axis=1, keepdims=True)) & jnp.logical_not(taken)
            idx = jnp.min(jnp.where(cand, lane, width), axis=1, keepdims=True)
            taken = taken | (lane == idx)
        sel_ref[g * T8:(g + 1) * T8, :] = taken.astype(F32)


def _cmp_select_decode(qg, kvc, kv_new, tile, past_len, n_new):
    batch, rows, _ = qg.shape
    width = NB_DEC * NSA_ROW
    n_past = kvc.shape[1]
    return pl.pallas_call(
        functools.partial(_cmp_select_decode_kernel, past_len=past_len, n_new=n_new),
        grid=(batch,),
        in_specs=[pl.BlockSpec((None, rows, NSA_DH), lambda b: (b, 0, 0)),
                  pl.BlockSpec((None, n_past, NSA_DH), lambda b: (b, 0, 0)),
                  pl.BlockSpec((None, T8, NSA_ROW * NSA_DH), lambda b: (b, 0, 0)),
                  pl.BlockSpec((rows, width), lambda b: (0, 0))],
        out_specs=[pl.BlockSpec((None, rows, NSA_DH), lambda b: (b, 0, 0)),
                   pl.BlockSpec((None, NSA_KV * T8, width), lambda b: (b, 0, 0))],
        out_shape=[jax.ShapeDtypeStruct((batch, rows, NSA_DH), F32),
                   jax.ShapeDtypeStruct((batch, NSA_KV * T8, width), F32)],
        scratch_shapes=[pltpu.VMEM((width + 8, NSA_DH), F32)],
        compiler_params=_cparams(("parallel",)),
        name="cmp_select_decode",
    )(qg, kvc, kv_new, tile)


def _masked_scores(st, sel, expand):
    rows, n = st.shape
    add = (jnp.dot(sel.astype(BF16), expand, preferred_element_type=F32) - 1.0) * (-NEG)
    st4 = st.reshape(NSA_KV, NSA_HPG, T8, n) + add.reshape(NSA_KV, 1, T8, n)
    return st4.reshape(rows, n)


def _slc_decode_kernel(pt_ref, q_ref, sel_ref, selnew_ref, e_ref, new_ref, tile_ref, newtile_ref, *rest):
    pages = rest[:NSA_PAGES_PER_STEP]
    o_ref, m_scr, l_scr, acc_scr = rest[NSA_PAGES_PER_STEP:]
    s = pl.program_id(1)
    last = pl.num_programs(1) - 1
    scale = NSA_DH ** -0.5
    width = PAGE_SIZE * NSA_ROW

    @pl.when(s == 0)
    def _():
        _flash_init(m_scr, l_scr, acc_scr)

    q = q_ref[...].astype(BF16)
    upd = functools.partial(_softmax_update, m_ref=m_scr, l_ref=l_scr, acc_ref=acc_scr)
    is_last = (s == last).astype(F32)
    tiles, values = [], []
    for k, pg in enumerate(pages):
        x = pg[...]
        st = _nt(q, x.astype(BF16)) * scale
        if k == NSA_PAGES_PER_STEP - 1:
            st = st + is_last * tile_ref[...]
        tiles.append(st)
        values.append(pltpu.roll(x, width - V_SHIFT, axis=0).astype(BF16))
    upd(_masked_scores(jnp.concatenate(tiles, axis=1), sel_ref[...], e_ref[...]), values, width)

    @pl.when(s == last)
    def _():
        x = new_ref[...]
        n = x.shape[0]
        st = _nt(q, x.astype(BF16)) * scale + newtile_ref[...]
        upd(_masked_scores(st, selnew_ref[...], e_ref[:, 0:n]), [pltpu.roll(x, n - V_SHIFT, axis=0).astype(BF16)], n)
        o_ref[...] = acc_scr[...] / l_scr[...]


def _slc_decode(page_table, qg, sel, expand, kv_new, tile, new_tile, cache, layer):
    batch, n_pages = page_table.shape
    steps = n_pages // NSA_PAGES_PER_STEP
    rows = qg.shape[1]
    keys = NSA_PAGES_PER_STEP * PAGE_SIZE * NSA_ROW
    grid_spec = pltpu.PrefetchScalarGridSpec(
        num_scalar_prefetch=1,
        grid=(batch, steps),
        in_specs=[pl.BlockSpec((None, rows, NSA_DH), lambda b, s, pt: (b, 0, 0)),
                  pl.BlockSpec((None, NSA_KV * T8, LANES), lambda b, s, pt: (b, 0, s)),
                  pl.BlockSpec((None, NSA_KV * T8, LANES), lambda b, s, pt: (b, 0, steps)),
                  pl.BlockSpec((LANES, keys), lambda b, s, pt: (0, 0)),
                  pl.BlockSpec((None,) + kv_new.shape[1:], lambda b, s, pt: (b, 0, 0)),
                  pl.BlockSpec(tile.shape, lambda b, s, pt: (0, 0)),
                  pl.BlockSpec(new_tile.shape, lambda b, s, pt: (0, 0))]
        + _nsa_page_specs(layer),
        out_specs=pl.BlockSpec((None, rows, NSA_DH), lambda b, s, pt: (b, 0, 0)),
        scratch_shapes=[pltpu.VMEM((rows, LANES), F32)] * 3,
    )
    return pl.pallas_call(
        _slc_decode_kernel,
        grid_spec=grid_spec,
        out_shape=jax.ShapeDtypeStruct((batch, rows, NSA_DH), F32),
        compiler_params=_cparams(("parallel", "arbitrary")),
        name="slc_decode",
    )(page_table, qg, sel, sel, expand, kv_new, tile, new_tile, *([cache] * NSA_PAGES_PER_STEP))


def _win_combine_decode_kernel(q_ref, state_ref, new_ref, tile_ref, gate_ref, ocmp_ref, oslc_ref, o_ref):
    scale = NSA_DH ** -0.5
    q = q_ref[...].astype(BF16)
    xs, xn = state_ref[...], new_ref[...]
    ns, nn = xs.shape[0], xn.shape[0]
    st = jnp.concatenate([_nt(q, xs.astype(BF16)), _nt(q, xn.astype(BF16))], axis=1) * scale + tile_ref[...]
    p = jnp.exp(st - jnp.max(st, axis=1, keepdims=True))
    p = (p / jnp.sum(p, axis=1, keepdims=True)).astype(BF16)
    o_win = (jnp.dot(p[:, 0:ns], pltpu.roll(xs, ns - V_SHIFT, axis=0).astype(BF16), preferred_element_type=F32)
             + jnp.dot(p[:, ns:], pltpu.roll(xn, nn - V_SHIFT, axis=0).astype(BF16), preferred_element_type=F32))
    sig = jax.nn.sigmoid(gate_ref[...])
    for g in range(NSA_KV):
        for h in range(NSA_HPG):
            head = g * NSA_HPG + h
            rows = slice(head * T8, (head + 1) * T8)
            gate = lambda branch: sig[:, branch * N_HEADS + head:branch * N_HEADS + head + 1]
            o = gate(0) * ocmp_ref[rows, :] + gate(1) * oslc_ref[rows, :] + gate(2) * o_win[rows]
            o_ref[:, head * NSA_DH:(head + 1) * NSA_DH] = o.astype(BF16)


def _win_combine_decode(qg, state_win, kv_new, tile, gates, o_cmp, o_slc):
    batch, rows, _ = qg.shape
    branch = pl.BlockSpec((None, rows, NSA_DH), lambda b: (b, 0, 0))
    return pl.pallas_call(
        _win_combine_decode_kernel,
        grid=(batch,),
        in_specs=[branch,
                  pl.BlockSpec((None,) + state_win.shape[1:], lambda b: (b, 0, 0)),
                  pl.BlockSpec((None,) + kv_new.shape[1:], lambda b: (b, 0, 0)),
                  pl.BlockSpec(tile.shape, lambda b: (0, 0)),
                  pl.BlockSpec((T8, LANES), lambda b: (b, COL_GATE)),
                  branch, branch],
        out_specs=pl.BlockSpec((T8, MIX_WIDTH), lambda b: (b, 0)),
        out_shape=jax.ShapeDtypeStruct((batch * T8, MIX_WIDTH), BF16),
        compiler_params=_cparams(("parallel",)),
        name="win_combine_decode",
    )(qg, state_win, kv_new, tile, gates, o_cmp, o_slc)


def _decode_buckets(past_len, wbuf):
    t = np.arange(T8)[:, None]
    dpos = np.repeat(np.arange(PAGE_SIZE), N_HEADS)[None, :]
    dnew = np.repeat(np.arange(NEW_ROWS), N_HEADS)[None, :]
    npos = np.repeat(np.arange(PAGE_SIZE), NSA_ROW)[None, :]
    nnew = np.repeat(np.arange(PAGE_SIZE // NSA_ROW), NSA_ROW)[None, :]
    nblk = np.repeat(np.arange(NB_DEC), NSA_ROW)[None, :]
    nwin = np.repeat(np.arange(wbuf), NSA_ROW)[None, :]
    d_cmp = past_len + t - ((nblk + 1) * CMP_BLOCK - 1)
    d_win = t + wbuf - nwin
    parts = [
        _bucket_or_masked(PAGE_SIZE + t - dpos, np.ones((T8, dpos.shape[1]), bool)),
        _bucket_or_masked(t - dnew, dnew <= t),
        _bucket_or_masked(PAGE_SIZE + t - npos, np.ones((T8, npos.shape[1]), bool)),
        _bucket_or_masked(t - nnew, nnew <= t),
        _bucket_or_masked(d_cmp, d_cmp >= 0),
        _bucket_or_masked(d_win, (d_win >= 0) & (d_win < WINDOW) & (past_len - wbuf + nwin >= 0)),
    ]
    return np.concatenate(parts, axis=1), np.cumsum([0] + [p.shape[1] for p in parts])


def _pad_rows(x, rows):
    return jnp.pad(x, ((0, 0), (0, rows - x.shape[1])) + ((0, 0),) * (x.ndim - 2))


def _tile_cfg():
    return dict(tm=1024, tn_even=512, tm_odd=512, tn_odd=1152, tn_out=1024, tm_mlp=512, tf=1024, tf_cast=512,
                tm_pool=512)


def kernel(x_prompt, x_sample, cache_diff_kv, cache_cmp_kv, cache_slc_kv, state_win_kv, state_pool, state_conv,
           page_table, rel_bias, norm_mix, norm_mlp, norm_final, even_w_in, even_w_out, diff_lambda, diff_subln,
           pool_w, pool_scale, odd_w_in, odd_w_out, conv_w, mlp_w1, mlp_w2):
    bp, seq, d = x_prompt.shape
    bs, ts, _ = x_sample.shape
    depth = norm_mix.shape[0]
    n_pages = page_table.shape[1]
    past_len = n_pages * PAGE_SIZE
    n_phys = cache_diff_kv.shape[1]
    wbuf = state_win_kv.shape[2]
    n_past_blk = past_len // SLC_BLOCK
    assert seq % TQ == 0 and ts <= T8 and n_pages % NSA_PAGES_PER_STEP == 0 and wbuf == WINDOW
    assert (past_len + ts - 1) // SLC_BLOCK == n_past_blk < NB_DEC
    cfg = _tile_cfg()

    w_in_o = [_odd_weight_layout(odd_w_in[o].astype(BF16)) for o in range(odd_w_in.shape[0])]
    pool_wb = pool_w.astype(BF16)

    tiles, cmp_t, expand = _nsa_buckets(seq, TQ, TK)
    tile_bias = _bias_tiles(rel_bias, tiles, LOG2E)
    cmp_bias_t = _bias_tiles(rel_bias, cmp_t)
    expand = jnp.asarray(expand, BF16)
    dec_buckets, off = _decode_buckets(past_len, wbuf)
    dec = _bias_tiles(rel_bias, dec_buckets)
    part = lambda i: dec[:, :, off[i]:off[i + 1]]

    def diff_rows(x):
        n = x.shape[-1]
        own = (np.arange(n) % N_HEADS)[None, None, :] == np.arange(N_HEADS)[:, None, None]
        x = jnp.where(own, x, NEG)
        return jnp.broadcast_to(x[None], (2,) + x.shape).reshape(2 * N_HEADS * T8, n)

    def nsa_rows(x):
        n = x.shape[-1]
        own = (np.arange(n) % NSA_ROW)[None, None, :] == (np.arange(N_HEADS) // NSA_HPG)[:, None, None]
        return jnp.where(own, x, NEG).reshape(N_HEADS * T8, n)

    diff_masks = jnp.stack([diff_rows(jnp.zeros_like(part(0))), diff_rows(part(0))])
    diff_new_mask = diff_rows(part(1))
    slc_tile = nsa_rows(part(2))
    nsa_new_tile = nsa_rows(part(3))
    cmp_tile = nsa_rows(part(4))
    win_tile = jnp.concatenate([nsa_rows(part(5)), nsa_new_tile], axis=1)
    key = np.arange(NSA_PAGES_PER_STEP * PAGE_SIZE * NSA_ROW)
    sel_lane = (key // (NSA_ROW * SLC_BLOCK)) * NSA_ROW + key % NSA_ROW
    is_key_row = (key % NSA_ROW) < NSA_KV
    expand_dec = jnp.asarray((np.arange(LANES)[:, None] == sel_lane[None, :]) & is_key_row[None, :], BF16)

    cache_cmp = cache_cmp_kv.reshape(cache_cmp_kv.shape[0], n_phys, PAGE_SIZE * NSA_ROW, NSA_DH)
    cache_slc = cache_slc_kv.reshape(cache_slc_kv.shape[0], n_phys, PAGE_SIZE * NSA_ROW, NSA_DH)

    kvc_all = _cmp_means(page_table, cache_cmp)
    mp, ms = bp * seq, bs * T8
    xp = x_prompt.reshape(mp, d)
    xs = _pad_rows(x_sample, T8).reshape(ms, d)
    tm_p = min(cfg["tm"], mp)
    outs = {k: [] for k in ("diff_s", "pool_p", "pool_s", "cmp_s", "slc_s", "win_s", "conv_p", "conv_s")}
    kvw = NSA_ROW * NSA_DH
    n_even, n_odd = (depth + 1) // 2, depth // 2
    even_taps = ((N_HEADS * DA_V, 2 * N_HEADS * DA_V, 1),)
    odd_taps = tuple((c * LANES, kvw, NSA_ROW) for c in (COL_KC, COL_KS, COL_KW))
    diff_kv_p, nsa_kv_p = (), ()

    for layer in range(depth):
        if layer % 2 == 0:
            e = layer // 2
            lam_init = 0.8 - 0.6 * math.exp(-0.3 * layer)
            zs, (w_in_b,) = _norm_matmul(xs, norm_mix[layer], even_w_in, ms, cfg["tn_even"], w_layer=e)
            zp, diff_kv_p = _norm_matmul(xp, norm_mix[layer], w_in_b, tm_p, cfg["tn_even"], even_taps, n_even, e,
                                         diff_kv_p)
            zp3 = zp.reshape(bp, seq, -1)
            zs3 = zs.reshape(bs, T8, -1)
            o_attn = _diff_attn_prompt(zp, diff_lambda[e], diff_subln[e], tile_bias, lam_init, bp, seq)
            o_pool = _pool_mix(zp, jnp.zeros((bp, HALO, MIX_WIDTH), F32), pool_wb[e], pool_scale[e], 0, bp, seq,
                               cfg["tm_pool"])
            q_ht = zs3[:, :, 0:1024].reshape(bs, T8, N_HEADS, DA_V).transpose(0, 2, 1, 3).reshape(
                bs, N_HEADS * T8, DA_V)
            kv_new = _pad_rows(zs3[:, :, 1024:3072].reshape(bs, T8, 2, N_HEADS, DA_V), NEW_ROWS)
            o_dec = _diff_decode(page_table, q_ht, kv_new, diff_masks, diff_new_mask, diff_lambda[e], diff_subln[e],
                                 cache_diff_kv, e, lam_init)
            o_attn_s = o_dec.reshape(bs, N_HEADS, T8, DA_V).transpose(0, 2, 1, 3).reshape(ms, N_HEADS * DA_V)
            prev = jnp.pad(state_pool[e], ((0, 0), (HALO - POOL_STATE, 0), (0, 0)))
            o_pool_s = _pool_mix(zs, prev, pool_wb[e], pool_scale[e], past_len, bs, T8, T8)
            xs, w_out_b = _out_proj(xs, o_attn_s.astype(BF16), o_pool_s, even_w_out, ms, cfg["tn_out"], w_layer=e)
            xp = _out_proj(xp, o_attn, o_pool, w_out_b, tm_p, cfg["tn_out"])
            outs["diff_s"].append(zs3[:, :ts, 1024:3072].reshape(bs, ts, 2, N_HEADS, DA_V))
            outs["pool_p"].append(zp3[:, seq - POOL_STATE:, 3072:])
            outs["pool_s"].append(jnp.concatenate([state_pool[e], zs3[:, :ts, 3072:]], axis=1)[:, -POOL_STATE:])
        else:
            o = layer // 2
            zp, nsa_kv_p = _norm_matmul(xp, norm_mix[layer], w_in_o[o], min(cfg["tm_odd"], mp), cfg["tn_odd"],
                                        odd_taps, n_odd, o, nsa_kv_p)
            zs, _ = _norm_matmul(xs, norm_mix[layer], w_in_o[o], ms, cfg["tn_odd"])
            zp3 = zp.reshape(bp, seq, -1)
            zs3 = zs.reshape(bs, T8, -1)
            c_kc, c_ks, c_kw = COL_KC * LANES, COL_KS * LANES, COL_KW * LANES
            o_nsa = _nsa_prompt(zp, tile_bias, cmp_bias_t, expand, bp, seq)
            o_conv, tail = _short_conv(zp, jnp.zeros((bp, 8, MIX_WIDTH), F32), conv_w[o], bp, seq, cfg["tm_pool"])
            qg = zs3[:, :, 0:1024].reshape(bs, T8, N_HEADS, NSA_DH).transpose(0, 2, 1, 3).reshape(
                bs, N_HEADS * T8, NSA_DH)
            new_rows = lambda c0: _pad_rows(zs3[:, :, c0:c0 + kvw].reshape(bs, T8 * NSA_ROW, NSA_DH), PAGE_SIZE)
            kvc = kvc_all[o]
            o_cmp, sel = _cmp_select_decode(qg, kvc, zs3[:, :, c_kc:c_kc + kvw], cmp_tile, past_len, ts)
            o_slc = _slc_decode(page_table, qg, sel, expand_dec, new_rows(c_ks), slc_tile, nsa_new_tile, cache_slc, o)
            o_nsa_s = _win_combine_decode(qg, state_win_kv[o].reshape(bs, wbuf * NSA_ROW, NSA_DH), new_rows(c_kw),
                                          win_tile, zs, o_cmp, o_slc)
            prev = jnp.pad(state_conv[o], ((0, 0), (8 - (CONV_WIDTH - 1), 0), (0, 0)))
            o_conv_s, tail_s = _short_conv(zs, prev, conv_w[o], bs, T8, T8)
            xs, w_out_b = _out_proj(xs, o_nsa_s, o_conv_s, odd_w_out, ms, cfg["tn_out"], w_layer=o)
            xp = _out_proj(xp, o_nsa, o_conv, w_out_b, tm_p, cfg["tn_out"])
            kv5 = lambda a, n: a.reshape(a.shape[0], n, 2, NSA_KV, NSA_DH)
            outs["cmp_s"].append(kv5(zs3[:, :ts, c_kc:c_kc + kvw], ts))
            outs["slc_s"].append(kv5(zs3[:, :ts, c_ks:c_ks + kvw], ts))
            n_win = min(WINDOW, past_len + ts)
            outs["win_s"].append(jnp.concatenate([state_win_kv[o], kv5(zs3[:, :ts, c_kw:c_kw + kvw], ts)],
                                                 axis=1)[:, -n_win:])
            outs["conv_p"].append(tail[:, 8 - (CONV_WIDTH - 1):])
            outs["conv_s"].append(tail_s[:, ts - (CONV_WIDTH - 1):ts])
        last = layer == depth - 1
        xs, w1b, w2b = _mlp(xs, norm_mlp[layer], mlp_w1, mlp_w2, norm_final, ms, cfg["tf_cast"], last, layer=layer)
        xp = _mlp(xp, norm_mlp[layer], w1b, w2b, norm_final, min(cfg["tm_mlp"], mp), cfg["tf"], last)

    st = {k: jnp.stack(v) for k, v in outs.items()}
    st["diff_p"] = diff_kv_p[0].reshape(n_even, bp, seq, 2, N_HEADS, DA_V)
    nsa6 = lambda a: a.reshape(n_odd, bp, seq, 2, NSA_KV, NSA_DH)
    st["cmp_p"], st["slc_p"] = nsa6(nsa_kv_p[0]), nsa6(nsa_kv_p[1])
    st["win_p"] = nsa6(nsa_kv_p[2])[:, :, seq - min(WINDOW, seq):]
    return (xp.reshape(bp, seq, d), xs.reshape(bs, T8, d)[:, :ts],
            st["diff_p"], st["diff_s"], st["pool_p"], st["pool_s"], st["cmp_p"], st["cmp_s"],
            st["slc_p"], st["slc_s"], st["win_p"], st["win_s"], st["conv_p"], st["conv_s"])
```
